```python
import jax, jax.numpy as jnp
from jax import lax
import numpy as np

D_MODEL = 4096
BATCH = 2
SEQ = 8192
DEPTH = 1
DEC_BATCH = 8
DEC_SEQ = 64
PAST_LEN = 1024

CHUNK = 64
CONV_K = 4
SSD_HEAD_DIM = 64
SSD_HEADS = D_MODEL // SSD_HEAD_DIM
SSD_WIDTH = SSD_HEADS * SSD_HEAD_DIM
SSD_GROUPS = 8
SSD_HEADS_PER_GROUP = SSD_HEADS // SSD_GROUPS
SSD_STATE = 128
SSD_CONV_DIM = SSD_WIDTH + 2 * SSD_GROUPS * SSD_STATE
GDN_HEAD_DIM = 128
GDN_HEADS = D_MODEL // GDN_HEAD_DIM
GDN_WIDTH = GDN_HEADS * GDN_HEAD_DIM
GDN_CONV_DIM = 3 * GDN_WIDTH
MIX_WIDTH = SSD_WIDTH + GDN_WIDTH
IN_SIZES = (SSD_WIDTH, SSD_CONV_DIM, SSD_HEADS, GDN_CONV_DIM, GDN_WIDTH, GDN_HEADS, GDN_HEADS)
IN_PROJ_DIM = SSD_WIDTH + SSD_CONV_DIM + SSD_HEADS + GDN_CONV_DIM + GDN_WIDTH + 2 * GDN_HEADS
N_EXPERTS = 32
TOP_K = 4
D_FF = D_MODEL
SWIGLU_LIMIT = 7.0
SWIGLU_ALPHA = 1.702
MOE_BLOCK = 256
EPS = 1e-6

kernel_name = 'hymba_ssd_gdn_moe_stream_step'


def rmsnorm(x, w):
    xf = x.astype(jnp.float32)
    xf = xf * lax.rsqrt(jnp.mean(xf * xf, axis=-1, keepdims=True) + EPS)
    return xf.astype(x.dtype) * w


def l2norm(x):
    return x * lax.rsqrt(jnp.sum(x * x, axis=-1, keepdims=True) + EPS)


def causal_conv(u, prev, w):
    L = u.shape[1]
    up = jnp.concatenate([prev.astype(u.dtype), u], axis=1)
    y = up[:, 0:L] * w[0]
    for j in range(1, CONV_K):
        y = y + up[:, j:j + L] * w[j]
    return y, up[:, L:]


def chunked_scan(step, state, xs):
    L = xs[0].shape[1]
    if L <= CHUNK:
        return step(state, xs)
    nc = L // CHUNK
    xs_c = tuple(jnp.moveaxis(a.reshape(a.shape[0], nc, CHUNK, *a.shape[2:]), 1, 0) for a in xs)
    state, ys = lax.scan(step, state, xs_c)
    ys = jnp.moveaxis(ys, 0, 1)
    return state, ys.reshape(ys.shape[0], L, *ys.shape[3:])


def ssd_mixer(z, xbc, dt_raw, conv_prev, state_prev, conv_w, conv_b, dt_bias, A_log, D_skip, norm_w):
    bsz, L, _ = z.shape
    G, R, P, N = SSD_GROUPS, SSD_HEADS_PER_GROUP, SSD_HEAD_DIM, SSD_STATE
    xbc_c, conv_new = causal_conv(xbc, conv_prev, conv_w)
    xbc_c = jax.nn.silu(xbc_c + conv_b).astype(jnp.float32)
    xs = xbc_c[..., :SSD_WIDTH].reshape(bsz, L, G, R, P)
    Bm = xbc_c[..., SSD_WIDTH:SSD_WIDTH + G * N].reshape(bsz, L, G, N)
    Cm = xbc_c[..., SSD_WIDTH + G * N:].reshape(bsz, L, G, N)
    dt = jax.nn.softplus((dt_raw + dt_bias).astype(jnp.float32)).reshape(bsz, L, G, R)
    A = -jnp.exp(A_log.astype(jnp.float32)).reshape(G, R)

    def step(S, inp):
        xc, dtc, Bc, Cc = inp
        Lc = xc.shape[1]
        causal = jnp.tril(jnp.ones((Lc, Lc), bool))
        cum = jnp.cumsum(dtc * A, axis=1)
        seg = cum[:, :, None] - cum[:, None, :]
        decay = jnp.exp(jnp.where(causal[None, :, :, None, None], seg, -jnp.inf))
        cb = jnp.einsum('btgn,bsgn->btsg', Cc, Bc)
        wts = cb[..., None] * decay * dtc[:, None]
        y = jnp.einsum('btsgr,bsgrp->btgrp', wts, xc)
        y = y + jnp.einsum('btgn,bgrpn->btgrp', Cc, S) * jnp.exp(cum)[..., None]
        to_end = jnp.exp(cum[:, -1:] - cum) * dtc
        S_new = jnp.exp(cum[:, -1])[..., None, None] * S + jnp.einsum('bsgr,bsgn,bsgrp->bgrpn', to_end, Bc, xc)
        return S_new, y

    S0 = state_prev.astype(jnp.float32).reshape(bsz, G, R, P, N)
    S_new, y = chunked_scan(step, S0, (xs, dt, Bm, Cm))
    y = y + D_skip.astype(jnp.float32).reshape(G, R)[:, :, None] * xs
    y = y.reshape(bsz, L, SSD_WIDTH) * jax.nn.silu(z.astype(jnp.float32))
    yg = y.reshape(bsz, L, G, SSD_WIDTH // G)
    yg = yg * lax.rsqrt(jnp.mean(yg * yg, axis=-1, keepdims=True) + EPS)
    y = yg.reshape(bsz, L, SSD_WIDTH).astype(z.dtype) * norm_w
    return y, conv_new, S_new.reshape(bsz, SSD_HEADS, P, N).astype(state_prev.dtype)


def gdn_step(S, inp):
    q, k, v, beta, g = inp
    q, k, v = (jnp.swapaxes(a, 1, 2) for a in (q, k, v))
    beta, g = jnp.swapaxes(beta, 1, 2), jnp.swapaxes(g, 1, 2)
    Lc = q.shape[2]
    incl = jnp.tril(jnp.ones((Lc, Lc), bool))
    strict = jnp.tril(jnp.ones((Lc, Lc), bool), -1)
    gam = jnp.cumsum(g, axis=-1)
    Gam = jnp.exp(jnp.where(incl, gam[..., :, None] - gam[..., None, :], -jnp.inf))
    kk = jnp.einsum('bhtk,bhsk->bhts', k, k)
    A_mat = jnp.where(strict, beta[..., :, None] * kk * Gam, 0.0)
    T_mat = A_mat + jnp.eye(Lc, dtype=jnp.float32)
    rhs = jnp.concatenate([v * beta[..., None], k * (beta * jnp.exp(gam))[..., None]], axis=-1)
    sol = lax.linalg.triangular_solve(T_mat, rhs, left_side=True, lower=True, unit_diagonal=True)
    u, w = sol[..., :GDN_HEAD_DIM], sol[..., GDN_HEAD_DIM:]
    v_new = u - jnp.einsum('bhtk,bhkv->bhtv', w, S)
    qk = jnp.einsum('bhtk,bhsk->bhts', q, k) * Gam
    o = jnp.einsum('bhtk,bhkv->bhtv', q * jnp.exp(gam)[..., None], S) + jnp.einsum('bhts,bhsv->bhtv', qk, v_new)
    to_end = jnp.exp(gam[..., -1:] - gam)
    S_new = jnp.exp(gam[..., -1])[..., None, None] * S + jnp.einsum('bhsk,bhsv->bhkv', k * to_end[..., None], v_new)
    return S_new, jnp.swapaxes(o, 1, 2)


def gdn_mixer(qkv, z, b_raw, a_raw, conv_prev, state_prev, conv_w, dt_bias, A_log, norm_w):
    bsz, L, _ = qkv.shape
    qkv_c, conv_new = causal_conv(qkv, conv_prev, conv_w)
    qkv_c = jax.nn.silu(qkv_c).astype(jnp.float32).reshape(bsz, L, 3, GDN_HEADS, GDN_HEAD_DIM)
    q = l2norm(qkv_c[:, :, 0]) * GDN_HEAD_DIM ** -0.5
    k = l2norm(qkv_c[:, :, 1])
    v = qkv_c[:, :, 2]
    beta = jax.nn.sigmoid(b_raw.astype(jnp.float32))
    g = -jnp.exp(A_log.astype(jnp.float32)) * jax.nn.softplus((a_raw + dt_bias).astype(jnp.float32))
    S_new, o = chunked_scan(gdn_step, state_prev.astype(jnp.float32), (q, k, v, beta, g))
    o = o * lax.rsqrt(jnp.mean(o * o, axis=-1, keepdims=True) + EPS)
    o = (o.astype(z.dtype) * norm_w) * jax.nn.silu(z.reshape(bsz, L, GDN_HEADS, GDN_HEAD_DIM))
    return o.reshape(bsz, L, GDN_WIDTH), conv_new, S_new.astype(state_prev.dtype)


def moe(h, w_router, b_router, w_gate, b_gate, w_up, b_up, w_down, b_down):
    T = h.shape[0]
    logits = (h @ w_router + b_router).astype(jnp.float32)
    top_val, top_idx = lax.top_k(logits, TOP_K)
    gates = jax.nn.softmax(top_val, axis=-1).astype(h.dtype)
    TK = T * TOP_K
    flat_e = top_idx.reshape(TK)
    flat_tok = jnp.arange(TK, dtype=jnp.int32) // TOP_K
    flat_g = gates.reshape(TK)
    order = jnp.argsort(flat_e)
    sorted_e = flat_e[order]
    counts = jnp.bincount(flat_e, length=N_EXPERTS)
    starts = jnp.cumsum(counts) - counts
    pcounts = (counts + MOE_BLOCK - 1) // MOE_BLOCK * MOE_BLOCK
    pends = jnp.cumsum(pcounts)
    pstarts = pends - pcounts
    dest = pstarts[sorted_e] + (jnp.arange(TK, dtype=jnp.int32) - starts[sorted_e])
    n_blocks = -(-TK // MOE_BLOCK) + N_EXPERTS
    P = n_blocks * MOE_BLOCK
    slot_tok = jnp.full((P,), T, jnp.int32).at[dest].set(flat_tok[order])
    slot_g = jnp.zeros((P,), h.dtype).at[dest].set(flat_g[order])
    blk_e = jnp.minimum(jnp.searchsorted(pends, jnp.arange(n_blocks, dtype=jnp.int32) * MOE_BLOCK, side='right'), N_EXPERTS - 1)
    h_pad = jnp.concatenate([h, jnp.zeros((1, h.shape[1]), h.dtype)], axis=0)

    def body(out, blk):
        tok, gw, e = blk
        xb = h_pad[tok]
        gate = jnp.minimum(xb @ w_gate[e] + b_gate[e], SWIGLU_LIMIT)
        up = jnp.clip(xb @ w_up[e] + b_up[e], -SWIGLU_LIMIT, SWIGLU_LIMIT)
        act = gate * jax.nn.sigmoid(SWIGLU_ALPHA * gate) * (up + 1.0)
        yb = act @ w_down[e] + b_down[e]
        return out.at[tok].add(yb * gw[:, None]), None

    out0 = jnp.zeros((T + 1, h.shape[1]), h.dtype)
    out, _ = lax.scan(body, out0, (slot_tok.reshape(n_blocks, MOE_BLOCK), slot_g.reshape(n_blocks, MOE_BLOCK), blk_e))
    return out[:T]


def hybrid_layer(x, ssd_conv_prev, ssd_prev, gdn_conv_prev, gdn_prev,
                 norm_mix, w_in, ssd_conv_w, ssd_conv_b, ssd_dt_bias, ssd_A_log, ssd_D, ssd_norm,
                 gdn_conv_w, gdn_dt_bias, gdn_A_log, gdn_norm, w_out,
                 norm_ffn, w_router, b_router, w_gate, b_gate, w_up, b_up, w_down, b_down):
    bsz, L, _ = x.shape
    h = rmsnorm(x, norm_mix)
    proj = h @ w_in
    splits = [int(s) for s in np.cumsum(IN_SIZES)[:-1]]
    z_ssd, xbc, dt_raw, qkv, z_gdn, b_raw, a_raw = jnp.split(proj, splits, axis=-1)
    y_ssd, ssd_conv_new, ssd_new = ssd_mixer(z_ssd, xbc, dt_raw, ssd_conv_prev, ssd_prev, ssd_conv_w, ssd_conv_b,
                                             ssd_dt_bias, ssd_A_log, ssd_D, ssd_norm)
    y_gdn, gdn_conv_new, gdn_new = gdn_mixer(qkv, z_gdn, b_raw, a_raw, gdn_conv_prev, gdn_prev, gdn_conv_w,
                                             gdn_dt_bias, gdn_A_log, gdn_norm)
    x = x + jnp.concatenate([y_ssd, y_gdn], axis=-1) @ w_out
    h2 = rmsnorm(x, norm_ffn).reshape(bsz * L, D_MODEL)
    x = x + moe(h2, w_router, b_router, w_gate, b_gate, w_up, b_up, w_down, b_down).reshape(bsz, L, D_MODEL)
    return x, ssd_conv_new, ssd_new, gdn_conv_new, gdn_new


def setup_inputs(seed: int = 0) -> dict:
    key = jax.random.key(seed)
    ks = jax.random.split(key, 32)
    f32 = jnp.float32
    nrm = lambda k, shape, s: jax.random.normal(k, shape, f32) * s

    def dt_bias_init(k, n):
        dt = jnp.exp(jax.random.uniform(k, (DEPTH, n), f32, np.log(1e-3), np.log(1e-1)))
        return dt + jnp.log(-jnp.expm1(-dt))

    return {
        'x_prompt': nrm(ks[0], (BATCH, SEQ, D_MODEL), 1.0),
        'x_sample': nrm(ks[1], (DEC_BATCH, DEC_SEQ, D_MODEL), 1.0),
        'state_ssd_conv': nrm(ks[2], (DEPTH, DEC_BATCH, CONV_K - 1, SSD_CONV_DIM), 1.0),
        'state_ssd': nrm(ks[3], (DEPTH, DEC_BATCH, SSD_HEADS, SSD_HEAD_DIM, SSD_STATE), 0.1),
        'state_gdn_conv': nrm(ks[4], (DEPTH, DEC_BATCH, CONV_K - 1, GDN_CONV_DIM), 1.0),
        'state_gdn': nrm(ks[5], (DEPTH, DEC_BATCH, GDN_HEADS, GDN_HEAD_DIM, GDN_HEAD_DIM), 0.1),
        'norm_mix': 1.0 + nrm(ks[6], (DEPTH, D_MODEL), 0.05),
        'w_in': nrm(ks[7], (DEPTH, D_MODEL, IN_PROJ_DIM), D_MODEL ** -0.5),
        'ssd_conv_w': nrm(ks[8], (DEPTH, CONV_K, SSD_CONV_DIM), CONV_K ** -0.5),
        'ssd_conv_b': nrm(ks[9], (DEPTH, SSD_CONV_DIM), 0.02),
        'ssd_dt_bias': dt_bias_init(ks[10], SSD_HEADS),
        'ssd_A_log': jnp.log(jax.random.uniform(ks[11], (DEPTH, SSD_HEADS), f32, 1.0, 16.0)),
        'ssd_D': 1.0 + nrm(ks[12], (DEPTH, SSD_HEADS), 0.1),
        'ssd_norm': 1.0 + nrm(ks[13], (DEPTH, SSD_WIDTH), 0.05),
        'gdn_conv_w': nrm(ks[14], (DEPTH, CONV_K, GDN_CONV_DIM), CONV_K ** -0.5),
        'gdn_dt_bias': dt_bias_init(ks[15], GDN_HEADS),
        'gdn_A_log': jnp.log(jax.random.uniform(ks[16], (DEPTH, GDN_HEADS), f32, 1.0, 16.0)),
        'gdn_norm': 1.0 + nrm(ks[17], (DEPTH, GDN_HEAD_DIM), 0.05),
        'w_out': nrm(ks[18], (DEPTH, MIX_WIDTH, D_MODEL), MIX_WIDTH ** -0.5),
        'norm_ffn': 1.0 + nrm(ks[19], (DEPTH, D_MODEL), 0.05),
        'w_router': nrm(ks[20], (DEPTH, D_MODEL, N_EXPERTS), D_MODEL ** -0.5),
        'b_router': nrm(ks[21], (DEPTH, N_EXPERTS), 0.01),
        'w_gate': nrm(ks[22], (DEPTH, N_EXPERTS, D_MODEL, D_FF), D_MODEL ** -0.5),
        'b_gate': nrm(ks[23], (DEPTH, N_EXPERTS, D_FF), 0.01),
        'w_up': nrm(ks[24], (DEPTH, N_EXPERTS, D_MODEL, D_FF), D_MODEL ** -0.5),
        'b_up': nrm(ks[25], (DEPTH, N_EXPERTS, D_FF), 0.01),
        'w_down': nrm(ks[26], (DEPTH, N_EXPERTS, D_FF, D_MODEL), D_FF ** -0.5),
        'b_down': nrm(ks[27], (DEPTH, N_EXPERTS, D_MODEL), 0.01),
        'norm_final': 1.0 + nrm(ks[28], (D_MODEL,), 0.05),
    }


def reference(x_prompt, x_sample, state_ssd_conv, state_ssd, state_gdn_conv, state_gdn,
              norm_mix, w_in, ssd_conv_w, ssd_conv_b, ssd_dt_bias, ssd_A_log, ssd_D, ssd_norm,
              gdn_conv_w, gdn_dt_bias, gdn_A_log, gdn_norm, w_out,
              norm_ffn, w_router, b_router, w_gate, b_gate, w_up, b_up, w_down, b_down, norm_final):
    nb = x_prompt.shape[0]
    yp, ys = x_prompt, x_sample
    p_conv_ssd, p_ssd, p_conv_gdn, p_gdn = [], [], [], []
    s_conv_ssd, s_ssd, s_conv_gdn, s_gdn = [], [], [], []
    for l in range(DEPTH):
        lw = (norm_mix[l], w_in[l], ssd_conv_w[l], ssd_conv_b[l], ssd_dt_bias[l], ssd_A_log[l], ssd_D[l], ssd_norm[l],
              gdn_conv_w[l], gdn_dt_bias[l], gdn_A_log[l], gdn_norm[l], w_out[l],
              norm_ffn[l], w_router[l], b_router[l], w_gate[l], b_gate[l], w_up[l], b_up[l], w_down[l], b_down[l])
        zc_ssd = jnp.zeros((nb, CONV_K - 1, SSD_CONV_DIM), x_prompt.dtype)
        z_ssd = jnp.zeros((nb, SSD_HEADS, SSD_HEAD_DIM, SSD_STATE), x_prompt.dtype)
        zc_gdn = jnp.zeros((nb, CONV_K - 1, GDN_CONV_DIM), x_prompt.dtype)
        z_gdn = jnp.zeros((nb, GDN_HEADS, GDN_HEAD_DIM, GDN_HEAD_DIM), x_prompt.dtype)
        yp, c1, s1, c2, s2 = hybrid_layer(yp, zc_ssd, z_ssd, zc_gdn, z_gdn, *lw)
        p_conv_ssd.append(c1); p_ssd.append(s1); p_conv_gdn.append(c2); p_gdn.append(s2)
        ys, c1, s1, c2, s2 = hybrid_layer(ys, state_ssd_conv[l], state_ssd[l], state_gdn_conv[l], state_gdn[l], *lw)
        s_conv_ssd.append(c1); s_ssd.append(s1); s_conv_gdn.append(c2); s_gdn.append(s2)
    y_prompt = rmsnorm(yp, norm_final)
    y_sample = rmsnorm(ys, norm_final)
    return (y_prompt, y_sample,
            jnp.stack(p_conv_ssd), jnp.stack(p_ssd), jnp.stack(p_conv_gdn), jnp.stack(p_gdn),
            jnp.stack(s_conv_ssd), jnp.stack(s_ssd), jnp.stack(s_conv_gdn), jnp.stack(s_gdn))
```

```python
import functools

import numpy as np
import jax
import jax.numpy as jnp
from jax import lax
from jax.experimental import pallas as pl
from jax.experimental.pallas import tpu as pltpu

F32 = jnp.float32
BF16 = jnp.bfloat16
I32 = jnp.int32

LANE = 128
SUBLANE = 8
VMEM_LIMIT = 56 * 1024 * 1024

CHUNK = 64
CONV_K = 4
SSD_P = 64
SSD_N = 128
SSD_G = 8
GDN_D = 128
N_EXPERTS = 32
TOP_K = 4
SWIGLU_LIMIT = 7.0
SWIGLU_ALPHA = 1.702
EPS = 1e-6
NEG_BIG = -1e30

MOE_BM = 256
GDN_HB = 8
HIGHEST = lax.Precision.HIGHEST


def _row_tile(n, target, mult=16):
    best = None
    for t in range(mult, min(n, target) + 1, mult):
        if n % t == 0:
            best = t
    assert best is not None, (n, target)
    return best


def _params(sem):
    return pltpu.CompilerParams(dimension_semantics=sem, vmem_limit_bytes=VMEM_LIMIT)


def _silu(x):
    return x * jax.nn.sigmoid(x)


def _softplus(x):
    return jnp.maximum(x, 0.0) + jnp.log1p(jnp.exp(-jnp.abs(x)))


def _dot(a, b, **kw):
    return jnp.dot(a, b, preferred_element_type=F32, **kw)


def _split3(a):
    hi = a.astype(BF16)
    r = a - hi.astype(F32)
    mid = r.astype(BF16)
    lo = (r - mid.astype(F32)).astype(BF16)
    return hi, mid, lo


def _rmsnorm_cast_kernel(x_ref, w_ref, o_ref):
    x = x_ref[...]
    xn = x * lax.rsqrt(jnp.mean(x * x, axis=-1, keepdims=True) + EPS)
    o_ref[...] = (xn * w_ref[...]).astype(o_ref.dtype)


def _rmsnorm_cast(x, w, tm):
    t, d = x.shape
    return pl.pallas_call(
        _rmsnorm_cast_kernel,
        grid=(t // tm,),
        in_specs=[pl.BlockSpec((tm, d), lambda i: (i, 0)), pl.BlockSpec((1, d), lambda i: (0, 0))],
        out_specs=pl.BlockSpec((tm, d), lambda i: (i, 0)),
        out_shape=jax.ShapeDtypeStruct((t, d), BF16),
        compiler_params=_params(("parallel",)),
        name="rmsnorm_cast",
    )(x, w.reshape(1, d))


def _inproj_kernel(a_ref, w_ref, o_ref):
    ncb = o_ref.shape[0]
    a = a_ref[...]
    step = 2 if ncb % 2 == 0 else 1
    for j in range(0, ncb, step):
        acc = _dot(a, w_ref[:, j * LANE:(j + step) * LANE])
        for s in range(step):
            o_ref[j + s] = acc[:, s * LANE:(s + 1) * LANE]


def _inproj(a, w, tm, tn_cb):
    t, d = a.shape
    n = w.shape[1]
    ncb = n // LANE
    return pl.pallas_call(
        _inproj_kernel,
        grid=(t // tm, ncb // tn_cb),
        in_specs=[pl.BlockSpec((tm, d), lambda i, j: (i, 0)),
                  pl.BlockSpec((d, tn_cb * LANE), lambda i, j: (0, j))],
        out_specs=pl.BlockSpec((tn_cb, tm, LANE), lambda i, j: (j, i, 0)),
        out_shape=jax.ShapeDtypeStruct((ncb, t, LANE), F32),
        compiler_params=_params(("parallel", "arbitrary")),
        name="inproj",
    )(a, w)


def _causal_conv(u, ext_scr, cin_ref, cw_ref, is_start):
    L = u.shape[1]
    base = SUBLANE - (CONV_K - 1)

    @pl.when(is_start)
    def _():
        ext_scr[:, base:SUBLANE, :] = cin_ref[0, 0]

    ext_scr[:, SUBLANE:SUBLANE + L, :] = u
    acc = ext_scr[:, base:base + L, :] * cw_ref[0, 0]
    for j in range(1, CONV_K):
        acc = acc + ext_scr[:, base + j:base + j + L, :] * cw_ref[0, j]
    ext_scr[:, base:SUBLANE, :] = ext_scr[:, base + L:SUBLANE + L, :]
    return acc


def _ssd_kernel(seq_ref, start_ref, end_ref,
                z_ref, x_ref, b_ref, c_ref, sm_ref, cin_ref, sin_ref, cw_ref, cb_ref,
                dtb_ref, alog_ref, d_ref, nw_ref,
                y_ref, sout_ref,
                ext_scr, s_scr):
    g = pl.program_id(0)
    c = pl.program_id(1)
    rb = x_ref.shape[0]
    r_heads = 2 * rb
    L = x_ref.shape[1]
    is_start = start_ref[c] == 1

    @pl.when(is_start)
    def _():
        s_scr[...] = sin_ref[0]

    u = jnp.concatenate([x_ref[...], b_ref[...], c_ref[...]], axis=0)
    uc = _silu(_causal_conv(u, ext_scr, cin_ref, cw_ref, is_start) + cb_ref[0])
    bm = uc[rb]
    cm = uc[rb + 1]

    sm = sm_ref[0]
    dt_all = _softplus(sm + dtb_ref[...])
    a_all = dt_all * (-jnp.exp(alog_ref[...]))
    ti = lax.broadcasted_iota(I32, (L, L), 0)
    si = lax.broadcasted_iota(I32, (L, L), 1)
    tril = (ti >= si).astype(F32)
    cum_all = _dot(tril, a_all, precision=HIGHEST)
    ej = lax.broadcasted_iota(I32, (LANE, rb * LANE), 0)
    ec = lax.broadcasted_iota(I32, (LANE, rb * LANE), 1)
    expand = (ej == g * r_heads + jnp.right_shift(ec, 6)).astype(F32)
    dtx = _dot(dt_all, expand, precision=HIGHEST)
    cumx = _dot(cum_all, expand, precision=HIGHEST)

    t2 = lax.broadcasted_iota(I32, (L, LANE), 0)
    l2 = lax.broadcasted_iota(I32, (L, LANE), 1)
    s2 = jnp.bitwise_and(l2, SSD_P - 1)
    diag2 = (t2 == s2).astype(F32)
    causal2 = t2 >= s2
    left = l2 < SSD_P

    b2 = jnp.concatenate([bm, bm], axis=0).astype(BF16)
    cm_b = cm.astype(BF16)
    bm_b = bm.astype(BF16)
    cb2 = lax.dot_general(cm_b, b2, (((1,), (1,)), ((), ())), preferred_element_type=F32)

    ygs = []
    ms = jnp.zeros((L, 1), F32)
    for j in range(rb):
        ccol = cumx[:, j * LANE:(j + 1) * LANE]
        dtc = dtx[:, j * LANE:(j + 1) * LANE]
        crow = jnp.sum(ccol * diag2, axis=0, keepdims=True)
        dec = jnp.exp(jnp.where(causal2, ccol - crow, NEG_BIG))
        ww = (cb2 * dec).astype(BF16)
        xb = uc[j]
        xdt = xb * dtc
        xbd = jnp.concatenate([jnp.where(left, xdt, 0.0), jnp.where(left, 0.0, xdt)], axis=0).astype(BF16)
        y = _dot(ww, xbd)
        sj = s_scr[j]
        y = y + _dot(cm_b, sj.astype(BF16)) * jnp.exp(ccol)
        y = y + d_ref[j] * xb
        cl = ccol[L - 1:L, :]
        xw = (xdt * jnp.exp(cl - ccol)).astype(BF16)
        s_scr[j] = sj * jnp.exp(cl) + lax.dot_general(
            bm_b, xw, (((0,), (0,)), ((), ())), preferred_element_type=F32)
        yg = y * _silu(z_ref[j])
        ms = ms + jnp.sum(yg * yg, axis=-1, keepdims=True)
        ygs.append(yg)
    inv = lax.rsqrt(ms / (rb * LANE) + EPS)
    for j in range(rb):
        y_ref[:, j * LANE:(j + 1) * LANE] = ((ygs[j] * inv) * nw_ref[j]).astype(y_ref.dtype)

    @pl.when(end_ref[c] == 1)
    def _():
        sout_ref[0] = s_scr[...]


def _ssd_mixer(proj, seq_id, start, end, conv_in, state_in, conv_w, conv_b, dtb, alog, d_exp, nw,
               t, width, cb_z, cb_x, cb_b, cb_c, cb_sm):
    rb = width // LANE // SSD_G
    nb = rb + 2
    nc = t // CHUNK
    nseq = state_in.shape[0]
    L = CHUNK

    def im(f):
        return lambda g, c, s, st, en: f(g, c, s)

    grid_spec = pltpu.PrefetchScalarGridSpec(
        num_scalar_prefetch=3,
        grid=(SSD_G, nc),
        in_specs=[
            pl.BlockSpec((rb, L, LANE), im(lambda g, c, s: (cb_z // rb + g, c, 0))),
            pl.BlockSpec((rb, L, LANE), im(lambda g, c, s: (cb_x // rb + g, c, 0))),
            pl.BlockSpec((1, L, LANE), im(lambda g, c, s: (cb_b + g, c, 0))),
            pl.BlockSpec((1, L, LANE), im(lambda g, c, s: (cb_c + g, c, 0))),
            pl.BlockSpec((1, L, LANE), im(lambda g, c, s: (cb_sm, c, 0))),
            pl.BlockSpec((1, 1, nb, CONV_K - 1, LANE), im(lambda g, c, s: (s[c], g, 0, 0, 0))),
            pl.BlockSpec((1, rb, SSD_N, LANE), im(lambda g, c, s: (s[c], g, 0, 0))),
            pl.BlockSpec((1, CONV_K, nb, 1, LANE), im(lambda g, c, s: (g, 0, 0, 0, 0))),
            pl.BlockSpec((1, nb, 1, LANE), im(lambda g, c, s: (g, 0, 0, 0))),
            pl.BlockSpec((1, LANE), im(lambda g, c, s: (0, 0))),
            pl.BlockSpec((1, LANE), im(lambda g, c, s: (0, 0))),
            pl.BlockSpec((rb, 1, LANE), im(lambda g, c, s: (g, 0, 0))),
            pl.BlockSpec((rb, 1, LANE), im(lambda g, c, s: (g, 0, 0))),
        ],
        out_specs=[
            pl.BlockSpec((L, rb * LANE), im(lambda g, c, s: (c, g))),
            pl.BlockSpec((1, rb, SSD_N, LANE), im(lambda g, c, s: (s[c], g, 0, 0))),
        ],
        scratch_shapes=[pltpu.VMEM((nb, L + SUBLANE, LANE), F32), pltpu.VMEM((rb, SSD_N, LANE), F32)],
    )
    assert cb_z % rb == 0 and cb_x % rb == 0
    return pl.pallas_call(
        _ssd_kernel,
        grid_spec=grid_spec,
        out_shape=[jax.ShapeDtypeStruct((t, width), BF16),
                   jax.ShapeDtypeStruct((nseq, width // LANE, SSD_N, LANE), F32)],
        compiler_params=_params(("arbitrary", "arbitrary")),
        name="ssd_mixer",
    )(seq_id, start, end, proj, proj, proj, proj, proj, conv_in, state_in, conv_w, conv_b, dtb, alog, d_exp, nw)


def _bdot(a, b, ca, cb):
    return lax.dot_general(a, b, (((ca,), (cb,)), ((0,), (0,))), preferred_element_type=F32)


def _bdot3(a, b):
    ah = a.astype(BF16)
    al = (a - ah.astype(F32)).astype(BF16)
    bh = b.astype(BF16)
    bl = (b - bh.astype(F32)).astype(BF16)
    return _bdot(ah, bh, 2, 1) + (_bdot(ah, bl, 2, 1) + _bdot(al, bh, 2, 1))


def _gdn_kernel(seq_ref, start_ref, end_ref,
                q_ref, k_ref, v_ref, z_ref, sm_ref, cin_ref, sin_ref, cw_ref,
                dtb_ref, alog_ref, nw_ref,
                y_ref, sout_ref,
                ext_scr, s_scr, *, lane_beta, lane_a):
    hb_i = pl.program_id(0)
    c = pl.program_id(1)
    hb = q_ref.shape[0]
    L = q_ref.shape[1]
    is_start = start_ref[c] == 1

    @pl.when(is_start)
    def _():
        s_scr[...] = sin_ref[0]

    u = jnp.concatenate([q_ref[...], k_ref[...], v_ref[...]], axis=0)
    uc = _silu(_causal_conv(u, ext_scr, cin_ref, cw_ref, is_start))
    q = uc[:hb]
    k = uc[hb:2 * hb]
    v = uc[2 * hb:]
    q = q * (lax.rsqrt(jnp.sum(q * q, axis=-1, keepdims=True) + EPS) * (GDN_D ** -0.5))
    k = k * lax.rsqrt(jnp.sum(k * k, axis=-1, keepdims=True) + EPS)

    sm = sm_ref[0]
    beta_all = jax.nn.sigmoid(sm)
    g_all = -jnp.exp(alog_ref[...]) * _softplus(sm + dtb_ref[...])
    ti = lax.broadcasted_iota(I32, (L, L), 0)
    si = lax.broadcasted_iota(I32, (L, L), 1)
    incl = ti >= si
    strict = ti > si
    gam_all = _dot(incl.astype(F32), g_all, precision=HIGHEST)
    ej = lax.broadcasted_iota(I32, (LANE, hb * LANE), 0)
    ec = jnp.right_shift(lax.broadcasted_iota(I32, (LANE, hb * LANE), 1), 7) + hb_i * hb
    betax = _dot(beta_all, (ej == ec + lane_beta).astype(F32), precision=HIGHEST)
    gamx = _dot(gam_all, (ej == ec + lane_a).astype(F32), precision=HIGHEST)
    beta_c = jnp.stack([betax[:, h * LANE:(h + 1) * LANE] for h in range(hb)])
    gam_c = jnp.stack([gamx[:, h * LANE:(h + 1) * LANE] for h in range(hb)])

    t2 = lax.broadcasted_iota(I32, (L, LANE), 0)
    l2 = lax.broadcasted_iota(I32, (L, LANE), 1)
    diag2 = (t2 == l2).astype(F32)
    gam_r = jnp.sum(gam_c * diag2, axis=1, keepdims=True)[:, :, :L]
    gam_t = gam_c[:, :, :L]
    gam_m = jnp.exp(jnp.where(incl, gam_t - gam_r, NEG_BIG))

    kb = k.astype(BF16)
    kk = _bdot(kb, kb, 2, 2)
    a_mat = jnp.where(strict, beta_c[:, :, :L] * kk * gam_m, 0.0)
    n_pow = -a_mat
    x_inv = jnp.where(ti == si, 1.0, 0.0) + n_pow
    span = 2
    while span < L:
        n_pow = _bdot3(n_pow, n_pow)
        x_inv = x_inv + _bdot3(x_inv, n_pow)
        span *= 2

    eg = jnp.exp(gam_c)
    rhs = jnp.concatenate([v * beta_c, k * (beta_c * eg)], axis=-1)
    sol = _bdot3(x_inv, rhs)
    u_ = sol[:, :, :GDN_D]
    w_ = sol[:, :, GDN_D:]
    s_prev = s_scr[...]
    s_b = s_prev.astype(BF16)
    v_new = u_ - _bdot(w_.astype(BF16), s_b, 2, 1)
    vn_b = v_new.astype(BF16)
    qk = _bdot(q.astype(BF16), kb, 2, 2) * gam_m
    o = _bdot((q * eg).astype(BF16), s_b, 2, 1) + _bdot(qk.astype(BF16), vn_b, 2, 1)
    gl = gam_c[:, L - 1:L, :]
    kt = (k * jnp.exp(gl - gam_c)).astype(BF16)
    for h in range(hb):
        upd = lax.dot_general(kt[h], vn_b[h], (((0,), (0,)), ((), ())), preferred_element_type=F32)
        s_scr[h] = s_prev[h] * jnp.exp(gl[h]) + upd

    o = o * lax.rsqrt(jnp.mean(o * o, axis=-1, keepdims=True) + EPS)
    o = (o * nw_ref[...]) * _silu(z_ref[...])
    for h in range(hb):
        y_ref[:, h * LANE:(h + 1) * LANE] = o[h].astype(y_ref.dtype)

    @pl.when(end_ref[c] == 1)
    def _():
        sout_ref[0] = s_scr[...]


def _gdn_mixer(proj, seq_id, start, end, conv_in, state_in, conv_w, dtb, alog, nw,
               t, heads, cb_q, cb_z, cb_sm, lane_beta, lane_a):
    hb = min(GDN_HB, heads)
    assert heads % hb == 0 and cb_q % hb == 0 and cb_z % hb == 0
    nhb = heads // hb
    nc = t // CHUNK
    nseq = state_in.shape[0]
    L = CHUNK

    def im(f):
        return lambda h, c, s, st, en: f(h, c, s)

    grid_spec = pltpu.PrefetchScalarGridSpec(
        num_scalar_prefetch=3,
        grid=(nhb, nc),
        in_specs=[
            pl.BlockSpec((hb, L, LANE), im(lambda h, c, s: (cb_q // hb + h, c, 0))),
            pl.BlockSpec((hb, L, LANE), im(lambda h, c, s: ((cb_q + heads) // hb + h, c, 0))),
            pl.BlockSpec((hb, L, LANE), im(lambda h, c, s: ((cb_q + 2 * heads) // hb + h, c, 0))),
            pl.BlockSpec((hb, L, LANE), im(lambda h, c, s: (cb_z // hb + h, c, 0))),
            pl.BlockSpec((1, L, LANE), im(lambda h, c, s: (cb_sm, c, 0))),
            pl.BlockSpec((1, 1, 3 * hb, CONV_K - 1, LANE), im(lambda h, c, s: (s[c], h, 0, 0, 0))),
            pl.BlockSpec((1, hb, GDN_D, GDN_D), im(lambda h, c, s: (s[c], h, 0, 0))),
            pl.BlockSpec((1, CONV_K, 3 * hb, 1, LANE), im(lambda h, c, s: (h, 0, 0, 0, 0))),
            pl.BlockSpec((1, LANE), im(lambda h, c, s: (0, 0))),
            pl.BlockSpec((1, LANE), im(lambda h, c, s: (0, 0))),
            pl.BlockSpec((1, LANE), im(lambda h, c, s: (0, 0))),
        ],
        out_specs=[
            pl.BlockSpec((L, hb * LANE), im(lambda h, c, s: (c, h))),
            pl.BlockSpec((1, hb, GDN_D, GDN_D), im(lambda h, c, s: (s[c], h, 0, 0))),
        ],
        scratch_shapes=[pltpu.VMEM((3 * hb, L + SUBLANE, LANE), F32), pltpu.VMEM((hb, GDN_D, GDN_D), F32)],
    )
    return pl.pallas_call(
        functools.partial(_gdn_kernel, lane_beta=lane_beta, lane_a=lane_a),
        grid_spec=grid_spec,
        out_shape=[jax.ShapeDtypeStruct((t, heads * GDN_D), BF16),
                   jax.ShapeDtypeStruct((nseq, heads, GDN_D, GDN_D), F32)],
        compiler_params=_params(("arbitrary", "arbitrary")),
        name="gdn_mixer",
    )(seq_id, start, end, proj, proj, proj, proj, proj, conv_in, state_in, conv_w, dtb, alog, nw)


def _outproj_kernel(a1_ref, a2_ref, w1_ref, w2_ref, x_ref, o_ref):
    acc = _dot(a1_ref[...], w1_ref[...]) + _dot(a2_ref[...], w2_ref[...])
    o_ref[...] = x_ref[...] + acc


def _outproj(a1, a2, w1, w2, x, tm, tn):
    t, k1 = a1.shape
    k2 = a2.shape[1]
    d = w1.shape[1]
    return pl.pallas_call(
        _outproj_kernel,
        grid=(t // tm, d // tn),
        in_specs=[pl.BlockSpec((tm, k1), lambda i, j: (i, 0)),
                  pl.BlockSpec((tm, k2), lambda i, j: (i, 0)),
                  pl.BlockSpec((k1, tn), lambda i, j: (0, j)),
                  pl.BlockSpec((k2, tn), lambda i, j: (0, j)),
                  pl.BlockSpec((tm, tn), lambda i, j: (i, j))],
        out_specs=pl.BlockSpec((tm, tn), lambda i, j: (i, j)),
        out_shape=jax.ShapeDtypeStruct((t, d), F32),
        compiler_params=_params(("parallel", "arbitrary")),
        name="outproj",
    )(a1, a2, w1, w2, x)


def _router_kernel(x_ref, nw_ref, wr_ref, br_ref, idx_ref, gate_ref):
    x = x_ref[...]
    h = (x * lax.rsqrt(jnp.mean(x * x, axis=-1, keepdims=True) + EPS)) * nw_ref[...]
    logits = _dot(h, wr_ref[...], precision=HIGHEST) + br_ref[...]
    lane = lax.broadcasted_iota(I32, logits.shape, 1)
    vals = logits
    idx_out = jnp.zeros(logits.shape, I32)
    top = []
    for kk in range(TOP_K):
        m = jnp.max(vals, axis=-1, keepdims=True)
        sel = jnp.min(jnp.where(vals == m, lane, LANE), axis=-1, keepdims=True)
        idx_out = jnp.where(lane == kk, sel, idx_out)
        top.append(m)
        vals = jnp.where(lane == sel, -jnp.inf, vals)
    es = [jnp.exp(m - top[0]) for m in top]
    den = es[0]
    for e in es[1:]:
        den = den + e
    gate_out = jnp.zeros(logits.shape, F32)
    for kk in range(TOP_K):
        gate_out = jnp.where(lane == kk, es[kk] / den, gate_out)
    idx_ref[...] = idx_out
    gate_ref[...] = gate_out


def _router(x1, nw, wr, br, tm):
    t, d = x1.shape
    return pl.pallas_call(
        _router_kernel,
        grid=(t // tm,),
        in_specs=[pl.BlockSpec((tm, d), lambda i: (i, 0)),
                  pl.BlockSpec((1, d), lambda i: (0, 0)),
                  pl.BlockSpec((d, LANE), lambda i: (0, 0)),
                  pl.BlockSpec((1, LANE), lambda i: (0, 0))],
        out_specs=[pl.BlockSpec((tm, LANE), lambda i: (i, 0)), pl.BlockSpec((tm, LANE), lambda i: (i, 0))],
        out_shape=[jax.ShapeDtypeStruct((t, LANE), I32), jax.ShapeDtypeStruct((t, LANE), F32)],
        compiler_params=_params(("parallel",)),
        name="router",
    )(x1, nw, wr, br)


def _row_copy(src_hbm, dst_vmem, sem, src_row, dst_row):
    return pltpu.make_async_copy(src_hbm.at[pl.ds(src_row, 1), :], dst_vmem.at[pl.ds(dst_row, 1), :], sem)


def _dispatch_kernel(tok_ref, nused_ref, x_hbm, nw_ref, o_ref, buf, sem):
    b = pl.program_id(0)
    bm = o_ref.shape[0]
    used = b < nused_ref[0]

    @pl.when(used)
    def _():
        def issue(r, carry):
            _row_copy(x_hbm, buf, sem, tok_ref[b * bm + r], r).start()
            return carry
        lax.fori_loop(0, bm, issue, 0)

        def drain(r, carry):
            _row_copy(x_hbm, buf, sem, 0, r).wait()
            return carry
        lax.fori_loop(0, bm, drain, 0)
        x = buf[...]
        h = (x * lax.rsqrt(jnp.mean(x * x, axis=-1, keepdims=True) + EPS)) * nw_ref[...]
        o_ref[...] = h.astype(o_ref.dtype)

    @pl.when(jnp.logical_not(used))
    def _():
        o_ref[...] = jnp.zeros(o_ref.shape, o_ref.dtype)


def _dispatch(slot_tok, n_used, x1, nw, nblk):
    t, d = x1.shape
    grid_spec = pltpu.PrefetchScalarGridSpec(
        num_scalar_prefetch=2,
        grid=(nblk,),
        in_specs=[pl.BlockSpec(memory_space=pl.ANY),
                  pl.BlockSpec((1, d), lambda b, tok, nu: (0, 0))],
        out_specs=pl.BlockSpec((MOE_BM, d), lambda b, tok, nu: (b, 0)),
        scratch_shapes=[pltpu.VMEM((MOE_BM, d), F32), pltpu.SemaphoreType.DMA(())],
    )
    return pl.pallas_call(
        _dispatch_kernel,
        grid_spec=grid_spec,
        out_shape=jax.ShapeDtypeStruct((nblk * MOE_BM, d), BF16),
        compiler_params=_params(("arbitrary",)),
        name="moe_dispatch",
    )(slot_tok, n_used, x1, nw)


def _expert_changed(blk_ref, b):
    prev = blk_ref[jnp.maximum(b - 1, 0)]
    return jnp.logical_or(b == 0, blk_ref[b] != prev)


def _gateup_kernel(blk_ref, nused_ref, x_ref, wg_ref, wu_ref, bg_ref, bu_ref, o_ref, wg_scr, wu_scr):
    b = pl.program_id(1)
    used = b < nused_ref[0]

    @pl.when(jnp.logical_and(used, _expert_changed(blk_ref, b)))
    def _():
        wg_scr[...] = wg_ref[0].astype(BF16)
        wu_scr[...] = wu_ref[0].astype(BF16)

    @pl.when(used)
    def _():
        x = x_ref[...]
        gate = jnp.minimum(_dot(x, wg_scr[...]) + bg_ref[0], SWIGLU_LIMIT)
        up = jnp.clip(_dot(x, wu_scr[...]) + bu_ref[0], -SWIGLU_LIMIT, SWIGLU_LIMIT)
        act = gate * jax.nn.sigmoid(SWIGLU_ALPHA * gate) * (up + 1.0)
        o_ref[...] = act.astype(o_ref.dtype)

    @pl.when(jnp.logical_not(used))
    def _():
        o_ref[...] = jnp.zeros(o_ref.shape, o_ref.dtype)


def _gateup(blk_e, n_used, xs, wg, wu, bg, bu, tn):
    p, d = xs.shape
    dff = wg.shape[2]
    nblk = p // MOE_BM
    grid_spec = pltpu.PrefetchScalarGridSpec(
        num_scalar_prefetch=2,
        grid=(dff // tn, nblk),
        in_specs=[pl.BlockSpec((MOE_BM, d), lambda n, b, be, nu: (b, 0)),
                  pl.BlockSpec((1, d, tn), lambda n, b, be, nu: (be[b], 0, n)),
                  pl.BlockSpec((1, d, tn), lambda n, b, be, nu: (be[b], 0, n)),
                  pl.BlockSpec((1, 1, tn), lambda n, b, be, nu: (be[b], 0, n)),
                  pl.BlockSpec((1, 1, tn), lambda n, b, be, nu: (be[b], 0, n))],
        out_specs=pl.BlockSpec((MOE_BM, tn), lambda n, b, be, nu: (b, n)),
        scratch_shapes=[pltpu.VMEM((d, tn), BF16), pltpu.VMEM((d, tn), BF16)],
    )
    return pl.pallas_call(
        _gateup_kernel,
        grid_spec=grid_spec,
        out_shape=jax.ShapeDtypeStruct((p, dff), BF16),
        compiler_params=_params(("arbitrary", "arbitrary")),
        name="moe_gateup",
    )(blk_e, n_used, xs, wg, wu, bg, bu)


def _down_kernel(blk_ref, nused_ref, a_ref, wd_ref, bd_ref, o_ref, wd_scr):
    b = pl.program_id(1)
    used = b < nused_ref[0]

    @pl.when(jnp.logical_and(used, _expert_changed(blk_ref, b)))
    def _():
        wd_scr[...] = wd_ref[0].astype(BF16)

    @pl.when(used)
    def _():
        o_ref[...] = _dot(a_ref[...], wd_scr[...]) + bd_ref[0]

    @pl.when(jnp.logical_not(used))
    def _():
        o_ref[...] = jnp.zeros(o_ref.shape, o_ref.dtype)


def _down(blk_e, n_used, act, wd, bd, tn):
    p, dff = act.shape
    d = wd.shape[2]
    nblk = p // MOE_BM
    grid_spec = pltpu.PrefetchScalarGridSpec(
        num_scalar_prefetch=2,
        grid=(d // tn, nblk),
        in_specs=[pl.BlockSpec((MOE_BM, dff), lambda n, b, be, nu: (b, 0)),
                  pl.BlockSpec((1, dff, tn), lambda n, b, be, nu: (be[b], 0, n)),
                  pl.BlockSpec((1, 1, tn), lambda n, b, be, nu: (be[b], 0, n))],
        out_specs=pl.BlockSpec((MOE_BM, tn), lambda n, b, be, nu: (b, n)),
        scratch_shapes=[pltpu.VMEM((dff, tn), BF16)],
    )
    return pl.pallas_call(
        _down_kernel,
        grid_spec=grid_spec,
        out_shape=jax.ShapeDtypeStruct((p, d), F32),
        compiler_params=_params(("arbitrary", "arbitrary")),
        name="moe_down",
    )(blk_e, n_used, act, wd, bd)


def _combine_kernel(pos_ref, ys_hbm, x_ref, g_ref, nw_ref, o_ref, buf, sem):
    i = pl.program_id(0)
    tt = o_ref.shape[0]
    n = TOP_K * tt

    def issue(r, carry):
        _row_copy(ys_hbm, buf, sem, pos_ref[i * n + r], r).start()
        return carry
    lax.fori_loop(0, n, issue, 0)

    def drain(r, carry):
        _row_copy(ys_hbm, buf, sem, 0, r).wait()
        return carry
    lax.fori_loop(0, n, drain, 0)

    g = g_ref[...]
    acc = x_ref[...]
    for kk in range(TOP_K):
        acc = acc + buf[kk * tt:(kk + 1) * tt, :] * g[:, kk:kk + 1]
    y = acc * lax.rsqrt(jnp.mean(acc * acc, axis=-1, keepdims=True) + EPS)
    o_ref[...] = y * nw_ref[...]


def _combine(pos_tiles, ys, x1, gates, nw, tt):
    t, d = x1.shape
    grid_spec = pltpu.PrefetchScalarGridSpec(
        num_scalar_prefetch=1,
        grid=(t // tt,),
        in_specs=[pl.BlockSpec(memory_space=pl.ANY),
                  pl.BlockSpec((tt, d), lambda i, pos: (i, 0)),
                  pl.BlockSpec((tt, LANE), lambda i, pos: (i, 0)),
                  pl.BlockSpec((1, d), lambda i, pos: (0, 0))],
        out_specs=pl.BlockSpec((tt, d), lambda i, pos: (i, 0)),
        scratch_shapes=[pltpu.VMEM((TOP_K * tt, d), F32), pltpu.SemaphoreType.DMA(())],
    )
    return pl.pallas_call(
        _combine_kernel,
        grid_spec=grid_spec,
        out_shape=jax.ShapeDtypeStruct((t, d), F32),
        compiler_params=_params(("arbitrary",)),
        name="moe_combine",
    )(pos_tiles, ys, x1, gates, nw)


def _pad_lanes(v, offset):
    out = jnp.zeros((LANE,), F32)
    return lax.dynamic_update_slice(out, v.astype(F32), (offset,)).reshape(1, LANE)


def _ssd_group_layout(a, width):
    lead = a.shape[:-2]
    rows = a.shape[-2]
    rb = width // LANE // SSD_G
    xs = a[..., :width].reshape(*lead, rows, SSD_G, rb, LANE)
    bs = a[..., width:width + SSD_G * SSD_N].reshape(*lead, rows, SSD_G, 1, LANE)
    cs = a[..., width + SSD_G * SSD_N:].reshape(*lead, rows, SSD_G, 1, LANE)
    cat = jnp.concatenate([xs, bs, cs], axis=-2)
    n = cat.ndim
    return jnp.moveaxis(cat, n - 4, n - 2)


def _gdn_block_layout(a, heads, hb):
    lead = a.shape[:-2]
    rows = a.shape[-2]
    r = a.reshape(*lead, rows, 3, heads // hb, hb, LANE)
    n = r.ndim
    r = jnp.moveaxis(r, n - 5, n - 2)
    r = jnp.moveaxis(r, n - 5, n - 4)
    return r.reshape(*lead, heads // hb, 3 * hb, rows, LANE)


def _pick_tn_cb(ncb):
    best = None
    for cand in (10, 8, 6, 4, 2):
        pad = (-ncb) % cand
        if best is None or pad < best[1]:
            best = (cand, pad)
    return best


def kernel(x_prompt, x_sample, state_ssd_conv, state_ssd, state_gdn_conv, state_gdn, norm_mix, w_in, ssd_conv_w,
           ssd_conv_b, ssd_dt_bias, ssd_A_log, ssd_D, ssd_norm, gdn_conv_w, gdn_dt_bias, gdn_A_log, gdn_norm, w_out,
           norm_ffn, w_router, b_router, w_gate, b_gate, w_up, b_up, w_down, b_down, norm_final):
    assert w_in.shape[0] == 1, "single layer"
    nb_p, seq_p, d = x_prompt.shape
    nb_s, seq_s, _ = x_sample.shape
    assert seq_p % CHUNK == 0 and seq_s % CHUNK == 0
    ssd_heads = d // SSD_P
    ssd_w = ssd_heads * SSD_P
    ssd_cs = ssd_w + 2 * SSD_G * SSD_N
    gdn_heads = d // GDN_D
    gdn_w = gdn_heads * GDN_D
    t_p = nb_p * seq_p
    t = t_p + nb_s * seq_s
    nseq = nb_p + nb_s

    x_all = jnp.concatenate([x_prompt.reshape(t_p, d), x_sample.reshape(nb_s * seq_s, d)], axis=0)
    seq_len = [seq_p] * nb_p + [seq_s] * nb_s
    seq_id, start, end = [], [], []
    for s, n in enumerate(seq_len):
        for cidx in range(n // CHUNK):
            seq_id.append(s)
            start.append(int(cidx == 0))
            end.append(int(cidx == n // CHUNK - 1))
    seq_id = jnp.asarray(np.array(seq_id, np.int32))
    start = jnp.asarray(np.array(start, np.int32))
    end = jnp.asarray(np.array(end, np.int32))

    o_z, o_xbc, o_dt = 0, ssd_w, ssd_w + ssd_cs
    o_qkv = o_dt + ssd_heads
    o_zg = o_qkv + 3 * gdn_w
    o_b = o_zg + gdn_w
    o_a = o_b + gdn_heads
    w0 = w_in[0]
    n_small = ssd_heads + 2 * gdn_heads
    assert n_small <= LANE
    cb_z = 0
    cb_x = ssd_w // LANE
    cb_q = cb_x + ssd_cs // LANE
    cb_zg = cb_q + 3 * gdn_w // LANE
    cb_sm = cb_zg + gdn_w // LANE
    ncb = cb_sm + 1
    tn_cb, pad_cb = _pick_tn_cb(ncb)
    w_perm = jnp.concatenate([
        w0[:, o_z:o_z + ssd_w], w0[:, o_xbc:o_xbc + ssd_cs], w0[:, o_qkv:o_qkv + 3 * gdn_w],
        w0[:, o_zg:o_zg + gdn_w], w0[:, o_dt:o_dt + ssd_heads], w0[:, o_b:o_b + 2 * gdn_heads],
        jnp.zeros((d, LANE - n_small + pad_cb * LANE), F32)], axis=1).astype(BF16)
    lane_beta = ssd_heads
    lane_a = ssd_heads + gdn_heads

    tm_big = _row_tile(t, 768)
    tm_mid = _row_tile(t, 528)
    h = _rmsnorm_cast(x_all, norm_mix[0], tm_mid)
    proj = _inproj(h, w_perm, tm_big, tn_cb)

    def with_zero_prompt(a):
        return jnp.concatenate([jnp.zeros((nb_p,) + a.shape[1:], a.dtype), a], axis=0)

    ssd_conv0 = _ssd_group_layout(with_zero_prompt(state_ssd_conv[0]), ssd_w)
    gdn_hb = min(GDN_HB, gdn_heads)
    gdn_conv0 = _gdn_block_layout(with_zero_prompt(state_gdn_conv[0]), gdn_heads, gdn_hb)
    s0 = with_zero_prompt(state_ssd[0])
    ssd_s0 = s0.reshape(nseq, ssd_heads // 2, 2, SSD_P, SSD_N).transpose(0, 1, 4, 2, 3).reshape(
        nseq, ssd_heads // 2, SSD_N, LANE)
    gdn_s0 = with_zero_prompt(state_gdn[0])

    ssd_cw = _ssd_group_layout(ssd_conv_w[0][None], ssd_w)[0]
    ssd_cw = jnp.swapaxes(ssd_cw, 1, 2)[:, :, :, None, :]
    ssd_cb = _ssd_group_layout(ssd_conv_b[0][None, None], ssd_w)[0]
    d_exp = jnp.repeat(ssd_D[0], SSD_P).reshape(ssd_w // LANE, 1, LANE)
    ssd_nw = ssd_norm[0].reshape(ssd_w // LANE, 1, LANE)
    y_ssd, ssd_s = _ssd_mixer(
        proj, seq_id, start, end, ssd_conv0, ssd_s0, ssd_cw, ssd_cb,
        _pad_lanes(ssd_dt_bias[0], 0), _pad_lanes(ssd_A_log[0], 0), d_exp, ssd_nw,
        t, ssd_w, cb_z, cb_x, cb_x + ssd_w // LANE, cb_x + ssd_w // LANE + SSD_G, cb_sm)

    gdn_cw = _gdn_block_layout(gdn_conv_w[0][None], gdn_heads, gdn_hb)[0]
    gdn_cw = jnp.swapaxes(gdn_cw, 1, 2)[:, :, :, None, :]
    y_gdn, gdn_s = _gdn_mixer(
        proj, seq_id, start, end, gdn_conv0, gdn_s0, gdn_cw,
        _pad_lanes(gdn_dt_bias[0], lane_a), _pad_lanes(gdn_A_log[0], lane_a), gdn_norm[0].reshape(1, LANE),
        t, gdn_heads, cb_q, cb_zg, cb_sm, lane_beta, lane_a)

    w_o = w_out[0].astype(BF16)
    x1 = _outproj(y_ssd, y_gdn, w_o[:ssd_w], w_o[ssd_w:], x_all, tm_mid, min(512, d))

    wr = jnp.concatenate([w_router[0], jnp.zeros((d, LANE - N_EXPERTS), F32)], axis=1)
    br = jnp.concatenate([b_router[0], jnp.full((LANE - N_EXPERTS,), NEG_BIG, F32)]).reshape(1, LANE)
    nffn = norm_ffn[0].reshape(1, d)
    idx_pad, gate_pad = _router(x1, nffn, wr, br, tm_mid)
    top_idx = idx_pad[:, :TOP_K]
    tk = t * TOP_K
    flat_e = top_idx.reshape(tk)
    order = jnp.argsort(flat_e).astype(I32)
    sorted_e = flat_e[order]
    counts = jnp.sum((flat_e[:, None] == jnp.arange(N_EXPERTS, dtype=I32)[None, :]).astype(I32), axis=0)
    starts = jnp.cumsum(counts) - counts
    pcounts = (counts + MOE_BM - 1) // MOE_BM * MOE_BM
    pends = jnp.cumsum(pcounts)
    pstarts = pends - pcounts
    dest = (pstarts[sorted_e] + (jnp.arange(tk, dtype=I32) - starts[sorted_e])).astype(I32)
    nblk = -(-tk // MOE_BM) + N_EXPERTS
    slot_tok = jnp.zeros((nblk * MOE_BM,), I32).at[dest].set(order // TOP_K)
    pos = jnp.zeros((tk,), I32).at[order].set(dest)
    blk_e = jnp.minimum(jnp.searchsorted(pends, jnp.arange(nblk, dtype=I32) * MOE_BM, side='right'),
                        N_EXPERTS - 1).astype(I32)
    n_used = (pends[-1] // MOE_BM).astype(I32).reshape(1)

    xs = _dispatch(slot_tok, n_used, x1, nffn, nblk)
    tn_ff = min(512, d)
    act = _gateup(blk_e, n_used, xs, w_gate[0], w_up[0], b_gate[0][:, None, :], b_up[0][:, None, :], tn_ff)
    ys = _down(blk_e, n_used, act, w_down[0], b_down[0][:, None, :], tn_ff)

    tt = CHUNK
    pos_tiles = pos.reshape(t // tt, tt, TOP_K).transpose(0, 2, 1).reshape(tk)
    y = _combine(pos_tiles, ys, x1, gate_pad, norm_final.reshape(1, d), tt)

    y_prompt = y[:t_p].reshape(nb_p, seq_p, d)
    y_sample = y[t_p:].reshape(nb_s, seq_s, d)

    def last_rows(cb0, ncols):
        nblk_c = ncols // LANE
        pr = proj[cb0:cb0 + nblk_c, :t_p].reshape(nblk_c, nb_p, seq_p, LANE)[:, :, seq_p - (CONV_K - 1):]
        sa = proj[cb0:cb0 + nblk_c, t_p:t].reshape(nblk_c, nb_s, seq_s, LANE)[:, :, seq_s - (CONV_K - 1):]
        fix = lambda a: a.transpose(1, 2, 0, 3).reshape(a.shape[1], CONV_K - 1, ncols)[None]
        return fix(pr), fix(sa)

    ssd_conv_p, ssd_conv_s = last_rows(cb_x, ssd_cs)
    gdn_conv_p, gdn_conv_s = last_rows(cb_q, 3 * gdn_w)
    ssd_state = ssd_s.reshape(nseq, ssd_heads // 2, SSD_N, 2, SSD_P).transpose(0, 1, 3, 4, 2).reshape(
        nseq, ssd_heads, SSD_P, SSD_N)
    return (y_prompt, y_sample,
            ssd_conv_p, ssd_state[:nb_p][None], gdn_conv_p, gdn_s[:nb_p][None],
            ssd_conv_s, ssd_state[nb_p:][None], gdn_conv_s, gdn_s[nb_p:][None])
```

```python
import functools

import numpy as np
import jax
import jax.numpy as jnp
from jax import lax
from jax.experimental import pallas as pl
from jax.experimental.pallas import tpu as pltpu

F32 = jnp.float32
BF16 = jnp.bfloat16
I32 = jnp.int32

LANE = 128
SUBLANE = 8
VMEM_LIMIT = 56 * 1024 * 1024

CHUNK = 64
CONV_K = 4
SSD_P = 64
SSD_N = 128
SSD_G = 8
GDN_D = 128
N_EXPERTS = 32
TOP_K = 4
SWIGLU_LIMIT = 7.0
SWIGLU_ALPHA = 1.702
EPS = 1e-6
NEG_BIG = -1e30

MOE_BM = 256
GDN_HB = 32
SSD_GS = 8
HIGHEST = lax.Precision.HIGHEST


def _row_tile(n, target, mult=16):
    best = None
    for t in range(mult, min(n, target) + 1, mult):
        if n % t == 0:
            best = t
    assert best is not None, (n, target)
    return best


def _params(sem):
    return pltpu.CompilerParams(dimension_semantics=sem, vmem_limit_bytes=VMEM_LIMIT)


def _silu(x):
    return x * jax.nn.sigmoid(x)


def _softplus(x):
    return jnp.maximum(x, 0.0) + jnp.log1p(jnp.exp(-jnp.abs(x)))


def _dot(a, b, **kw):
    return jnp.dot(a, b, preferred_element_type=F32, **kw)


def _split3(a):
    hi = a.astype(BF16)
    r = a - hi.astype(F32)
    mid = r.astype(BF16)
    lo = (r - mid.astype(F32)).astype(BF16)
    return hi, mid, lo


def _two_source_specs(block, na, col=None):
    if col is None:
        ia = lambda i, *_: (jnp.minimum(i, na - 1), 0)
        ib = lambda i, *_: (jnp.maximum(i - na, 0), 0)
    else:
        ia = lambda i, j, *_: (jnp.minimum(i, na - 1), jnp.where(i < na, j, col))
        ib = lambda i, j, *_: (jnp.maximum(i - na, 0), jnp.where(i >= na, j, 0))
    return pl.BlockSpec(block, ia), pl.BlockSpec(block, ib)


def _rmsnorm_cast_kernel(xa_ref, xb_ref, w_ref, o_ref, *, na):
    def body(x_ref):
        x = x_ref[...]
        xn = x * lax.rsqrt(jnp.mean(x * x, axis=-1, keepdims=True) + EPS)
        o_ref[...] = (xn * w_ref[...]).astype(o_ref.dtype)

    pl.when(pl.program_id(0) < na)(lambda: body(xa_ref))
    pl.when(pl.program_id(0) >= na)(lambda: body(xb_ref))


def _rmsnorm_cast(xa, xb, w, tm):
    ta, d = xa.shape
    tb = xb.shape[0]
    na = ta // tm
    spec_a, spec_b = _two_source_specs((tm, d), na)
    return pl.pallas_call(
        functools.partial(_rmsnorm_cast_kernel, na=na),
        grid=(na + tb // tm,),
        in_specs=[spec_a, spec_b, pl.BlockSpec((1, d), lambda i: (0, 0))],
        out_specs=pl.BlockSpec((tm, d), lambda i: (i, 0)),
        out_shape=jax.ShapeDtypeStruct((ta + tb, d), BF16),
        compiler_params=_params(("arbitrary",)),
        name="rmsnorm_cast",
    )(xa, xb, w.reshape(1, d))


def _inproj_kernel(a_ref, w_ref, o_ref):
    ncb = o_ref.shape[0]
    a = a_ref[...]
    step = 2 if ncb % 2 == 0 else 1
    for j in range(0, ncb, step):
        acc = _dot(a, w_ref[:, j * LANE:(j + step) * LANE])
        for s in range(step):
            o_ref[j + s] = acc[:, s * LANE:(s + 1) * LANE]


def _inproj(a, w, tm, tn_cb):
    t, d = a.shape
    n = w.shape[1]
    ncb = n // LANE
    return pl.pallas_call(
        _inproj_kernel,
        grid=(t // tm, ncb // tn_cb),
        in_specs=[pl.BlockSpec((tm, d), lambda i, j: (i, 0)),
                  pl.BlockSpec((d, tn_cb * LANE), lambda i, j: (0, j))],
        out_specs=pl.BlockSpec((tn_cb, tm, LANE), lambda i, j: (j, i, 0)),
        out_shape=jax.ShapeDtypeStruct((ncb, t, LANE), F32),
        compiler_params=_params(("parallel", "arbitrary")),
        name="inproj",
    )(a, w)


def _causal_conv(u, ext_scr, conv_in, taps, is_start):
    L = u.shape[1]
    base = SUBLANE - (CONV_K - 1)

    @pl.when(is_start)
    def _():
        ext_scr[:, base:SUBLANE, :] = conv_in()

    ext_scr[:, SUBLANE:SUBLANE + L, :] = u
    acc = ext_scr[:, base:base + L, :] * taps(0)
    for j in range(1, CONV_K):
        acc = acc + ext_scr[:, base + j:base + j + L, :] * taps(j)
    ext_scr[:, base:SUBLANE, :] = ext_scr[:, base + L:SUBLANE + L, :]
    return acc


def _cumsum_rows(a, incl):
    m = incl.astype(BF16)
    hi, mid, lo = _split3(a)
    return _dot(m, hi) + (_dot(m, mid) + _dot(m, lo))


def _select_dot(a, onehot):
    oh = onehot.astype(BF16)
    hi, mid, lo = _split3(a)
    return _dot(hi, oh) + (_dot(mid, oh) + _dot(lo, oh))


def _ssd_kernel(seq_ref, start_ref, end_ref,
                z_ref, x_ref, b_ref, c_ref, sm_ref, cin_ref, sin_ref, cw_ref, cb_ref,
                dtb_ref, alog_ref, d_ref, nw_ref,
                y_ref, sout_ref,
                ext_scr, s_scr):
    gi0 = pl.program_id(0)
    c = pl.program_id(1)
    gs = b_ref.shape[0]
    rb = x_ref.shape[0] // gs
    nb = rb + 2
    L = x_ref.shape[1]
    is_start = start_ref[c] == 1

    @pl.when(is_start)
    def _():
        s_scr[...] = sin_ref[0]

    u = jnp.concatenate(
        [p for gi in range(gs) for p in (x_ref[gi * rb:(gi + 1) * rb], b_ref[gi:gi + 1], c_ref[gi:gi + 1])], axis=0)
    conv = _causal_conv(u, ext_scr, lambda: cin_ref[0].reshape(gs * nb, CONV_K - 1, LANE),
                        lambda j: cw_ref[:, j].reshape(gs * nb, 1, LANE), is_start)
    uc = _silu(conv + cb_ref[...].reshape(gs * nb, 1, LANE))

    sm = sm_ref[0]
    dt_all = _softplus(sm + dtb_ref[...])
    a_all = dt_all * (-jnp.exp(alog_ref[...]))
    ti = lax.broadcasted_iota(I32, (L, L), 0)
    si = lax.broadcasted_iota(I32, (L, L), 1)
    cum_all = _cumsum_rows(a_all, ti >= si)
    ej = lax.broadcasted_iota(I32, (LANE, gs * rb * LANE), 0)
    ec = lax.broadcasted_iota(I32, (LANE, gs * rb * LANE), 1)
    expand = ej == gi0 * (gs * 2 * rb) + jnp.right_shift(ec, 6)
    dtx = _select_dot(dt_all, expand)
    cumx = _select_dot(cum_all, expand)

    t2 = lax.broadcasted_iota(I32, (L, LANE), 0)
    l2 = lax.broadcasted_iota(I32, (L, LANE), 1)
    s2 = jnp.bitwise_and(l2, SSD_P - 1)
    diag2 = (t2 == s2).astype(F32)
    causal2 = t2 >= s2
    left = l2 < SSD_P

    for gi in range(gs):
        bm = uc[gi * nb + rb]
        cm = uc[gi * nb + rb + 1]
        b2 = jnp.concatenate([bm, bm], axis=0).astype(BF16)
        cm_b = cm.astype(BF16)
        bm_b = bm.astype(BF16)
        cb2 = lax.dot_general(cm_b, b2, (((1,), (1,)), ((), ())), preferred_element_type=F32)
        ygs = []
        ms = jnp.zeros((L, 1), F32)
        for j in range(rb):
            jj = gi * rb + j
            ccol = cumx[:, jj * LANE:(jj + 1) * LANE]
            dtc = dtx[:, jj * LANE:(jj + 1) * LANE]
            crow = jnp.sum(ccol * diag2, axis=0, keepdims=True)
            dec = jnp.exp(jnp.where(causal2, ccol - crow, NEG_BIG))
            ww = (cb2 * dec).astype(BF16)
            xb = uc[gi * nb + j]
            xdt = xb * dtc
            xbd = jnp.concatenate([jnp.where(left, xdt, 0.0), jnp.where(left, 0.0, xdt)], axis=0).astype(BF16)
            y = _dot(ww, xbd)
            sj = s_scr[jj]
            y = y + _dot(cm_b, sj.astype(BF16)) * jnp.exp(ccol)
            y = y + d_ref[jj] * xb
            cl = ccol[L - 1:L, :]
            xw = (xdt * jnp.exp(cl - ccol)).astype(BF16)
            s_scr[jj] = sj * jnp.exp(cl) + lax.dot_general(
                bm_b, xw, (((0,), (0,)), ((), ())), preferred_element_type=F32)
            yg = y * _silu(z_ref[jj])
            ms = ms + jnp.sum(yg * yg, axis=-1, keepdims=True)
            ygs.append(yg)
        inv = lax.rsqrt(ms / (rb * LANE) + EPS)
        for j in range(rb):
            jj = gi * rb + j
            y_ref[:, jj * LANE:(jj + 1) * LANE] = ((ygs[j] * inv) * nw_ref[jj]).astype(y_ref.dtype)

    @pl.when(end_ref[c] == 1)
    def _():
        sout_ref[0] = s_scr[...]


def _ssd_mixer(proj, seq_id, start, end, conv_in, state_in, conv_w, conv_b, dtb, alog, d_exp, nw,
               t, width, cb_z, cb_x, cb_b, cb_c, cb_sm):
    rb = width // LANE // SSD_G
    nb = rb + 2
    nc = t // CHUNK
    nseq = state_in.shape[0]
    L = CHUNK

    def im(f):
        return lambda g, c, s, st, en: f(g, c, s)

    gs = SSD_GS
    grb = gs * rb
    grid_spec = pltpu.PrefetchScalarGridSpec(
        num_scalar_prefetch=3,
        grid=(SSD_G // gs, nc),
        in_specs=[
            pl.BlockSpec((grb, L, LANE), im(lambda g, c, s: (cb_z // grb + g, c, 0))),
            pl.BlockSpec((grb, L, LANE), im(lambda g, c, s: (cb_x // grb + g, c, 0))),
            pl.BlockSpec((gs, L, LANE), im(lambda g, c, s: (cb_b // gs + g, c, 0))),
            pl.BlockSpec((gs, L, LANE), im(lambda g, c, s: (cb_c // gs + g, c, 0))),
            pl.BlockSpec((1, L, LANE), im(lambda g, c, s: (cb_sm, c, 0))),
            pl.BlockSpec((1, gs, nb, CONV_K - 1, LANE), im(lambda g, c, s: (s[c], g, 0, 0, 0))),
            pl.BlockSpec((1, grb, SSD_N, LANE), im(lambda g, c, s: (s[c], g, 0, 0))),
            pl.BlockSpec((gs, CONV_K, nb, 1, LANE), im(lambda g, c, s: (g, 0, 0, 0, 0))),
            pl.BlockSpec((gs, nb, 1, LANE), im(lambda g, c, s: (g, 0, 0, 0))),
            pl.BlockSpec((1, LANE), im(lambda g, c, s: (0, 0))),
            pl.BlockSpec((1, LANE), im(lambda g, c, s: (0, 0))),
            pl.BlockSpec((grb, 1, LANE), im(lambda g, c, s: (g, 0, 0))),
            pl.BlockSpec((grb, 1, LANE), im(lambda g, c, s: (g, 0, 0))),
        ],
        out_specs=[
            pl.BlockSpec((L, grb * LANE), im(lambda g, c, s: (c, g))),
            pl.BlockSpec((1, grb, SSD_N, LANE), im(lambda g, c, s: (s[c], g, 0, 0))),
        ],
        scratch_shapes=[pltpu.VMEM((gs * nb, L + SUBLANE, LANE), F32), pltpu.VMEM((grb, SSD_N, LANE), F32)],
    )
    assert cb_z % grb == 0 and cb_x % grb == 0 and cb_b % gs == 0 and cb_c % gs == 0 and SSD_G % gs == 0
    return pl.pallas_call(
        _ssd_kernel,
        grid_spec=grid_spec,
        out_shape=[jax.ShapeDtypeStruct((t, width), BF16),
                   jax.ShapeDtypeStruct((nseq, width // LANE, SSD_N, LANE), F32)],
        compiler_params=_params(("arbitrary", "arbitrary")),
        name="ssd_mixer",
    )(seq_id, start, end, proj, proj, proj, proj, proj, conv_in, state_in, conv_w, conv_b, dtb, alog, d_exp, nw)


def _bdot(a, b, ca, cb):
    return lax.dot_general(a, b, (((ca,), (cb,)), ((0,), (0,))), preferred_element_type=F32)


def _bdot3(a, b):
    ah = a.astype(BF16)
    al = (a - ah.astype(F32)).astype(BF16)
    bh = b.astype(BF16)
    bl = (b - bh.astype(F32)).astype(BF16)
    return _bdot(ah, bh, 2, 1) + (_bdot(ah, bl, 2, 1) + _bdot(al, bh, 2, 1))


def _gdn_kernel(seq_ref, start_ref, end_ref,
                q_ref, k_ref, v_ref, z_ref, sm_ref, cin_ref, sin_ref, cw_ref,
                dtb_ref, alog_ref, nw_ref,
                y_ref, sout_ref,
                ext_scr, s_scr, *, lane_beta, lane_a):
    hb_i = pl.program_id(0)
    c = pl.program_id(1)
    hb = q_ref.shape[0]
    L = q_ref.shape[1]
    is_start = start_ref[c] == 1

    @pl.when(is_start)
    def _():
        s_scr[...] = sin_ref[0]

    u = jnp.concatenate([q_ref[...], k_ref[...], v_ref[...]], axis=0)
    uc = _silu(_causal_conv(u, ext_scr, lambda: cin_ref[0, 0], lambda j: cw_ref[0, j], is_start))
    q = uc[:hb]
    k = uc[hb:2 * hb]
    v = uc[2 * hb:]
    q = q * (lax.rsqrt(jnp.sum(q * q, axis=-1, keepdims=True) + EPS) * (GDN_D ** -0.5))
    k = k * lax.rsqrt(jnp.sum(k * k, axis=-1, keepdims=True) + EPS)

    sm = sm_ref[0]
    beta_all = jax.nn.sigmoid(sm)
    g_all = -jnp.exp(alog_ref[...]) * _softplus(sm + dtb_ref[...])
    ti = lax.broadcasted_iota(I32, (L, L), 0)
    si = lax.broadcasted_iota(I32, (L, L), 1)
    incl = ti >= si
    strict = ti > si
    gam_all = _cumsum_rows(g_all, incl)
    ej = lax.broadcasted_iota(I32, (LANE, hb * LANE), 0)
    ec = jnp.right_shift(lax.broadcasted_iota(I32, (LANE, hb * LANE), 1), 7) + hb_i * hb
    betax = _select_dot(beta_all, ej == ec + lane_beta)
    gamx = _select_dot(gam_all, ej == ec + lane_a)
    beta_c = jnp.stack([betax[:, h * LANE:(h + 1) * LANE] for h in range(hb)])
    gam_c = jnp.stack([gamx[:, h * LANE:(h + 1) * LANE] for h in range(hb)])

    t2 = lax.broadcasted_iota(I32, (L, LANE), 0)
    l2 = lax.broadcasted_iota(I32, (L, LANE), 1)
    diag2 = (t2 == l2).astype(F32)
    gam_r = jnp.sum(gam_c * diag2, axis=1, keepdims=True)[:, :, :L]
    gam_t = gam_c[:, :, :L]
    gam_m = jnp.exp(jnp.where(incl, gam_t - gam_r, NEG_BIG))

    kb = k.astype(BF16)
    kk = _bdot(kb, kb, 2, 2)
    a_mat = jnp.where(strict, beta_c[:, :, :L] * kk * gam_m, 0.0)
    n_pow = -a_mat
    x_inv = jnp.where(ti == si, 1.0, 0.0) + n_pow
    span = 2
    while span < L:
        n_pow = _bdot3(n_pow, n_pow)
        x_inv = x_inv + _bdot3(x_inv, n_pow)
        span *= 2

    eg = jnp.exp(gam_c)
    rhs = jnp.concatenate([v * beta_c, k * (beta_c * eg)], axis=-1)
    sol = _bdot3(x_inv, rhs)
    u_ = sol[:, :, :GDN_D]
    w_ = sol[:, :, GDN_D:]
    s_prev = s_scr[...]
    s_b = s_prev.astype(BF16)
    v_new = u_ - _bdot(w_.astype(BF16), s_b, 2, 1)
    vn_b = v_new.astype(BF16)
    qk = _bdot(q.astype(BF16), kb, 2, 2) * gam_m
    o = _bdot((q * eg).astype(BF16), s_b, 2, 1) + _bdot(qk.astype(BF16), vn_b, 2, 1)
    gl = gam_c[:, L - 1:L, :]
    kt = (k * jnp.exp(gl - gam_c)).astype(BF16)
    for h in range(hb):
        upd = lax.dot_general(kt[h], vn_b[h], (((0,), (0,)), ((), ())), preferred_element_type=F32)
        s_scr[h] = s_prev[h] * jnp.exp(gl[h]) + upd

    o = o * lax.rsqrt(jnp.mean(o * o, axis=-1, keepdims=True) + EPS)
    o = (o * nw_ref[...]) * _silu(z_ref[...])
    for h in range(hb):
        y_ref[:, h * LANE:(h + 1) * LANE] = o[h].astype(y_ref.dtype)

    @pl.when(end_ref[c] == 1)
    def _():
        sout_ref[0] = s_scr[...]


def _gdn_mixer(proj, seq_id, start, end, conv_in, state_in, conv_w, dtb, alog, nw,
               t, heads, cb_q, cb_z, cb_sm, lane_beta, lane_a):
    hb = min(GDN_HB, heads)
    assert heads % hb == 0 and cb_q % hb == 0 and cb_z % hb == 0
    nhb = heads // hb
    nc = t // CHUNK
    nseq = state_in.shape[0]
    L = CHUNK

    def im(f):
        return lambda h, c, s, st, en: f(h, c, s)

    grid_spec = pltpu.PrefetchScalarGridSpec(
        num_scalar_prefetch=3,
        grid=(nhb, nc),
        in_specs=[
            pl.BlockSpec((hb, L, LANE), im(lambda h, c, s: (cb_q // hb + h, c, 0))),
            pl.BlockSpec((hb, L, LANE), im(lambda h, c, s: ((cb_q + heads) // hb + h, c, 0))),
            pl.BlockSpec((hb, L, LANE), im(lambda h, c, s: ((cb_q + 2 * heads) // hb + h, c, 0))),
            pl.BlockSpec((hb, L, LANE), im(lambda h, c, s: (cb_z // hb + h, c, 0))),
            pl.BlockSpec((1, L, LANE), im(lambda h, c, s: (cb_sm, c, 0))),
            pl.BlockSpec((1, 1, 3 * hb, CONV_K - 1, LANE), im(lambda h, c, s: (s[c], h, 0, 0, 0))),
            pl.BlockSpec((1, hb, GDN_D, GDN_D), im(lambda h, c, s: (s[c], h, 0, 0))),
            pl.BlockSpec((1, CONV_K, 3 * hb, 1, LANE), im(lambda h, c, s: (h, 0, 0, 0, 0))),
            pl.BlockSpec((1, LANE), im(lambda h, c, s: (0, 0))),
            pl.BlockSpec((1, LANE), im(lambda h, c, s: (0, 0))),
            pl.BlockSpec((1, LANE), im(lambda h, c, s: (0, 0))),
        ],
        out_specs=[
            pl.BlockSpec((L, hb * LANE), im(lambda h, c, s: (c, h))),
            pl.BlockSpec((1, hb, GDN_D, GDN_D), im(lambda h, c, s: (s[c], h, 0, 0))),
        ],
        scratch_shapes=[pltpu.VMEM((3 * hb, L + SUBLANE, LANE), F32), pltpu.VMEM((hb, GDN_D, GDN_D), F32)],
    )
    return pl.pallas_call(
        functools.partial(_gdn_kernel, lane_beta=lane_beta, lane_a=lane_a),
        grid_spec=grid_spec,
        out_shape=[jax.ShapeDtypeStruct((t, heads * GDN_D), BF16),
                   jax.ShapeDtypeStruct((nseq, heads, GDN_D, GDN_D), F32)],
        compiler_params=_params(("arbitrary", "arbitrary")),
        name="gdn_mixer",
    )(seq_id, start, end, proj, proj, proj, proj, proj, conv_in, state_in, conv_w, dtb, alog, nw)


def _outproj_kernel(a1_ref, a2_ref, w1_ref, w2_ref, xa_ref, xb_ref, o_ref, *, na):
    acc = _dot(a1_ref[...], w1_ref[...]) + _dot(a2_ref[...], w2_ref[...])

    @pl.when(pl.program_id(0) < na)
    def _():
        o_ref[...] = xa_ref[...] + acc

    @pl.when(pl.program_id(0) >= na)
    def _():
        o_ref[...] = xb_ref[...] + acc


def _outproj(a1, a2, w1, w2, xa, xb, tm, tn):
    t, k1 = a1.shape
    k2 = a2.shape[1]
    d = w1.shape[1]
    na = xa.shape[0] // tm
    spec_a, spec_b = _two_source_specs((tm, tn), na, col=d // tn - 1)
    return pl.pallas_call(
        functools.partial(_outproj_kernel, na=na),
        grid=(t // tm, d // tn),
        in_specs=[pl.BlockSpec((tm, k1), lambda i, j: (i, 0)),
                  pl.BlockSpec((tm, k2), lambda i, j: (i, 0)),
                  pl.BlockSpec((k1, tn), lambda i, j: (0, j)),
                  pl.BlockSpec((k2, tn), lambda i, j: (0, j)),
                  spec_a, spec_b],
        out_specs=pl.BlockSpec((tm, tn), lambda i, j: (i, j)),
        out_shape=jax.ShapeDtypeStruct((t, d), F32),
        compiler_params=_params(("arbitrary", "arbitrary")),
        name="outproj",
    )(a1, a2, w1, w2, xa, xb)


def _router_kernel(x_ref, nw_ref, wr_ref, br_ref, idx_ref, gate_ref):
    x = x_ref[...]
    h = (x * lax.rsqrt(jnp.mean(x * x, axis=-1, keepdims=True) + EPS)) * nw_ref[...]
    logits = _dot(h, wr_ref[...], precision=HIGHEST) + br_ref[...]
    lane = lax.broadcasted_iota(I32, logits.shape, 1)
    vals = logits
    idx_out = jnp.zeros(logits.shape, I32)
    top = []
    for kk in range(TOP_K):
        m = jnp.max(vals, axis=-1, keepdims=True)
        sel = jnp.min(jnp.where(vals == m, lane, LANE), axis=-1, keepdims=True)
        idx_out = jnp.where(lane == kk, sel, idx_out)
        top.append(m)
        vals = jnp.where(lane == sel, -jnp.inf, vals)
    es = [jnp.exp(m - top[0]) for m in top]
    den = es[0]
    for e in es[1:]:
        den = den + e
    gate_out = jnp.zeros(logits.shape, F32)
    for kk in range(TOP_K):
        gate_out = jnp.where(lane == kk, es[kk] / den, gate_out)
    idx_ref[...] = idx_out
    gate_ref[...] = gate_out


def _router(x1, nw, wr, br, tm):
    t, d = x1.shape
    return pl.pallas_call(
        _router_kernel,
        grid=(t // tm,),
        in_specs=[pl.BlockSpec((tm, d), lambda i: (i, 0)),
                  pl.BlockSpec((1, d), lambda i: (0, 0)),
                  pl.BlockSpec((d, LANE), lambda i: (0, 0)),
                  pl.BlockSpec((1, LANE), lambda i: (0, 0))],
        out_specs=[pl.BlockSpec((tm, LANE), lambda i: (i, 0)), pl.BlockSpec((tm, LANE), lambda i: (i, 0))],
        out_shape=[jax.ShapeDtypeStruct((t, LANE), I32), jax.ShapeDtypeStruct((t, LANE), F32)],
        compiler_params=_params(("parallel",)),
        name="router",
    )(x1, nw, wr, br)


def _row_copy(src_hbm, dst_vmem, sem, src_row, dst_row):
    return pltpu.make_async_copy(src_hbm.at[pl.ds(src_row, 1), :], dst_vmem.at[pl.ds(dst_row, 1), :], sem)


def _drain_rows(src_hbm, dst_vmem, sem, n):
    def body(r, carry):
        _row_copy(src_hbm, dst_vmem, sem, 0, r).wait()
        return carry
    lax.fori_loop(0, n, body, 0)


def _dispatch_kernel(tok_ref, blk_ref, starts_ref, pstarts_ref, counts_ref, nused_ref,
                     x_hbm, nw_ref, o_ref, buf, sem):
    b = pl.program_id(0)
    bm = o_ref.shape[0]
    n_used = nused_ref[0]
    n_assign = tok_ref.shape[0]
    slot = lax.rem(b, 2)

    def issue(blk, slot_):
        e = blk_ref[blk]
        off0 = blk * bm - pstarts_ref[e]
        base = starts_ref[e] + off0
        n_valid = counts_ref[e] - off0

        def body(r, carry):
            tok = jnp.where(r < n_valid, tok_ref[jnp.minimum(base + r, n_assign - 1)], 0)
            _row_copy(x_hbm, buf.at[slot_], sem.at[slot_], tok, r).start()
            return carry
        lax.fori_loop(0, bm, body, 0)

    @pl.when(jnp.logical_and(b == 0, n_used > 0))
    def _():
        issue(0, 0)

    @pl.when(b + 1 < n_used)
    def _():
        issue(b + 1, 1 - slot)

    @pl.when(b < n_used)
    def _():
        _drain_rows(x_hbm, buf.at[slot], sem.at[slot], bm)
        x = buf[slot]
        h = (x * lax.rsqrt(jnp.mean(x * x, axis=-1, keepdims=True) + EPS)) * nw_ref[...]
        o_ref[...] = h.astype(o_ref.dtype)

    @pl.when(b >= n_used)
    def _():
        o_ref[...] = jnp.zeros(o_ref.shape, o_ref.dtype)


def _dispatch(order_tok, blk_e, starts, pstarts, counts, n_used, x1, nw, nblk):
    t, d = x1.shape
    grid_spec = pltpu.PrefetchScalarGridSpec(
        num_scalar_prefetch=6,
        grid=(nblk,),
        in_specs=[pl.BlockSpec(memory_space=pl.ANY),
                  pl.BlockSpec((1, d), lambda b, *_: (0, 0))],
        out_specs=pl.BlockSpec((MOE_BM, d), lambda b, *_: (b, 0)),
        scratch_shapes=[pltpu.VMEM((2, MOE_BM, d), F32), pltpu.SemaphoreType.DMA((2,))],
    )
    return pl.pallas_call(
        _dispatch_kernel,
        grid_spec=grid_spec,
        out_shape=jax.ShapeDtypeStruct((nblk * MOE_BM, d), BF16),
        compiler_params=_params(("arbitrary",)),
        name="moe_dispatch",
    )(order_tok, blk_e, starts, pstarts, counts, n_used, x1, nw)


def _stream_expert_blocks(blk0_ref, nblk_ref, src_hbm, dst_hbm, xbuf, obuf, sem_in, sem_out, prepare, compute):
    n = pl.program_id(0)
    e = pl.program_id(1)
    n_blocks = nblk_ref[e]
    first = blk0_ref[e]
    bm = xbuf.shape[1]
    tn = obuf.shape[2]
    col = pl.multiple_of(n * tn, tn)

    def rows(i):
        return pl.ds(pl.multiple_of((first + i) * bm, bm), bm)

    def x_copy(i, slot):
        return pltpu.make_async_copy(src_hbm.at[rows(i), :], xbuf.at[slot], sem_in.at[slot])

    def o_copy(i, slot):
        return pltpu.make_async_copy(obuf.at[slot], dst_hbm.at[rows(i), pl.ds(col, tn)], sem_out.at[slot])

    @pl.when(n_blocks > 0)
    def _():
        x_copy(0, 0).start()
        prepare()

        def step(i, carry):
            slot = lax.rem(i, 2)
            x_copy(i, slot).wait()

            @pl.when(i + 1 < n_blocks)
            def _():
                x_copy(i + 1, 1 - slot).start()

            @pl.when(i >= 2)
            def _():
                o_copy(i - 2, slot).wait()

            obuf[slot] = compute(xbuf[slot])
            o_copy(i, slot).start()
            return carry
        lax.fori_loop(0, n_blocks, step, 0)

        @pl.when(n_blocks >= 2)
        def _():
            o_copy(n_blocks - 2, lax.rem(n_blocks, 2)).wait()
        o_copy(n_blocks - 1, lax.rem(n_blocks - 1, 2)).wait()

    @pl.when(e == pl.num_programs(1) - 1)
    def _():
        n_spare = dst_hbm.shape[0] // bm - (first + n_blocks)
        obuf[0] = jnp.zeros(obuf.shape[1:], obuf.dtype)
        lax.fori_loop(0, n_spare, lambda i, c: (o_copy(n_blocks + i, 0).start(), c)[1], 0)
        lax.fori_loop(0, n_spare, lambda i, c: (o_copy(n_blocks + i, 0).wait(), c)[1], 0)


def _gateup_kernel(blk0_ref, nblk_ref, x_hbm, wg_ref, wu_ref, bg_ref, bu_ref, act_hbm,
                   wg_scr, wu_scr, xbuf, obuf, sem_in, sem_out):
    def prepare():
        wg_scr[...] = wg_ref[0].astype(BF16)
        wu_scr[...] = wu_ref[0].astype(BF16)

    def compute(x):
        gate = jnp.minimum(_dot(x, wg_scr[...]) + bg_ref[0], SWIGLU_LIMIT)
        up = jnp.clip(_dot(x, wu_scr[...]) + bu_ref[0], -SWIGLU_LIMIT, SWIGLU_LIMIT)
        act = gate * jax.nn.sigmoid(SWIGLU_ALPHA * gate) * (up + 1.0)
        return act.astype(obuf.dtype)

    _stream_expert_blocks(blk0_ref, nblk_ref, x_hbm, act_hbm, xbuf, obuf, sem_in, sem_out, prepare, compute)


def _gateup(blk0, nblk_e, xs, wg, wu, bg, bu, tn):
    p, d = xs.shape
    n_exp, _, dff = wg.shape
    w_spec = pl.BlockSpec((1, d, tn), lambda n, e, *_: (e, 0, n))
    b_spec = pl.BlockSpec((1, 1, tn), lambda n, e, *_: (e, 0, n))
    grid_spec = pltpu.PrefetchScalarGridSpec(
        num_scalar_prefetch=2,
        grid=(dff // tn, n_exp),
        in_specs=[pl.BlockSpec(memory_space=pl.ANY), w_spec, w_spec, b_spec, b_spec],
        out_specs=pl.BlockSpec(memory_space=pl.ANY),
        scratch_shapes=[pltpu.VMEM((d, tn), BF16), pltpu.VMEM((d, tn), BF16),
                        pltpu.VMEM((2, MOE_BM, d), BF16), pltpu.VMEM((2, MOE_BM, tn), BF16),
                        pltpu.SemaphoreType.DMA((2,)), pltpu.SemaphoreType.DMA((2,))],
    )
    return pl.pallas_call(
        _gateup_kernel,
        grid_spec=grid_spec,
        out_shape=jax.ShapeDtypeStruct((p, dff), BF16),
        compiler_params=_params(("arbitrary", "arbitrary")),
        name="moe_gateup",
    )(blk0, nblk_e, xs, wg, wu, bg, bu)


def _down_kernel(blk0_ref, nblk_ref, a_hbm, wd_ref, bd_ref, y_hbm, wd_scr, xbuf, obuf, sem_in, sem_out):
    def prepare():
        wd_scr[...] = wd_ref[0].astype(BF16)

    def compute(a):
        return _dot(a, wd_scr[...]) + bd_ref[0]

    _stream_expert_blocks(blk0_ref, nblk_ref, a_hbm, y_hbm, xbuf, obuf, sem_in, sem_out, prepare, compute)


def _down(blk0, nblk_e, act, wd, bd, tn):
    p, dff = act.shape
    n_exp, _, d = wd.shape
    grid_spec = pltpu.PrefetchScalarGridSpec(
        num_scalar_prefetch=2,
        grid=(d // tn, n_exp),
        in_specs=[pl.BlockSpec(memory_space=pl.ANY),
                  pl.BlockSpec((1, dff, tn), lambda n, e, *_: (e, 0, n)),
                  pl.BlockSpec((1, 1, tn), lambda n, e, *_: (e, 0, n))],
        out_specs=pl.BlockSpec(memory_space=pl.ANY),
        scratch_shapes=[pltpu.VMEM((dff, tn), BF16),
                        pltpu.VMEM((2, MOE_BM, dff), BF16), pltpu.VMEM((2, MOE_BM, tn), F32),
                        pltpu.SemaphoreType.DMA((2,)), pltpu.SemaphoreType.DMA((2,))],
    )
    return pl.pallas_call(
        _down_kernel,
        grid_spec=grid_spec,
        out_shape=jax.ShapeDtypeStruct((p, d), F32),
        compiler_params=_params(("arbitrary", "arbitrary")),
        name="moe_down",
    )(blk0, nblk_e, act, wd, bd)


def _combine_kernel(pos_ref, ys_hbm, x_ref, g_ref, nw_ref, o_ref, buf, sem, *, tile0):
    i = pl.program_id(0)
    tt = o_ref.shape[0]
    n = TOP_K * tt
    slot = lax.rem(i, 2)

    def issue(tile, slot_):
        def body(r, carry):
            _row_copy(ys_hbm, buf.at[slot_], sem.at[slot_], pos_ref[(tile0 + tile) * n + r], r).start()
            return carry
        lax.fori_loop(0, n, body, 0)

    @pl.when(i == 0)
    def _():
        issue(0, 0)

    @pl.when(i + 1 < pl.num_programs(0))
    def _():
        issue(i + 1, 1 - slot)

    _drain_rows(ys_hbm, buf.at[slot], sem.at[slot], n)
    g = g_ref[...]
    acc = x_ref[...]
    for kk in range(TOP_K):
        acc = acc + buf[slot, kk * tt:(kk + 1) * tt, :] * g[:, kk:kk + 1]
    y = acc * lax.rsqrt(jnp.mean(acc * acc, axis=-1, keepdims=True) + EPS)
    o_ref[...] = y * nw_ref[...]


def _combine(pos_tiles, ys, x1, gates, nw, tt, tile0, n_tiles):
    d = x1.shape[1]
    grid_spec = pltpu.PrefetchScalarGridSpec(
        num_scalar_prefetch=1,
        grid=(n_tiles,),
        in_specs=[pl.BlockSpec(memory_space=pl.ANY),
                  pl.BlockSpec((tt, d), lambda i, pos: (tile0 + i, 0)),
                  pl.BlockSpec((tt, LANE), lambda i, pos: (tile0 + i, 0)),
                  pl.BlockSpec((1, d), lambda i, pos: (0, 0))],
        out_specs=pl.BlockSpec((tt, d), lambda i, pos: (i, 0)),
        scratch_shapes=[pltpu.VMEM((2, TOP_K * tt, d), F32), pltpu.SemaphoreType.DMA((2,))],
    )
    return pl.pallas_call(
        functools.partial(_combine_kernel, tile0=tile0),
        grid_spec=grid_spec,
        out_shape=jax.ShapeDtypeStruct((n_tiles * tt, d), F32),
        compiler_params=_params(("arbitrary",)),
        name="moe_combine",
    )(pos_tiles, ys, x1, gates, nw)


def _pad_lanes(v, offset):
    out = jnp.zeros((LANE,), F32)
    return lax.dynamic_update_slice(out, v.astype(F32), (offset,)).reshape(1, LANE)


def _ssd_group_layout(a, width):
    lead = a.shape[:-2]
    rows = a.shape[-2]
    rb = width // LANE // SSD_G
    xs = a[..., :width].reshape(*lead, rows, SSD_G, rb, LANE)
    bs = a[..., width:width + SSD_G * SSD_N].reshape(*lead, rows, SSD_G, 1, LANE)
    cs = a[..., width + SSD_G * SSD_N:].reshape(*lead, rows, SSD_G, 1, LANE)
    cat = jnp.concatenate([xs, bs, cs], axis=-2)
    n = cat.ndim
    return jnp.moveaxis(cat, n - 4, n - 2)


def _gdn_block_layout(a, heads, hb):
    lead = a.shape[:-2]
    rows = a.shape[-2]
    r = a.reshape(*lead, rows, 3, heads // hb, hb, LANE)
    n = r.ndim
    r = jnp.moveaxis(r, n - 5, n - 2)
    r = jnp.moveaxis(r, n - 5, n - 4)
    return r.reshape(*lead, heads // hb, 3 * hb, rows, LANE)


def _pick_tn_cb(ncb):
    best = None
    for cand in (10, 8, 6, 4, 2):
        pad = (-ncb) % cand
        if best is None or pad < best[1]:
            best = (cand, pad)
    return best


def kernel(x_prompt, x_sample, state_ssd_conv, state_ssd, state_gdn_conv, state_gdn, norm_mix, w_in, ssd_conv_w,
           ssd_conv_b, ssd_dt_bias, ssd_A_log, ssd_D, ssd_norm, gdn_conv_w, gdn_dt_bias, gdn_A_log, gdn_norm, w_out,
           norm_ffn, w_router, b_router, w_gate, b_gate, w_up, b_up, w_down, b_down, norm_final):
    assert w_in.shape[0] == 1, "single layer"
    nb_p, seq_p, d = x_prompt.shape
    nb_s, seq_s, _ = x_sample.shape
    assert seq_p % CHUNK == 0 and seq_s % CHUNK == 0
    ssd_heads = d // SSD_P
    ssd_w = ssd_heads * SSD_P
    ssd_cs = ssd_w + 2 * SSD_G * SSD_N
    gdn_heads = d // GDN_D
    gdn_w = gdn_heads * GDN_D
    t_p = nb_p * seq_p
    t = t_p + nb_s * seq_s
    nseq = nb_p + nb_s

    x_p = x_prompt.reshape(t_p, d)
    x_s = x_sample.reshape(nb_s * seq_s, d)
    seq_len = [seq_p] * nb_p + [seq_s] * nb_s
    seq_id, start, end = [], [], []
    for s, n in enumerate(seq_len):
        for cidx in range(n // CHUNK):
            seq_id.append(s)
            start.append(int(cidx == 0))
            end.append(int(cidx == n // CHUNK - 1))
    seq_id = jnp.asarray(np.array(seq_id, np.int32))
    start = jnp.asarray(np.array(start, np.int32))
    end = jnp.asarray(np.array(end, np.int32))

    o_z, o_xbc, o_dt = 0, ssd_w, ssd_w + ssd_cs
    o_qkv = o_dt + ssd_heads
    o_zg = o_qkv + 3 * gdn_w
    o_b = o_zg + gdn_w
    o_a = o_b + gdn_heads
    w0 = w_in[0]
    n_small = ssd_heads + 2 * gdn_heads
    assert n_small <= LANE
    cb_q = 0
    cb_zg = cb_q + 3 * gdn_w // LANE
    cb_z = cb_zg + gdn_w // LANE
    cb_x = cb_z + ssd_w // LANE
    cb_sm = cb_x + ssd_cs // LANE
    ncb = cb_sm + 1
    tn_cb, pad_cb = _pick_tn_cb(ncb)
    w_perm = jnp.concatenate([
        w0[:, o_qkv:o_qkv + 3 * gdn_w], w0[:, o_zg:o_zg + gdn_w], w0[:, o_z:o_z + ssd_w],
        w0[:, o_xbc:o_xbc + ssd_cs], w0[:, o_dt:o_dt + ssd_heads], w0[:, o_b:o_b + 2 * gdn_heads],
        jnp.zeros((d, LANE - n_small + pad_cb * LANE), F32)], axis=1).astype(BF16)
    lane_beta = ssd_heads
    lane_a = ssd_heads + gdn_heads

    tm_big = _row_tile(t, 768)
    tm_mid = _row_tile(t, 528)
    tm_src = _row_tile(int(np.gcd(t_p, t - t_p)), 512)
    h = _rmsnorm_cast(x_p, x_s, norm_mix[0], tm_src)
    proj = _inproj(h, w_perm, tm_big, tn_cb)

    def with_zero_prompt(a):
        return jnp.concatenate([jnp.zeros((nb_p,) + a.shape[1:], a.dtype), a], axis=0)

    ssd_conv0 = _ssd_group_layout(with_zero_prompt(state_ssd_conv[0]), ssd_w)
    gdn_hb = min(GDN_HB, gdn_heads)
    gdn_conv0 = _gdn_block_layout(with_zero_prompt(state_gdn_conv[0]), gdn_heads, gdn_hb)
    s0 = with_zero_prompt(state_ssd[0])
    ssd_s0 = s0.reshape(nseq, ssd_heads // 2, 2, SSD_P, SSD_N).transpose(0, 1, 4, 2, 3).reshape(
        nseq, ssd_heads // 2, SSD_N, LANE)
    gdn_s0 = with_zero_prompt(state_gdn[0])

    ssd_cw = _ssd_group_layout(ssd_conv_w[0][None], ssd_w)[0]
    ssd_cw = jnp.swapaxes(ssd_cw, 1, 2)[:, :, :, None, :]
    ssd_cb = _ssd_group_layout(ssd_conv_b[0][None, None], ssd_w)[0]
    d_exp = jnp.repeat(ssd_D[0], SSD_P).reshape(ssd_w // LANE, 1, LANE)
    ssd_nw = ssd_norm[0].reshape(ssd_w // LANE, 1, LANE)
    y_ssd, ssd_s = _ssd_mixer(
        proj, seq_id, start, end, ssd_conv0, ssd_s0, ssd_cw, ssd_cb,
        _pad_lanes(ssd_dt_bias[0], 0), _pad_lanes(ssd_A_log[0], 0), d_exp, ssd_nw,
        t, ssd_w, cb_z, cb_x, cb_x + ssd_w // LANE, cb_x + ssd_w // LANE + SSD_G, cb_sm)

    gdn_cw = _gdn_block_layout(gdn_conv_w[0][None], gdn_heads, gdn_hb)[0]
    gdn_cw = jnp.swapaxes(gdn_cw, 1, 2)[:, :, :, None, :]
    y_gdn, gdn_s = _gdn_mixer(
        proj, seq_id, start, end, gdn_conv0, gdn_s0, gdn_cw,
        _pad_lanes(gdn_dt_bias[0], lane_a), _pad_lanes(gdn_A_log[0], lane_a), gdn_norm[0].reshape(1, LANE),
        t, gdn_heads, cb_q, cb_zg, cb_sm, lane_beta, lane_a)

    w_o = w_out[0].astype(BF16)
    x1 = _outproj(y_ssd, y_gdn, w_o[:ssd_w], w_o[ssd_w:], x_p, x_s, tm_src, min(512, d))

    wr = jnp.concatenate([w_router[0], jnp.zeros((d, LANE - N_EXPERTS), F32)], axis=1)
    br = jnp.concatenate([b_router[0], jnp.full((LANE - N_EXPERTS,), NEG_BIG, F32)]).reshape(1, LANE)
    nffn = norm_ffn[0].reshape(1, d)
    idx_pad, gate_pad = _router(x1, nffn, wr, br, tm_mid)
    top_idx = idx_pad[:, :TOP_K]
    tk = t * TOP_K
    flat_e = top_idx.reshape(tk)
    order = jnp.argsort(flat_e).astype(I32)
    sorted_e = flat_e[order]
    counts = jnp.sum((flat_e[:, None] == jnp.arange(N_EXPERTS, dtype=I32)[None, :]).astype(I32), axis=0)
    starts = jnp.cumsum(counts) - counts
    pcounts = (counts + MOE_BM - 1) // MOE_BM * MOE_BM
    pends = jnp.cumsum(pcounts)
    pstarts = pends - pcounts
    dest = (pstarts[sorted_e] + (jnp.arange(tk, dtype=I32) - starts[sorted_e])).astype(I32)
    nblk = -(-tk // MOE_BM) + N_EXPERTS
    pos = jnp.zeros((tk,), I32).at[order].set(dest)
    blk_e = jnp.minimum(jnp.searchsorted(pends, jnp.arange(nblk, dtype=I32) * MOE_BM, side='right'),
                        N_EXPERTS - 1).astype(I32)
    n_used = (pends[-1] // MOE_BM).astype(I32).reshape(1)
    blk0 = (pstarts // MOE_BM).astype(I32)
    nblk_e = (pcounts // MOE_BM).astype(I32)

    xs = _dispatch(order // TOP_K, blk_e, starts.astype(I32), pstarts.astype(I32), counts, n_used, x1, nffn, nblk)
    tn_ff = min(512, d)
    act = _gateup(blk0, nblk_e, xs, w_gate[0], w_up[0], b_gate[0][:, None, :], b_up[0][:, None, :], tn_ff)
    ys = _down(blk0, nblk_e, act, w_down[0], b_down[0][:, None, :], tn_ff)

    tt = CHUNK
    pos_tiles = pos.reshape(t // tt, tt, TOP_K).transpose(0, 2, 1).reshape(tk)
    nfin = norm_final.reshape(1, d)
    y_prompt = _combine(pos_tiles, ys, x1, gate_pad, nfin, tt, 0, t_p // tt).reshape(nb_p, seq_p, d)
    y_sample = _combine(pos_tiles, ys, x1, gate_pad, nfin, tt, t_p // tt, (t - t_p) // tt).reshape(nb_s, seq_s, d)

    def last_rows(cb0, ncols):
        nblk_c = ncols // LANE
        ends = np.cumsum(seq_len)
        rows = jnp.concatenate(
            [lax.slice(proj, (cb0, int(e) - (CONV_K - 1), 0), (cb0 + nblk_c, int(e), LANE)) for e in ends], axis=1)
        a = rows.reshape(nblk_c, nseq, CONV_K - 1, LANE).transpose(1, 2, 0, 3).reshape(nseq, CONV_K - 1, ncols)
        return a[:nb_p][None], a[nb_p:][None]

    ssd_conv_p, ssd_conv_s = last_rows(cb_x, ssd_cs)
    gdn_conv_p, gdn_conv_s = last_rows(cb_q, 3 * gdn_w)
    ssd_state = ssd_s.reshape(nseq, ssd_heads // 2, SSD_N, 2, SSD_P).transpose(0, 1, 3, 4, 2).reshape(
        nseq, ssd_heads, SSD_P, SSD_N)
    return (y_prompt, y_sample,
            ssd_conv_p, ssd_state[:nb_p][None], gdn_conv_p, gdn_s[:nb_p][None],
            ssd_conv_s, ssd_state[nb_p:][None], gdn_conv_s, gdn_s[nb_p:][None])
```

```python
import functools

import numpy as np
import jax
import jax.numpy as jnp
from jax import lax
from jax.experimental import pallas as pl
from jax.experimental.pallas import tpu as pltpu

F32 = jnp.float32
BF16 = jnp.bfloat16
I32 = jnp.int32

LANE = 128
SUBLANE = 8
VMEM_LIMIT = 56 * 1024 * 1024

CHUNK = 64
CONV_K = 4
SSD_P = 64
SSD_N = 128
SSD_G = 8
GDN_D = 128
N_EXPERTS = 32
TOP_K = 4
SWIGLU_LIMIT = 7.0
SWIGLU_ALPHA = 1.702
EPS = 1e-6
NEG_BIG = -1e30

MOE_BM = 256
MOE_SB_BLOCKS = 10
MOE_M_LADDER = (1, 2, 4, 6, 8, 9, 10)
MOE_TN = 256
GATHER_UNROLL = 8
INPROJ_TN_CB = 4
GDN_HB = 32
SSD_GS = 8
HIGHEST = lax.Precision.HIGHEST


def _row_tile(n, target, mult=16):
    best = None
    for t in range(mult, min(n, target) + 1, mult):
        if n % t == 0:
            best = t
    assert best is not None, (n, target)
    return best


def _params(sem):
    return pltpu.CompilerParams(dimension_semantics=sem, vmem_limit_bytes=VMEM_LIMIT)


def _silu(x):
    return x * jax.nn.sigmoid(x)


def _softplus(x):
    return jnp.maximum(x, 0.0) + jnp.log1p(jnp.exp(-jnp.abs(x)))


def _dot(a, b, **kw):
    return jnp.dot(a, b, preferred_element_type=F32, **kw)


def _split3(a):
    hi = a.astype(BF16)
    r = a - hi.astype(F32)
    mid = r.astype(BF16)
    lo = (r - mid.astype(F32)).astype(BF16)
    return hi, mid, lo


def _two_source_specs(block, na, col=None):
    if col is None:
        ia = lambda i, *_: (jnp.minimum(i, na - 1), 0)
        ib = lambda i, *_: (jnp.maximum(i - na, 0), 0)
    else:
        ia = lambda i, j, *_: (jnp.minimum(i, na - 1), jnp.where(i < na, j, col))
        ib = lambda i, j, *_: (jnp.maximum(i - na, 0), jnp.where(i >= na, j, 0))
    return pl.BlockSpec(block, ia), pl.BlockSpec(block, ib)


def _rmsnorm_cast_kernel(xa_ref, xb_ref, w_ref, o_ref, *, na):
    def body(x_ref):
        x = x_ref[...]
        xn = x * lax.rsqrt(jnp.mean(x * x, axis=-1, keepdims=True) + EPS)
        o_ref[...] = (xn * w_ref[...]).astype(o_ref.dtype)

    pl.when(pl.program_id(0) < na)(lambda: body(xa_ref))
    pl.when(pl.program_id(0) >= na)(lambda: body(xb_ref))


def _rmsnorm_cast(xa, xb, w, tm):
    ta, d = xa.shape
    tb = xb.shape[0]
    na = ta // tm
    spec_a, spec_b = _two_source_specs((tm, d), na)
    return pl.pallas_call(
        functools.partial(_rmsnorm_cast_kernel, na=na),
        grid=(na + tb // tm,),
        in_specs=[spec_a, spec_b, pl.BlockSpec((1, d), lambda i: (0, 0))],
        out_specs=pl.BlockSpec((tm, d), lambda i: (i, 0)),
        out_shape=jax.ShapeDtypeStruct((ta + tb, d), BF16),
        compiler_params=_params(("arbitrary",)),
        name="rmsnorm_cast",
    )(xa, xb, w.reshape(1, d))


def _inproj_kernel(a_ref, w_ref, o_ref):
    ncb = o_ref.shape[0]
    a = a_ref[...]
    step = 2 if ncb % 2 == 0 else 1
    for j in range(0, ncb, step):
        acc = _dot(a, w_ref[:, j * LANE:(j + step) * LANE])
        for s in range(step):
            o_ref[j + s] = acc[:, s * LANE:(s + 1) * LANE]


def _inproj(a, w, tm, tn_cb):
    t, d = a.shape
    n = w.shape[1]
    ncb = n // LANE
    return pl.pallas_call(
        _inproj_kernel,
        grid=(t // tm, ncb // tn_cb),
        in_specs=[pl.BlockSpec((tm, d), lambda i, j: (i, 0)),
                  pl.BlockSpec((d, tn_cb * LANE), lambda i, j: (0, j))],
        out_specs=pl.BlockSpec((tn_cb, tm, LANE), lambda i, j: (j, i, 0)),
        out_shape=jax.ShapeDtypeStruct((ncb, t, LANE), F32),
        compiler_params=_params(("parallel", "arbitrary")),
        name="inproj",
    )(a, w)


def _causal_conv(u, ext_scr, conv_in, taps, is_start):
    L = u.shape[1]
    base = SUBLANE - (CONV_K - 1)

    @pl.when(is_start)
    def _():
        ext_scr[:, base:SUBLANE, :] = conv_in()

    ext_scr[:, SUBLANE:SUBLANE + L, :] = u
    acc = ext_scr[:, base:base + L, :] * taps(0)
    for j in range(1, CONV_K):
        acc = acc + ext_scr[:, base + j:base + j + L, :] * taps(j)
    ext_scr[:, base:SUBLANE, :] = ext_scr[:, base + L:SUBLANE + L, :]
    return acc


def _cumsum_rows(a, incl):
    m = incl.astype(BF16)
    hi, mid, lo = _split3(a)
    return _dot(m, hi) + (_dot(m, mid) + _dot(m, lo))


def _select_dot(a, onehot):
    oh = onehot.astype(BF16)
    hi, mid, lo = _split3(a)
    return _dot(hi, oh) + (_dot(mid, oh) + _dot(lo, oh))


def _ssd_kernel(seq_ref, start_ref, end_ref,
                z_ref, x_ref, b_ref, c_ref, sm_ref, cin_ref, sin_ref, cw_ref, cb_ref,
                dtb_ref, alog_ref, d_ref, nw_ref,
                y_ref, sout_ref,
                ext_scr, s_scr):
    gi0 = pl.program_id(0)
    c = pl.program_id(1)
    gs = b_ref.shape[0]
    rb = x_ref.shape[0] // gs
    nb = rb + 2
    L = x_ref.shape[1]
    is_start = start_ref[c] == 1

    @pl.when(is_start)
    def _():
        s_scr[...] = sin_ref[0]

    u = jnp.concatenate(
        [p for gi in range(gs) for p in (x_ref[gi * rb:(gi + 1) * rb], b_ref[gi:gi + 1], c_ref[gi:gi + 1])], axis=0)
    conv = _causal_conv(u, ext_scr, lambda: cin_ref[0].reshape(gs * nb, CONV_K - 1, LANE),
                        lambda j: cw_ref[:, j].reshape(gs * nb, 1, LANE), is_start)
    uc = _silu(conv + cb_ref[...].reshape(gs * nb, 1, LANE))

    sm = sm_ref[0]
    dt_all = _softplus(sm + dtb_ref[...])
    a_all = dt_all * (-jnp.exp(alog_ref[...]))
    ti = lax.broadcasted_iota(I32, (L, L), 0)
    si = lax.broadcasted_iota(I32, (L, L), 1)
    cum_all = _cumsum_rows(a_all, ti >= si)
    ej = lax.broadcasted_iota(I32, (LANE, gs * rb * LANE), 0)
    ec = lax.broadcasted_iota(I32, (LANE, gs * rb * LANE), 1)
    expand = ej == gi0 * (gs * 2 * rb) + jnp.right_shift(ec, 6)
    dtx = _select_dot(dt_all, expand)
    cumx = _select_dot(cum_all, expand)

    t2 = lax.broadcasted_iota(I32, (L, LANE), 0)
    l2 = lax.broadcasted_iota(I32, (L, LANE), 1)
    s2 = jnp.bitwise_and(l2, SSD_P - 1)
    diag2 = (t2 == s2).astype(F32)
    causal2 = t2 >= s2
    left = l2 < SSD_P

    for gi in range(gs):
        bm = uc[gi * nb + rb]
        cm = uc[gi * nb + rb + 1]
        b2 = jnp.concatenate([bm, bm], axis=0).astype(BF16)
        cm_b = cm.astype(BF16)
        bm_b = bm.astype(BF16)
        cb2 = lax.dot_general(cm_b, b2, (((1,), (1,)), ((), ())), preferred_element_type=F32)
        ygs = []
        ms = jnp.zeros((L, 1), F32)
        for j in range(rb):
            jj = gi * rb + j
            ccol = cumx[:, jj * LANE:(jj + 1) * LANE]
            dtc = dtx[:, jj * LANE:(jj + 1) * LANE]
            crow = jnp.sum(ccol * diag2, axis=0, keepdims=True)
            dec = jnp.exp(jnp.where(causal2, ccol - crow, NEG_BIG))
            ww = (cb2 * dec).astype(BF16)
            xb = uc[gi * nb + j]
            xdt = xb * dtc
            xbd = jnp.concatenate([jnp.where(left, xdt, 0.0), jnp.where(left, 0.0, xdt)], axis=0).astype(BF16)
            y = _dot(ww, xbd)
            sj = s_scr[jj]
            y = y + _dot(cm_b, sj.astype(BF16)) * jnp.exp(ccol)
            y = y + d_ref[jj] * xb
            cl = ccol[L - 1:L, :]
            xw = (xdt * jnp.exp(cl - ccol)).astype(BF16)
            s_scr[jj] = sj * jnp.exp(cl) + lax.dot_general(
                bm_b, xw, (((0,), (0,)), ((), ())), preferred_element_type=F32)
            yg = y * _silu(z_ref[jj])
            ms = ms + jnp.sum(yg * yg, axis=-1, keepdims=True)
            ygs.append(yg)
        inv = lax.rsqrt(ms / (rb * LANE) + EPS)
        for j in range(rb):
            jj = gi * rb + j
            y_ref[:, jj * LANE:(jj + 1) * LANE] = ((ygs[j] * inv) * nw_ref[jj]).astype(y_ref.dtype)

    @pl.when(end_ref[c] == 1)
    def _():
        sout_ref[0] = s_scr[...]


def _ssd_mixer(proj, seq_id, start, end, conv_in, state_in, conv_w, conv_b, dtb, alog, d_exp, nw,
               t, width, cb_z, cb_x, cb_b, cb_c, cb_sm):
    rb = width // LANE // SSD_G
    nb = rb + 2
    nc = t // CHUNK
    nseq = state_in.shape[0]
    L = CHUNK

    def im(f):
        return lambda g, c, s, st, en: f(g, c, s)

    gs = SSD_GS
    grb = gs * rb
    grid_spec = pltpu.PrefetchScalarGridSpec(
        num_scalar_prefetch=3,
        grid=(SSD_G // gs, nc),
        in_specs=[
            pl.BlockSpec((grb, L, LANE), im(lambda g, c, s: (cb_z // grb + g, c, 0))),
            pl.BlockSpec((grb, L, LANE), im(lambda g, c, s: (cb_x // grb + g, c, 0))),
            pl.BlockSpec((gs, L, LANE), im(lambda g, c, s: (cb_b // gs + g, c, 0))),
            pl.BlockSpec((gs, L, LANE), im(lambda g, c, s: (cb_c // gs + g, c, 0))),
            pl.BlockSpec((1, L, LANE), im(lambda g, c, s: (cb_sm, c, 0))),
            pl.BlockSpec((1, gs, nb, CONV_K - 1, LANE), im(lambda g, c, s: (s[c], g, 0, 0, 0))),
            pl.BlockSpec((1, grb, SSD_N, LANE), im(lambda g, c, s: (s[c], g, 0, 0))),
            pl.BlockSpec((gs, CONV_K, nb, 1, LANE), im(lambda g, c, s: (g, 0, 0, 0, 0))),
            pl.BlockSpec((gs, nb, 1, LANE), im(lambda g, c, s: (g, 0, 0, 0))),
            pl.BlockSpec((1, LANE), im(lambda g, c, s: (0, 0))),
            pl.BlockSpec((1, LANE), im(lambda g, c, s: (0, 0))),
            pl.BlockSpec((grb, 1, LANE), im(lambda g, c, s: (g, 0, 0))),
            pl.BlockSpec((grb, 1, LANE), im(lambda g, c, s: (g, 0, 0))),
        ],
        out_specs=[
            pl.BlockSpec((L, grb * LANE), im(lambda g, c, s: (c, g))),
            pl.BlockSpec((1, grb, SSD_N, LANE), im(lambda g, c, s: (s[c], g, 0, 0))),
        ],
        scratch_shapes=[pltpu.VMEM((gs * nb, L + SUBLANE, LANE), F32), pltpu.VMEM((grb, SSD_N, LANE), F32)],
    )
    assert cb_z % grb == 0 and cb_x % grb == 0 and cb_b % gs == 0 and cb_c % gs == 0 and SSD_G % gs == 0
    return pl.pallas_call(
        _ssd_kernel,
        grid_spec=grid_spec,
        out_shape=[jax.ShapeDtypeStruct((t, width), BF16),
                   jax.ShapeDtypeStruct((nseq, width // LANE, SSD_N, LANE), F32)],
        compiler_params=_params(("arbitrary", "arbitrary")),
        name="ssd_mixer",
    )(seq_id, start, end, proj, proj, proj, proj, proj, conv_in, state_in, conv_w, conv_b, dtb, alog, d_exp, nw)


def _bdot(a, b, ca, cb):
    return lax.dot_general(a, b, (((ca,), (cb,)), ((0,), (0,))), preferred_element_type=F32)


def _bdot3(a, b):
    ah = a.astype(BF16)
    al = (a - ah.astype(F32)).astype(BF16)
    bh = b.astype(BF16)
    bl = (b - bh.astype(F32)).astype(BF16)
    return _bdot(ah, bh, 2, 1) + (_bdot(ah, bl, 2, 1) + _bdot(al, bh, 2, 1))


def _gdn_kernel(seq_ref, start_ref, end_ref,
                q_ref, k_ref, v_ref, z_ref, sm_ref, cin_ref, sin_ref, cw_ref,
                dtb_ref, alog_ref, nw_ref,
                y_ref, sout_ref,
                ext_scr, s_scr, *, lane_beta, lane_a):
    hb_i = pl.program_id(0)
    c = pl.program_id(1)
    hb = q_ref.shape[0]
    L = q_ref.shape[1]
    is_start = start_ref[c] == 1

    @pl.when(is_start)
    def _():
        s_scr[...] = sin_ref[0]

    u = jnp.concatenate([q_ref[...], k_ref[...], v_ref[...]], axis=0)
    uc = _silu(_causal_conv(u, ext_scr, lambda: cin_ref[0, 0], lambda j: cw_ref[0, j], is_start))
    q = uc[:hb]
    k = uc[hb:2 * hb]
    v = uc[2 * hb:]
    q = q * (lax.rsqrt(jnp.sum(q * q, axis=-1, keepdims=True) + EPS) * (GDN_D ** -0.5))
    k = k * lax.rsqrt(jnp.sum(k * k, axis=-1, keepdims=True) + EPS)

    sm = sm_ref[0]
    beta_all = jax.nn.sigmoid(sm)
    g_all = -jnp.exp(alog_ref[...]) * _softplus(sm + dtb_ref[...])
    ti = lax.broadcasted_iota(I32, (L, L), 0)
    si = lax.broadcasted_iota(I32, (L, L), 1)
    incl = ti >= si
    strict = ti > si
    gam_all = _cumsum_rows(g_all, incl)
    ej = lax.broadcasted_iota(I32, (LANE, hb * LANE), 0)
    ec = jnp.right_shift(lax.broadcasted_iota(I32, (LANE, hb * LANE), 1), 7) + hb_i * hb
    betax = _select_dot(beta_all, ej == ec + lane_beta)
    gamx = _select_dot(gam_all, ej == ec + lane_a)
    beta_c = jnp.stack([betax[:, h * LANE:(h + 1) * LANE] for h in range(hb)])
    gam_c = jnp.stack([gamx[:, h * LANE:(h + 1) * LANE] for h in range(hb)])

    t2 = lax.broadcasted_iota(I32, (L, LANE), 0)
    l2 = lax.broadcasted_iota(I32, (L, LANE), 1)
    diag2 = (t2 == l2).astype(F32)
    gam_r = jnp.sum(gam_c * diag2, axis=1, keepdims=True)[:, :, :L]
    gam_t = gam_c[:, :, :L]
    gam_m = jnp.exp(jnp.where(incl, gam_t - gam_r, NEG_BIG))

    kb = k.astype(BF16)
    kk = _bdot(kb, kb, 2, 2)
    a_mat = jnp.where(strict, beta_c[:, :, :L] * kk * gam_m, 0.0)
    n_pow = -a_mat
    x_inv = jnp.where(ti == si, 1.0, 0.0) + n_pow
    span = 2
    while span < L:
        n_pow = _bdot3(n_pow, n_pow)
        x_inv = x_inv + _bdot3(x_inv, n_pow)
        span *= 2

    eg = jnp.exp(gam_c)
    rhs = jnp.concatenate([v * beta_c, k * (beta_c * eg)], axis=-1)
    sol = _bdot3(x_inv, rhs)
    u_ = sol[:, :, :GDN_D]
    w_ = sol[:, :, GDN_D:]
    s_prev = s_scr[...]
    s_b = s_prev.astype(BF16)
    v_new = u_ - _bdot(w_.astype(BF16), s_b, 2, 1)
    vn_b = v_new.astype(BF16)
    qk = _bdot(q.astype(BF16), kb, 2, 2) * gam_m
    o = _bdot((q * eg).astype(BF16), s_b, 2, 1) + _bdot(qk.astype(BF16), vn_b, 2, 1)
    gl = gam_c[:, L - 1:L, :]
    kt = (k * jnp.exp(gl - gam_c)).astype(BF16)
    for h in range(hb):
        upd = lax.dot_general(kt[h], vn_b[h], (((0,), (0,)), ((), ())), preferred_element_type=F32)
        s_scr[h] = s_prev[h] * jnp.exp(gl[h]) + upd

    o = o * lax.rsqrt(jnp.mean(o * o, axis=-1, keepdims=True) + EPS)
    o = (o * nw_ref[...]) * _silu(z_ref[...])
    for h in range(hb):
        y_ref[:, h * LANE:(h + 1) * LANE] = o[h].astype(y_ref.dtype)

    @pl.when(end_ref[c] == 1)
    def _():
        sout_ref[0] = s_scr[...]


def _gdn_mixer(proj, seq_id, start, end, conv_in, state_in, conv_w, dtb, alog, nw,
               t, heads, cb_q, cb_z, cb_sm, lane_beta, lane_a):
    hb = min(GDN_HB, heads)
    assert heads % hb == 0 and cb_q % hb == 0 and cb_z % hb == 0
    nhb = heads // hb
    nc = t // CHUNK
    nseq = state_in.shape[0]
    L = CHUNK

    def im(f):
        return lambda h, c, s, st, en: f(h, c, s)

    grid_spec = pltpu.PrefetchScalarGridSpec(
        num_scalar_prefetch=3,
        grid=(nhb, nc),
        in_specs=[
            pl.BlockSpec((hb, L, LANE), im(lambda h, c, s: (cb_q // hb + h, c, 0))),
            pl.BlockSpec((hb, L, LANE), im(lambda h, c, s: ((cb_q + heads) // hb + h, c, 0))),
            pl.BlockSpec((hb, L, LANE), im(lambda h, c, s: ((cb_q + 2 * heads) // hb + h, c, 0))),
            pl.BlockSpec((hb, L, LANE), im(lambda h, c, s: (cb_z // hb + h, c, 0))),
            pl.BlockSpec((1, L, LANE), im(lambda h, c, s: (cb_sm, c, 0))),
            pl.BlockSpec((1, 1, 3 * hb, CONV_K - 1, LANE), im(lambda h, c, s: (s[c], h, 0, 0, 0))),
            pl.BlockSpec((1, hb, GDN_D, GDN_D), im(lambda h, c, s: (s[c], h, 0, 0))),
            pl.BlockSpec((1, CONV_K, 3 * hb, 1, LANE), im(lambda h, c, s: (h, 0, 0, 0, 0))),
            pl.BlockSpec((1, LANE), im(lambda h, c, s: (0, 0))),
            pl.BlockSpec((1, LANE), im(lambda h, c, s: (0, 0))),
            pl.BlockSpec((1, LANE), im(lambda h, c, s: (0, 0))),
        ],
        out_specs=[
            pl.BlockSpec((L, hb * LANE), im(lambda h, c, s: (c, h))),
            pl.BlockSpec((1, hb, GDN_D, GDN_D), im(lambda h, c, s: (s[c], h, 0, 0))),
        ],
        scratch_shapes=[pltpu.VMEM((3 * hb, L + SUBLANE, LANE), F32), pltpu.VMEM((hb, GDN_D, GDN_D), F32)],
    )
    return pl.pallas_call(
        functools.partial(_gdn_kernel, lane_beta=lane_beta, lane_a=lane_a),
        grid_spec=grid_spec,
        out_shape=[jax.ShapeDtypeStruct((t, heads * GDN_D), BF16),
                   jax.ShapeDtypeStruct((nseq, heads, GDN_D, GDN_D), F32)],
        compiler_params=_params(("arbitrary", "arbitrary")),
        name="gdn_mixer",
    )(seq_id, start, end, proj, proj, proj, proj, proj, conv_in, state_in, conv_w, dtb, alog, nw)


def _outproj_kernel(a1_ref, a2_ref, w1_ref, w2_ref, xa_ref, xb_ref, o_ref, *, na):
    acc = _dot(a1_ref[...], w1_ref[...]) + _dot(a2_ref[...], w2_ref[...])

    @pl.when(pl.program_id(0) < na)
    def _():
        o_ref[...] = xa_ref[...] + acc

    @pl.when(pl.program_id(0) >= na)
    def _():
        o_ref[...] = xb_ref[...] + acc


def _outproj(a1, a2, w1, w2, xa, xb, tm, tn):
    t, k1 = a1.shape
    k2 = a2.shape[1]
    d = w1.shape[1]
    na = xa.shape[0] // tm
    spec_a, spec_b = _two_source_specs((tm, tn), na, col=d // tn - 1)
    return pl.pallas_call(
        functools.partial(_outproj_kernel, na=na),
        grid=(t // tm, d // tn),
        in_specs=[pl.BlockSpec((tm, k1), lambda i, j: (i, 0)),
                  pl.BlockSpec((tm, k2), lambda i, j: (i, 0)),
                  pl.BlockSpec((k1, tn), lambda i, j: (0, j)),
                  pl.BlockSpec((k2, tn), lambda i, j: (0, j)),
                  spec_a, spec_b],
        out_specs=pl.BlockSpec((tm, tn), lambda i, j: (i, j)),
        out_shape=jax.ShapeDtypeStruct((t, d), F32),
        compiler_params=_params(("arbitrary", "arbitrary")),
        name="outproj",
    )(a1, a2, w1, w2, xa, xb)


def _router_kernel(x_ref, nw_ref, wr_ref, br_ref, idx_ref, gate_ref):
    x = x_ref[...]
    h = (x * lax.rsqrt(jnp.mean(x * x, axis=-1, keepdims=True) + EPS)) * nw_ref[...]
    logits = _dot(h, wr_ref[...], precision=HIGHEST) + br_ref[...]
    lane = lax.broadcasted_iota(I32, logits.shape, 1)
    vals = logits
    idx_out = jnp.zeros(logits.shape, I32)
    top = []
    for kk in range(TOP_K):
        m = jnp.max(vals, axis=-1, keepdims=True)
        sel = jnp.min(jnp.where(vals == m, lane, LANE), axis=-1, keepdims=True)
        idx_out = jnp.where(lane == kk, sel, idx_out)
        top.append(m)
        vals = jnp.where(lane == sel, -jnp.inf, vals)
    es = [jnp.exp(m - top[0]) for m in top]
    den = es[0]
    for e in es[1:]:
        den = den + e
    gate_out = jnp.zeros(logits.shape, F32)
    for kk in range(TOP_K):
        gate_out = jnp.where(lane == kk, es[kk] / den, gate_out)
    idx_ref[...] = idx_out
    gate_ref[...] = gate_out


def _router(x1, nw, wr, br, tm):
    t, d = x1.shape
    return pl.pallas_call(
        _router_kernel,
        grid=(t // tm,),
        in_specs=[pl.BlockSpec((tm, d), lambda i: (i, 0)),
                  pl.BlockSpec((1, d), lambda i: (0, 0)),
                  pl.BlockSpec((d, LANE), lambda i: (0, 0)),
                  pl.BlockSpec((1, LANE), lambda i: (0, 0))],
        out_specs=[pl.BlockSpec((tm, LANE), lambda i: (i, 0)), pl.BlockSpec((tm, LANE), lambda i: (i, 0))],
        out_shape=[jax.ShapeDtypeStruct((t, LANE), I32), jax.ShapeDtypeStruct((t, LANE), F32)],
        compiler_params=_params(("parallel",)),
        name="router",
    )(x1, nw, wr, br)


def _row_copy(src_hbm, dst_vmem, sem, src_row, dst_row):
    return pltpu.make_async_copy(src_hbm.at[pl.ds(src_row, 1), :], dst_vmem.at[pl.ds(dst_row, 1), :], sem)


def _gather_rows(idx_ref, idx0, src_hbm, dst_vmem, sem):
    def body(r, carry):
        _row_copy(src_hbm, dst_vmem, sem, idx_ref[idx0 + r], r).start()
        return carry
    lax.fori_loop(0, dst_vmem.shape[0], body, 0, unroll=GATHER_UNROLL)


def _drain_rows(src_hbm, dst_vmem, sem):
    pltpu.make_async_copy(src_hbm.at[pl.ds(0, dst_vmem.shape[0]), :], dst_vmem, sem).wait()


def _dispatch_kernel(tok_ref, nused_ref, x_hbm, nw_ref, o_ref, buf, sem):
    b = pl.program_id(0)
    bm = o_ref.shape[0]
    n_used = nused_ref[0]
    slot = lax.rem(b, 2)

    @pl.when(jnp.logical_and(b == 0, n_used > 0))
    def _():
        _gather_rows(tok_ref, 0, x_hbm, buf.at[0], sem.at[0])

    @pl.when(b + 1 < n_used)
    def _():
        _gather_rows(tok_ref, (b + 1) * bm, x_hbm, buf.at[1 - slot], sem.at[1 - slot])

    @pl.when(b < n_used)
    def _():
        _drain_rows(x_hbm, buf.at[slot], sem.at[slot])
        x = buf[slot]
        h = (x * lax.rsqrt(jnp.mean(x * x, axis=-1, keepdims=True) + EPS)) * nw_ref[...]
        o_ref[...] = h.astype(o_ref.dtype)

    @pl.when(b >= n_used)
    def _():
        o_ref[...] = jnp.zeros(o_ref.shape, o_ref.dtype)


def _dispatch(slot_tok, n_used, x1, nw, nblk):
    t, d = x1.shape
    grid_spec = pltpu.PrefetchScalarGridSpec(
        num_scalar_prefetch=2,
        grid=(nblk,),
        in_specs=[pl.BlockSpec(memory_space=pl.ANY),
                  pl.BlockSpec((1, d), lambda b, *_: (0, 0))],
        out_specs=pl.BlockSpec((MOE_BM, d), lambda b, *_: (b, 0)),
        scratch_shapes=[pltpu.VMEM((2, MOE_BM, d), F32), pltpu.SemaphoreType.DMA((2,))],
    )
    return pl.pallas_call(
        _dispatch_kernel,
        grid_spec=grid_spec,
        out_shape=jax.ShapeDtypeStruct((nblk * MOE_BM, d), BF16),
        compiler_params=_params(("arbitrary",)),
        name="moe_dispatch",
    )(slot_tok, n_used, x1, nw)


def _resident_rows_matmul(blk0_ref, nblk_ref, nused_ref, src_hbm, dst_hbm,
                          xbuf, obuf, zbuf, sem_in, sem_out, sem_z, compute):
    s = pl.program_id(0)
    n = pl.program_id(1)
    nt = pl.num_programs(1)
    bm = MOE_BM
    tn = obuf.shape[2]
    n_blocks = nblk_ref[s]
    first = blk0_ref[s]
    lin = s * nt + n
    slot = lax.rem(lin, 2)
    col = pl.multiple_of(n * tn, tn)

    def hbm_rows(blk):
        return pl.ds(pl.multiple_of(blk * bm, bm), bm)

    def x_copy(i):
        return pltpu.make_async_copy(src_hbm.at[hbm_rows(first + i), :], xbuf.at[pl.ds(i * bm, bm), :], sem_in)

    def o_copy(slot_, blk, i):
        return pltpu.make_async_copy(obuf.at[slot_, pl.ds(i * bm, bm), :],
                                     dst_hbm.at[hbm_rows(blk), pl.ds(col, tn)], sem_out.at[slot_])

    def for_blocks(count, fn):
        for i in range(MOE_SB_BLOCKS):
            pl.when(i < count)(functools.partial(fn, i))

    def wait_out(step, slot_):
        for_blocks(nblk_ref[lax.div(step, nt)], lambda i: o_copy(slot_, 0, i).wait())

    @pl.when(lin == 0)
    def _():
        xbuf[...] = jnp.zeros(xbuf.shape, xbuf.dtype)
        zbuf[...] = jnp.zeros(zbuf.shape, zbuf.dtype)

    @pl.when(n == 0)
    def _():
        for_blocks(n_blocks, lambda i: x_copy(i).start())
        for_blocks(n_blocks, lambda i: x_copy(i).wait())

    @pl.when(lin >= 2)
    def _():
        wait_out(lin - 2, slot)

    lo = 0
    for m in MOE_M_LADDER:
        @pl.when(jnp.logical_and(n_blocks > lo, n_blocks <= m))
        def _(m=m):
            obuf[slot, 0:m * bm, :] = compute(xbuf[0:m * bm, :])
        lo = m
    for_blocks(n_blocks, lambda i: o_copy(slot, first + i, i).start())

    @pl.when(lin == pl.num_programs(0) * nt - 1)
    def _():
        for_blocks(n_blocks, lambda i: o_copy(slot, 0, i).wait())

        @pl.when(lin >= 1)
        def _():
            wait_out(lin - 1, 1 - slot)

    @pl.when(s == pl.num_programs(0) - 1)
    def _():
        n_used = nused_ref[0]
        n_spare = dst_hbm.shape[0] // bm - n_used

        def z_copy(i):
            return pltpu.make_async_copy(zbuf, dst_hbm.at[hbm_rows(n_used + i), pl.ds(col, tn)], sem_z)
        lax.fori_loop(0, n_spare, lambda i, c: (z_copy(i).start(), c)[1], 0)
        lax.fori_loop(0, n_spare, lambda i, c: (z_copy(i).wait(), c)[1], 0)


def _resident_call(body, sb_e, sb_blk0, sb_nblk, n_used, src, weights, biases, dff_out, tn, out_dtype, name):
    p, k = src.shape
    nsb = sb_e.shape[0]
    nt = dff_out // tn

    def w_map(s, n, e_ref, b0_ref, nb_ref, nu_ref):
        return (e_ref[s], 0, jnp.where(nb_ref[s] > 0, n, nt - 1))

    grid_spec = pltpu.PrefetchScalarGridSpec(
        num_scalar_prefetch=4,
        grid=(nsb, nt),
        in_specs=([pl.BlockSpec(memory_space=pl.ANY)]
                  + [pl.BlockSpec((1, k, tn), w_map) for _ in weights]
                  + [pl.BlockSpec((1, 1, tn), w_map) for _ in biases]),
        out_specs=pl.BlockSpec(memory_space=pl.ANY),
        scratch_shapes=[pltpu.VMEM((MOE_SB_BLOCKS * MOE_BM, k), BF16),
                        pltpu.VMEM((2, MOE_SB_BLOCKS * MOE_BM, tn), out_dtype),
                        pltpu.VMEM((MOE_BM, tn), out_dtype),
                        pltpu.SemaphoreType.DMA(()), pltpu.SemaphoreType.DMA((2,)), pltpu.SemaphoreType.DMA(())],
    )
    return pl.pallas_call(
        body,
        grid_spec=grid_spec,
        out_shape=jax.ShapeDtypeStruct((p, dff_out), out_dtype),
        compiler_params=_params(("arbitrary", "arbitrary")),
        name=name,
    )(sb_e, sb_blk0, sb_nblk, n_used, src, *weights, *biases)


def _gateup_kernel(e_ref, blk0_ref, nblk_ref, nused_ref, x_hbm, wg_ref, wu_ref, bg_ref, bu_ref, act_hbm,
                   xbuf, obuf, zbuf, sem_in, sem_out, sem_z):
    def compute(x):
        gate = jnp.minimum(_dot(x, wg_ref[0].astype(BF16)) + bg_ref[0], SWIGLU_LIMIT)
        up = jnp.clip(_dot(x, wu_ref[0].astype(BF16)) + bu_ref[0], -SWIGLU_LIMIT, SWIGLU_LIMIT)
        act = gate * jax.nn.sigmoid(SWIGLU_ALPHA * gate) * (up + 1.0)
        return act.astype(obuf.dtype)

    _resident_rows_matmul(blk0_ref, nblk_ref, nused_ref, x_hbm, act_hbm,
                          xbuf, obuf, zbuf, sem_in, sem_out, sem_z, compute)


def _down_kernel(e_ref, blk0_ref, nblk_ref, nused_ref, a_hbm, wd_ref, bd_ref, y_hbm,
                 xbuf, obuf, zbuf, sem_in, sem_out, sem_z):
    def compute(a):
        return _dot(a, wd_ref[0].astype(BF16)) + bd_ref[0]

    _resident_rows_matmul(blk0_ref, nblk_ref, nused_ref, a_hbm, y_hbm,
                          xbuf, obuf, zbuf, sem_in, sem_out, sem_z, compute)


def _combine_kernel(pos_ref, ys_hbm, x_ref, g_ref, nw_ref, o_ref, buf, sem, *, tile0):
    i = pl.program_id(0)
    tt = o_ref.shape[0]
    n = TOP_K * tt
    slot = lax.rem(i, 2)

    def issue(tile, slot_):
        _gather_rows(pos_ref, (tile0 + tile) * n, ys_hbm, buf.at[slot_], sem.at[slot_])

    @pl.when(i == 0)
    def _():
        issue(0, 0)

    @pl.when(i + 1 < pl.num_programs(0))
    def _():
        issue(i + 1, 1 - slot)

    _drain_rows(ys_hbm, buf.at[slot], sem.at[slot])
    g = g_ref[...]
    acc = x_ref[...]
    for kk in range(TOP_K):
        acc = acc + buf[slot, kk * tt:(kk + 1) * tt, :] * g[:, kk:kk + 1]
    y = acc * lax.rsqrt(jnp.mean(acc * acc, axis=-1, keepdims=True) + EPS)
    o_ref[...] = y * nw_ref[...]


def _combine(pos_tiles, ys, x1, gates, nw, tt, tile0, n_tiles):
    d = x1.shape[1]
    grid_spec = pltpu.PrefetchScalarGridSpec(
        num_scalar_prefetch=1,
        grid=(n_tiles,),
        in_specs=[pl.BlockSpec(memory_space=pl.ANY),
                  pl.BlockSpec((tt, d), lambda i, pos: (tile0 + i, 0)),
                  pl.BlockSpec((tt, LANE), lambda i, pos: (tile0 + i, 0)),
                  pl.BlockSpec((1, d), lambda i, pos: (0, 0))],
        out_specs=pl.BlockSpec((tt, d), lambda i, pos: (i, 0)),
        scratch_shapes=[pltpu.VMEM((2, TOP_K * tt, d), F32), pltpu.SemaphoreType.DMA((2,))],
    )
    return pl.pallas_call(
        functools.partial(_combine_kernel, tile0=tile0),
        grid_spec=grid_spec,
        out_shape=jax.ShapeDtypeStruct((n_tiles * tt, d), F32),
        compiler_params=_params(("arbitrary",)),
        name="moe_combine",
    )(pos_tiles, ys, x1, gates, nw)


def _pad_lanes(v, offset):
    out = jnp.zeros((LANE,), F32)
    return lax.dynamic_update_slice(out, v.astype(F32), (offset,)).reshape(1, LANE)


def _ssd_group_layout(a, width):
    lead = a.shape[:-2]
    rows = a.shape[-2]
    rb = width // LANE // SSD_G
    xs = a[..., :width].reshape(*lead, rows, SSD_G, rb, LANE)
    bs = a[..., width:width + SSD_G * SSD_N].reshape(*lead, rows, SSD_G, 1, LANE)
    cs = a[..., width + SSD_G * SSD_N:].reshape(*lead, rows, SSD_G, 1, LANE)
    cat = jnp.concatenate([xs, bs, cs], axis=-2)
    n = cat.ndim
    return jnp.moveaxis(cat, n - 4, n - 2)


def _gdn_block_layout(a, heads, hb):
    lead = a.shape[:-2]
    rows = a.shape[-2]
    r = a.reshape(*lead, rows, 3, heads // hb, hb, LANE)
    n = r.ndim
    r = jnp.moveaxis(r, n - 5, n - 2)
    r = jnp.moveaxis(r, n - 5, n - 4)
    return r.reshape(*lead, heads // hb, 3 * hb, rows, LANE)


def _pick_tn_cb(ncb):
    return INPROJ_TN_CB, (-ncb) % INPROJ_TN_CB


def kernel(x_prompt, x_sample, state_ssd_conv, state_ssd, state_gdn_conv, state_gdn, norm_mix, w_in, ssd_conv_w,
           ssd_conv_b, ssd_dt_bias, ssd_A_log, ssd_D, ssd_norm, gdn_conv_w, gdn_dt_bias, gdn_A_log, gdn_norm, w_out,
           norm_ffn, w_router, b_router, w_gate, b_gate, w_up, b_up, w_down, b_down, norm_final):
    assert w_in.shape[0] == 1, "single layer"
    nb_p, seq_p, d = x_prompt.shape
    nb_s, seq_s, _ = x_sample.shape
    assert seq_p % CHUNK == 0 and seq_s % CHUNK == 0
    ssd_heads = d // SSD_P
    ssd_w = ssd_heads * SSD_P
    ssd_cs = ssd_w + 2 * SSD_G * SSD_N
    gdn_heads = d // GDN_D
    gdn_w = gdn_heads * GDN_D
    t_p = nb_p * seq_p
    t = t_p + nb_s * seq_s
    nseq = nb_p + nb_s

    x_p = x_prompt.reshape(t_p, d)
    x_s = x_sample.reshape(nb_s * seq_s, d)
    seq_len = [seq_p] * nb_p + [seq_s] * nb_s
    seq_id, start, end = [], [], []
    for s, n in enumerate(seq_len):
        for cidx in range(n // CHUNK):
            seq_id.append(s)
            start.append(int(cidx == 0))
            end.append(int(cidx == n // CHUNK - 1))
    seq_id = jnp.asarray(np.array(seq_id, np.int32))
    start = jnp.asarray(np.array(start, np.int32))
    end = jnp.asarray(np.array(end, np.int32))

    o_z, o_xbc, o_dt = 0, ssd_w, ssd_w + ssd_cs
    o_qkv = o_dt + ssd_heads
    o_zg = o_qkv + 3 * gdn_w
    o_b = o_zg + gdn_w
    o_a = o_b + gdn_heads
    w0 = w_in[0]
    n_small = ssd_heads + 2 * gdn_heads
    assert n_small <= LANE
    cb_q = 0
    cb_zg = cb_q + 3 * gdn_w // LANE
    cb_z = cb_zg + gdn_w // LANE
    cb_x = cb_z + ssd_w // LANE
    cb_sm = cb_x + ssd_cs // LANE
    ncb = cb_sm + 1
    tn_cb, pad_cb = _pick_tn_cb(ncb)
    w_perm = jnp.concatenate([
        w0[:, o_qkv:o_qkv + 3 * gdn_w], w0[:, o_zg:o_zg + gdn_w], w0[:, o_z:o_z + ssd_w],
        w0[:, o_xbc:o_xbc + ssd_cs], w0[:, o_dt:o_dt + ssd_heads], w0[:, o_b:o_b + 2 * gdn_heads],
        jnp.zeros((d, LANE - n_small + pad_cb * LANE), F32)], axis=1).astype(BF16)
    lane_beta = ssd_heads
    lane_a = ssd_heads + gdn_heads

    tm_big = _row_tile(t, 1536)
    tm_mid = _row_tile(t, 528)
    tm_src = _row_tile(int(np.gcd(t_p, t - t_p)), 512)
    h = _rmsnorm_cast(x_p, x_s, norm_mix[0], tm_src)
    proj = _inproj(h, w_perm, tm_big, tn_cb)

    def with_zero_prompt(a):
        return jnp.concatenate([jnp.zeros((nb_p,) + a.shape[1:], a.dtype), a], axis=0)

    ssd_conv0 = _ssd_group_layout(with_zero_prompt(state_ssd_conv[0]), ssd_w)
    gdn_hb = min(GDN_HB, gdn_heads)
    gdn_conv0 = _gdn_block_layout(with_zero_prompt(state_gdn_conv[0]), gdn_heads, gdn_hb)
    s0 = with_zero_prompt(state_ssd[0])
    ssd_s0 = s0.reshape(nseq, ssd_heads // 2, 2, SSD_P, SSD_N).transpose(0, 1, 4, 2, 3).reshape(
        nseq, ssd_heads // 2, SSD_N, LANE)
    gdn_s0 = with_zero_prompt(state_gdn[0])

    ssd_cw = _ssd_group_layout(ssd_conv_w[0][None], ssd_w)[0]
    ssd_cw = jnp.swapaxes(ssd_cw, 1, 2)[:, :, :, None, :]
    ssd_cb = _ssd_group_layout(ssd_conv_b[0][None, None], ssd_w)[0]
    d_exp = jnp.repeat(ssd_D[0], SSD_P).reshape(ssd_w // LANE, 1, LANE)
    ssd_nw = ssd_norm[0].reshape(ssd_w // LANE, 1, LANE)
    y_ssd, ssd_s = _ssd_mixer(
        proj, seq_id, start, end, ssd_conv0, ssd_s0, ssd_cw, ssd_cb,
        _pad_lanes(ssd_dt_bias[0], 0), _pad_lanes(ssd_A_log[0], 0), d_exp, ssd_nw,
        t, ssd_w, cb_z, cb_x, cb_x + ssd_w // LANE, cb_x + ssd_w // LANE + SSD_G, cb_sm)

    gdn_cw = _gdn_block_layout(gdn_conv_w[0][None], gdn_heads, gdn_hb)[0]
    gdn_cw = jnp.swapaxes(gdn_cw, 1, 2)[:, :, :, None, :]
    y_gdn, gdn_s = _gdn_mixer(
        proj, seq_id, start, end, gdn_conv0, gdn_s0, gdn_cw,
        _pad_lanes(gdn_dt_bias[0], lane_a), _pad_lanes(gdn_A_log[0], lane_a), gdn_norm[0].reshape(1, LANE),
        t, gdn_heads, cb_q, cb_zg, cb_sm, lane_beta, lane_a)

    w_o = w_out[0].astype(BF16)
    x1 = _outproj(y_ssd, y_gdn, w_o[:ssd_w], w_o[ssd_w:], x_p, x_s, tm_src, min(512, d))

    wr = jnp.concatenate([w_router[0], jnp.zeros((d, LANE - N_EXPERTS), F32)], axis=1)
    br = jnp.concatenate([b_router[0], jnp.full((LANE - N_EXPERTS,), NEG_BIG, F32)]).reshape(1, LANE)
    nffn = norm_ffn[0].reshape(1, d)
    idx_pad, gate_pad = _router(x1, nffn, wr, br, tm_mid)
    top_idx = idx_pad[:, :TOP_K]
    tk = t * TOP_K
    flat_e = top_idx.reshape(tk)
    order = jnp.argsort(flat_e).astype(I32)
    sorted_e = flat_e[order]
    counts = jnp.sum((flat_e[:, None] == jnp.arange(N_EXPERTS, dtype=I32)[None, :]).astype(I32), axis=0)
    starts = jnp.cumsum(counts) - counts
    pcounts = (counts + MOE_BM - 1) // MOE_BM * MOE_BM
    pends = jnp.cumsum(pcounts)
    pstarts = pends - pcounts
    dest = (pstarts[sorted_e] + (jnp.arange(tk, dtype=I32) - starts[sorted_e])).astype(I32)
    nblk = -(-tk // MOE_BM) + N_EXPERTS
    pos = jnp.zeros((tk,), I32).at[order].set(dest)
    blk_e = jnp.minimum(jnp.searchsorted(pends, jnp.arange(nblk, dtype=I32) * MOE_BM, side='right'),
                        N_EXPERTS - 1).astype(I32)
    n_used = (pends[-1] // MOE_BM).astype(I32).reshape(1)
    blk0 = (pstarts // MOE_BM).astype(I32)
    nblk_e = (pcounts // MOE_BM).astype(I32)
    slot = jnp.arange(nblk * MOE_BM, dtype=I32)
    slot_e = jnp.repeat(blk_e, MOE_BM)
    slot_off = slot - pstarts[slot_e].astype(I32)
    slot_src = jnp.clip(starts[slot_e].astype(I32) + slot_off, 0, tk - 1)
    slot_tok = jnp.where(slot_off < counts[slot_e], (order // TOP_K)[slot_src], 0).astype(I32)
    nsb_e = (nblk_e + MOE_SB_BLOCKS - 1) // MOE_SB_BLOCKS
    sb_ends = jnp.cumsum(nsb_e)
    sb_starts = sb_ends - nsb_e
    n_sb = sb_ends[-1]
    sb_i = jnp.arange(N_EXPERTS + nblk // MOE_SB_BLOCKS, dtype=I32)
    sb_c = jnp.minimum(sb_i, n_sb - 1)
    sb_e = jnp.minimum(jnp.searchsorted(sb_ends, sb_c, side='right'), N_EXPERTS - 1).astype(I32)
    sb_j = sb_c - sb_starts[sb_e]
    sb_blk0 = (blk0[sb_e] + sb_j * MOE_SB_BLOCKS).astype(I32)
    sb_nblk = jnp.where(sb_i < n_sb, jnp.clip(nblk_e[sb_e] - sb_j * MOE_SB_BLOCKS, 0, MOE_SB_BLOCKS), 0).astype(I32)

    xs = _dispatch(slot_tok, n_used, x1, nffn, nblk)
    tn_ff = min(MOE_TN, d)
    act = _resident_call(_gateup_kernel, sb_e, sb_blk0, sb_nblk, n_used, xs, (w_gate[0], w_up[0]),
                         (b_gate[0][:, None, :], b_up[0][:, None, :]), w_gate.shape[3], tn_ff, BF16, "moe_gateup")
    ys = _resident_call(_down_kernel, sb_e, sb_blk0, sb_nblk, n_used, act, (w_down[0],),
                        (b_down[0][:, None, :],), d, tn_ff, F32, "moe_down")

    tt = CHUNK
    pos_tiles = pos.reshape(t // tt, tt, TOP_K).transpose(0, 2, 1).reshape(tk)
    nfin = norm_final.reshape(1, d)
    y_prompt = _combine(pos_tiles, ys, x1, gate_pad, nfin, tt, 0, t_p // tt).reshape(nb_p, seq_p, d)
    y_sample = _combine(pos_tiles, ys, x1, gate_pad, nfin, tt, t_p // tt, (t - t_p) // tt).reshape(nb_s, seq_s, d)

    def last_rows(cb0, ncols):
        nblk_c = ncols // LANE
        ends = np.cumsum(seq_len)
        rows = jnp.concatenate(
            [lax.slice(proj, (cb0, int(e) - (CONV_K - 1), 0), (cb0 + nblk_c, int(e), LANE)) for e in ends], axis=1)
        a = rows.reshape(nblk_c, nseq, CONV_K - 1, LANE).transpose(1, 2, 0, 3).reshape(nseq, CONV_K - 1, ncols)
        return a[:nb_p][None], a[nb_p:][None]

    ssd_conv_p, ssd_conv_s = last_rows(cb_x, ssd_cs)
    gdn_conv_p, gdn_conv_s = last_rows(cb_q, 3 * gdn_w)
    ssd_state = ssd_s.reshape(nseq, ssd_heads // 2, SSD_N, 2, SSD_P).transpose(0, 1, 3, 4, 2).reshape(
        nseq, ssd_heads, SSD_P, SSD_N)
    return (y_prompt, y_sample,
            ssd_conv_p, ssd_state[:nb_p][None], gdn_conv_p, gdn_s[:nb_p][None],
            ssd_conv_s, ssd_state[nb_p:][None], gdn_conv_s, gdn_s[nb_p:][None])
```

```python
import functools

import numpy as np
import jax
import jax.numpy as jnp
from jax import lax
from jax.experimental import pallas as pl
from jax.experimental.pallas import tpu as pltpu

F32 = jnp.float32
BF16 = jnp.bfloat16
I32 = jnp.int32

LANE = 128
SUBLANE = 8
VMEM_LIMIT = 60 * 1024 * 1024

CHUNK = 64
CONV_K = 4
SSD_P = 64
SSD_N = 128
SSD_G = 8
GDN_D = 128
N_EXPERTS = 32
TOP_K = 4
SWIGLU_LIMIT = 7.0
SWIGLU_ALPHA = 1.702
EPS = 1e-6
NEG_BIG = -1e30

MOE_BM = 256
MOE_SB_BLOCKS = 9
MOE_PARTIAL_HEIGHTS = (4, 2, 1)
MOE_TN = 256
GATHER_UNROLL = 8
INPROJ_TN_CB = 10
GDN_HB = 32
SSD_GS = 8
HIGHEST = lax.Precision.HIGHEST


def _row_tile(n, target, mult=16):
    best = None
    for t in range(mult, min(n, target) + 1, mult):
        if n % t == 0:
            best = t
    assert best is not None, (n, target)
    return best


def _params(sem):
    return pltpu.CompilerParams(dimension_semantics=sem, vmem_limit_bytes=VMEM_LIMIT)


def _silu(x):
    return x * jax.nn.sigmoid(x)


def _softplus(x):
    return jnp.maximum(x, 0.0) + jnp.log1p(jnp.exp(-jnp.abs(x)))


def _dot(a, b, **kw):
    return jnp.dot(a, b, preferred_element_type=F32, **kw)


def _split3(a):
    hi = a.astype(BF16)
    r = a - hi.astype(F32)
    mid = r.astype(BF16)
    lo = (r - mid.astype(F32)).astype(BF16)
    return hi, mid, lo


def _two_source_specs(block, na, col=None):
    if col is None:
        ia = lambda i, *_: (jnp.minimum(i, na - 1), 0)
        ib = lambda i, *_: (jnp.maximum(i - na, 0), 0)
    else:
        ia = lambda i, j, *_: (jnp.minimum(i, na - 1), jnp.where(i < na, j, col))
        ib = lambda i, j, *_: (jnp.maximum(i - na, 0), jnp.where(i >= na, j, 0))
    return pl.BlockSpec(block, ia), pl.BlockSpec(block, ib)


def _rmsnorm_cast_kernel(xa_ref, xb_ref, w_ref, o_ref, *, na):
    def body(x_ref):
        x = x_ref[...]
        xn = x * lax.rsqrt(jnp.mean(x * x, axis=-1, keepdims=True) + EPS)
        o_ref[...] = (xn * w_ref[...]).astype(o_ref.dtype)

    pl.when(pl.program_id(0) < na)(lambda: body(xa_ref))
    pl.when(pl.program_id(0) >= na)(lambda: body(xb_ref))


def _rmsnorm_cast(xa, xb, w, tm):
    ta, d = xa.shape
    tb = xb.shape[0]
    na = ta // tm
    spec_a, spec_b = _two_source_specs((tm, d), na)
    return pl.pallas_call(
        functools.partial(_rmsnorm_cast_kernel, na=na),
        grid=(na + tb // tm,),
        in_specs=[spec_a, spec_b, pl.BlockSpec((1, d), lambda i: (0, 0))],
        out_specs=pl.BlockSpec((tm, d), lambda i: (i, 0)),
        out_shape=jax.ShapeDtypeStruct((ta + tb, d), BF16),
        compiler_params=_params(("arbitrary",)),
        name="rmsnorm_cast",
    )(xa, xb, w.reshape(1, d))


def _inproj_kernel(a_ref, w_ref, o_ref):
    ncb = o_ref.shape[0]
    a = a_ref[...]
    step = 2 if ncb % 2 == 0 else 1
    for j in range(0, ncb, step):
        acc = _dot(a, w_ref[:, j * LANE:(j + step) * LANE])
        for s in range(step):
            o_ref[j + s] = acc[:, s * LANE:(s + 1) * LANE]


def _inproj(a, w, tm, tn_cb):
    t, d = a.shape
    n = w.shape[1]
    ncb = n // LANE
    return pl.pallas_call(
        _inproj_kernel,
        grid=(t // tm, ncb // tn_cb),
        in_specs=[pl.BlockSpec((tm, d), lambda i, j: (i, 0)),
                  pl.BlockSpec((d, tn_cb * LANE), lambda i, j: (0, j))],
        out_specs=pl.BlockSpec((tn_cb, tm, LANE), lambda i, j: (j, i, 0)),
        out_shape=jax.ShapeDtypeStruct((ncb, t, LANE), F32),
        compiler_params=_params(("parallel", "arbitrary")),
        name="inproj",
    )(a, w)


def _causal_conv(u, ext_scr, conv_in, taps, is_start):
    L = u.shape[1]
    base = SUBLANE - (CONV_K - 1)

    @pl.when(is_start)
    def _():
        ext_scr[:, base:SUBLANE, :] = conv_in()

    ext_scr[:, SUBLANE:SUBLANE + L, :] = u
    acc = ext_scr[:, base:base + L, :] * taps(0)
    for j in range(1, CONV_K):
        acc = acc + ext_scr[:, base + j:base + j + L, :] * taps(j)
    ext_scr[:, base:SUBLANE, :] = ext_scr[:, base + L:SUBLANE + L, :]
    return acc


def _cumsum_rows(a, incl):
    m = incl.astype(BF16)
    hi, mid, lo = _split3(a)
    return _dot(m, hi) + (_dot(m, mid) + _dot(m, lo))


def _select_dot(a, onehot):
    oh = onehot.astype(BF16)
    hi, mid, lo = _split3(a)
    return _dot(hi, oh) + (_dot(mid, oh) + _dot(lo, oh))


def _ssd_kernel(seq_ref, start_ref, end_ref,
                z_ref, x_ref, b_ref, c_ref, sm_ref, cin_ref, sin_ref, cw_ref, cb_ref,
                dtb_ref, alog_ref, d_ref, nw_ref,
                y_ref, sout_ref,
                ext_scr, s_scr):
    gi0 = pl.program_id(0)
    c = pl.program_id(1)
    gs = b_ref.shape[0]
    rb = x_ref.shape[0] // gs
    nb = rb + 2
    L = x_ref.shape[1]
    is_start = start_ref[c] == 1

    @pl.when(is_start)
    def _():
        s_scr[...] = sin_ref[0]

    u = jnp.concatenate(
        [p for gi in range(gs) for p in (x_ref[gi * rb:(gi + 1) * rb], b_ref[gi:gi + 1], c_ref[gi:gi + 1])], axis=0)
    conv = _causal_conv(u, ext_scr, lambda: cin_ref[0].reshape(gs * nb, CONV_K - 1, LANE),
                        lambda j: cw_ref[:, j].reshape(gs * nb, 1, LANE), is_start)
    uc = _silu(conv + cb_ref[...].reshape(gs * nb, 1, LANE))

    sm = sm_ref[0]
    dt_all = _softplus(sm + dtb_ref[...])
    a_all = dt_all * (-jnp.exp(alog_ref[...]))
    ti = lax.broadcasted_iota(I32, (L, L), 0)
    si = lax.broadcasted_iota(I32, (L, L), 1)
    cum_all = _cumsum_rows(a_all, ti >= si)
    ej = lax.broadcasted_iota(I32, (LANE, gs * rb * LANE), 0)
    ec = lax.broadcasted_iota(I32, (LANE, gs * rb * LANE), 1)
    expand = ej == gi0 * (gs * 2 * rb) + jnp.right_shift(ec, 6)
    dtx = _select_dot(dt_all, expand)
    cumx = _select_dot(cum_all, expand)

    t2 = lax.broadcasted_iota(I32, (L, LANE), 0)
    l2 = lax.broadcasted_iota(I32, (L, LANE), 1)
    s2 = jnp.bitwise_and(l2, SSD_P - 1)
    diag2 = (t2 == s2).astype(F32)
    causal2 = t2 >= s2
    left = l2 < SSD_P

    for gi in range(gs):
        bm = uc[gi * nb + rb]
        cm = uc[gi * nb + rb + 1]
        b2 = jnp.concatenate([bm, bm], axis=0).astype(BF16)
        cm_b = cm.astype(BF16)
        bm_b = bm.astype(BF16)
        cb2 = lax.dot_general(cm_b, b2, (((1,), (1,)), ((), ())), preferred_element_type=F32)
        ygs = []
        ms = jnp.zeros((L, 1), F32)
        for j in range(rb):
            jj = gi * rb + j
            ccol = cumx[:, jj * LANE:(jj + 1) * LANE]
            dtc = dtx[:, jj * LANE:(jj + 1) * LANE]
            crow = jnp.sum(ccol * diag2, axis=0, keepdims=True)
            dec = jnp.exp(jnp.where(causal2, ccol - crow, NEG_BIG))
            ww = (cb2 * dec).astype(BF16)
            xb = uc[gi * nb + j]
            xdt = xb * dtc
            xbd = jnp.concatenate([jnp.where(left, xdt, 0.0), jnp.where(left, 0.0, xdt)], axis=0).astype(BF16)
            y = _dot(ww, xbd)
            sj = s_scr[jj]
            y = y + _dot(cm_b, sj.astype(BF16)) * jnp.exp(ccol)
            y = y + d_ref[jj] * xb
            cl = ccol[L - 1:L, :]
            xw = (xdt * jnp.exp(cl - ccol)).astype(BF16)
            s_scr[jj] = sj * jnp.exp(cl) + lax.dot_general(
                bm_b, xw, (((0,), (0,)), ((), ())), preferred_element_type=F32)
            yg = y * _silu(z_ref[jj])
            ms = ms + jnp.sum(yg * yg, axis=-1, keepdims=True)
            ygs.append(yg)
        inv = lax.rsqrt(ms / (rb * LANE) + EPS)
        for j in range(rb):
            jj = gi * rb + j
            y_ref[:, jj * LANE:(jj + 1) * LANE] = ((ygs[j] * inv) * nw_ref[jj]).astype(y_ref.dtype)

    @pl.when(end_ref[c] == 1)
    def _():
        sout_ref[0] = s_scr[...]


def _ssd_mixer(proj, seq_id, start, end, conv_in, state_in, conv_w, conv_b, dtb, alog, d_exp, nw,
               t, width, cb_z, cb_x, cb_b, cb_c, cb_sm):
    rb = width // LANE // SSD_G
    nb = rb + 2
    nc = t // CHUNK
    nseq = state_in.shape[0]
    L = CHUNK

    def im(f):
        return lambda g, c, s, st, en: f(g, c, s)

    gs = SSD_GS
    grb = gs * rb
    grid_spec = pltpu.PrefetchScalarGridSpec(
        num_scalar_prefetch=3,
        grid=(SSD_G // gs, nc),
        in_specs=[
            pl.BlockSpec((grb, L, LANE), im(lambda g, c, s: (cb_z // grb + g, c, 0))),
            pl.BlockSpec((grb, L, LANE), im(lambda g, c, s: (cb_x // grb + g, c, 0))),
            pl.BlockSpec((gs, L, LANE), im(lambda g, c, s: (cb_b // gs + g, c, 0))),
            pl.BlockSpec((gs, L, LANE), im(lambda g, c, s: (cb_c // gs + g, c, 0))),
            pl.BlockSpec((1, L, LANE), im(lambda g, c, s: (cb_sm, c, 0))),
            pl.BlockSpec((1, gs, nb, CONV_K - 1, LANE), im(lambda g, c, s: (s[c], g, 0, 0, 0))),
            pl.BlockSpec((1, grb, SSD_N, LANE), im(lambda g, c, s: (s[c], g, 0, 0))),
            pl.BlockSpec((gs, CONV_K, nb, 1, LANE), im(lambda g, c, s: (g, 0, 0, 0, 0))),
            pl.BlockSpec((gs, nb, 1, LANE), im(lambda g, c, s: (g, 0, 0, 0))),
            pl.BlockSpec((1, LANE), im(lambda g, c, s: (0, 0))),
            pl.BlockSpec((1, LANE), im(lambda g, c, s: (0, 0))),
            pl.BlockSpec((grb, 1, LANE), im(lambda g, c, s: (g, 0, 0))),
            pl.BlockSpec((grb, 1, LANE), im(lambda g, c, s: (g, 0, 0))),
        ],
        out_specs=[
            pl.BlockSpec((L, grb * LANE), im(lambda g, c, s: (c, g))),
            pl.BlockSpec((1, grb, SSD_N, LANE), im(lambda g, c, s: (s[c], g, 0, 0))),
        ],
        scratch_shapes=[pltpu.VMEM((gs * nb, L + SUBLANE, LANE), F32), pltpu.VMEM((grb, SSD_N, LANE), F32)],
    )
    assert cb_z % grb == 0 and cb_x % grb == 0 and cb_b % gs == 0 and cb_c % gs == 0 and SSD_G % gs == 0
    return pl.pallas_call(
        _ssd_kernel,
        grid_spec=grid_spec,
        out_shape=[jax.ShapeDtypeStruct((t, width), BF16),
                   jax.ShapeDtypeStruct((nseq, width // LANE, SSD_N, LANE), F32)],
        compiler_params=_params(("arbitrary", "arbitrary")),
        name="ssd_mixer",
    )(seq_id, start, end, proj, proj, proj, proj, proj, conv_in, state_in, conv_w, conv_b, dtb, alog, d_exp, nw)


def _bdot(a, b, ca, cb):
    return lax.dot_general(a, b, (((ca,), (cb,)), ((0,), (0,))), preferred_element_type=F32)


def _bdot3(a, b):
    ah = a.astype(BF16)
    al = (a - ah.astype(F32)).astype(BF16)
    bh = b.astype(BF16)
    bl = (b - bh.astype(F32)).astype(BF16)
    return _bdot(ah, bh, 2, 1) + (_bdot(ah, bl, 2, 1) + _bdot(al, bh, 2, 1))


def _gdn_kernel(seq_ref, start_ref, end_ref,
                q_ref, k_ref, v_ref, z_ref, sm_ref, cin_ref, sin_ref, cw_ref,
                dtb_ref, alog_ref, nw_ref,
                y_ref, sout_ref,
                ext_scr, s_scr, *, lane_beta, lane_a):
    hb_i = pl.program_id(0)
    c = pl.program_id(1)
    hb = q_ref.shape[0]
    L = q_ref.shape[1]
    is_start = start_ref[c] == 1

    @pl.when(is_start)
    def _():
        s_scr[...] = sin_ref[0]

    u = jnp.concatenate([q_ref[...], k_ref[...], v_ref[...]], axis=0)
    uc = _silu(_causal_conv(u, ext_scr, lambda: cin_ref[0, 0], lambda j: cw_ref[0, j], is_start))
    q = uc[:hb]
    k = uc[hb:2 * hb]
    v = uc[2 * hb:]
    q = q * (lax.rsqrt(jnp.sum(q * q, axis=-1, keepdims=True) + EPS) * (GDN_D ** -0.5))
    k = k * lax.rsqrt(jnp.sum(k * k, axis=-1, keepdims=True) + EPS)

    sm = sm_ref[0]
    beta_all = jax.nn.sigmoid(sm)
    g_all = -jnp.exp(alog_ref[...]) * _softplus(sm + dtb_ref[...])
    ti = lax.broadcasted_iota(I32, (L, L), 0)
    si = lax.broadcasted_iota(I32, (L, L), 1)
    incl = ti >= si
    strict = ti > si
    gam_all = _cumsum_rows(g_all, incl)
    ej = lax.broadcasted_iota(I32, (LANE, hb * LANE), 0)
    ec = jnp.right_shift(lax.broadcasted_iota(I32, (LANE, hb * LANE), 1), 7) + hb_i * hb
    betax = _select_dot(beta_all, ej == ec + lane_beta)
    gamx = _select_dot(gam_all, ej == ec + lane_a)
    beta_c = jnp.stack([betax[:, h * LANE:(h + 1) * LANE] for h in range(hb)])
    gam_c = jnp.stack([gamx[:, h * LANE:(h + 1) * LANE] for h in range(hb)])

    t2 = lax.broadcasted_iota(I32, (L, LANE), 0)
    l2 = lax.broadcasted_iota(I32, (L, LANE), 1)
    diag2 = (t2 == l2).astype(F32)
    gam_r = jnp.sum(gam_c * diag2, axis=1, keepdims=True)[:, :, :L]
    gam_t = gam_c[:, :, :L]
    gam_m = jnp.exp(jnp.where(incl, gam_t - gam_r, NEG_BIG))

    kb = k.astype(BF16)
    kk = _bdot(kb, kb, 2, 2)
    a_mat = jnp.where(strict, beta_c[:, :, :L] * kk * gam_m, 0.0)
    n_pow = -a_mat
    x_inv = jnp.where(ti == si, 1.0, 0.0) + n_pow
    span = 2
    while span < L:
        n_pow = _bdot3(n_pow, n_pow)
        x_inv = x_inv + _bdot3(x_inv, n_pow)
        span *= 2

    eg = jnp.exp(gam_c)
    rhs = jnp.concatenate([v * beta_c, k * (beta_c * eg)], axis=-1)
    sol = _bdot3(x_inv, rhs)
    u_ = sol[:, :, :GDN_D]
    w_ = sol[:, :, GDN_D:]
    s_prev = s_scr[...]
    s_b = s_prev.astype(BF16)
    v_new = u_ - _bdot(w_.astype(BF16), s_b, 2, 1)
    vn_b = v_new.astype(BF16)
    qk = _bdot(q.astype(BF16), kb, 2, 2) * gam_m
    o = _bdot((q * eg).astype(BF16), s_b, 2, 1) + _bdot(qk.astype(BF16), vn_b, 2, 1)
    gl = gam_c[:, L - 1:L, :]
    kt = (k * jnp.exp(gl - gam_c)).astype(BF16)
    for h in range(hb):
        upd = lax.dot_general(kt[h], vn_b[h], (((0,), (0,)), ((), ())), preferred_element_type=F32)
        s_scr[h] = s_prev[h] * jnp.exp(gl[h]) + upd

    o = o * lax.rsqrt(jnp.mean(o * o, axis=-1, keepdims=True) + EPS)
    o = (o * nw_ref[...]) * _silu(z_ref[...])
    for h in range(hb):
        y_ref[:, h * LANE:(h + 1) * LANE] = o[h].astype(y_ref.dtype)

    @pl.when(end_ref[c] == 1)
    def _():
        sout_ref[0] = s_scr[...]


def _gdn_mixer(proj, seq_id, start, end, conv_in, state_in, conv_w, dtb, alog, nw,
               t, heads, cb_q, cb_z, cb_sm, lane_beta, lane_a):
    hb = min(GDN_HB, heads)
    assert heads % hb == 0 and cb_q % hb == 0 and cb_z % hb == 0
    nhb = heads // hb
    nc = t // CHUNK
    nseq = state_in.shape[0]
    L = CHUNK

    def im(f):
        return lambda h, c, s, st, en: f(h, c, s)

    grid_spec = pltpu.PrefetchScalarGridSpec(
        num_scalar_prefetch=3,
        grid=(nhb, nc),
        in_specs=[
            pl.BlockSpec((hb, L, LANE), im(lambda h, c, s: (cb_q // hb + h, c, 0))),
            pl.BlockSpec((hb, L, LANE), im(lambda h, c, s: ((cb_q + heads) // hb + h, c, 0))),
            pl.BlockSpec((hb, L, LANE), im(lambda h, c, s: ((cb_q + 2 * heads) // hb + h, c, 0))),
            pl.BlockSpec((hb, L, LANE), im(lambda h, c, s: (cb_z // hb + h, c, 0))),
            pl.BlockSpec((1, L, LANE), im(lambda h, c, s: (cb_sm, c, 0))),
            pl.BlockSpec((1, 1, 3 * hb, CONV_K - 1, LANE), im(lambda h, c, s: (s[c], h, 0, 0, 0))),
            pl.BlockSpec((1, hb, GDN_D, GDN_D), im(lambda h, c, s: (s[c], h, 0, 0))),
            pl.BlockSpec((1, CONV_K, 3 * hb, 1, LANE), im(lambda h, c, s: (h, 0, 0, 0, 0))),
            pl.BlockSpec((1, LANE), im(lambda h, c, s: (0, 0))),
            pl.BlockSpec((1, LANE), im(lambda h, c, s: (0, 0))),
            pl.BlockSpec((1, LANE), im(lambda h, c, s: (0, 0))),
        ],
        out_specs=[
            pl.BlockSpec((L, hb * LANE), im(lambda h, c, s: (c, h))),
            pl.BlockSpec((1, hb, GDN_D, GDN_D), im(lambda h, c, s: (s[c], h, 0, 0))),
        ],
        scratch_shapes=[pltpu.VMEM((3 * hb, L + SUBLANE, LANE), F32), pltpu.VMEM((hb, GDN_D, GDN_D), F32)],
    )
    return pl.pallas_call(
        functools.partial(_gdn_kernel, lane_beta=lane_beta, lane_a=lane_a),
        grid_spec=grid_spec,
        out_shape=[jax.ShapeDtypeStruct((t, heads * GDN_D), BF16),
                   jax.ShapeDtypeStruct((nseq, heads, GDN_D, GDN_D), F32)],
        compiler_params=_params(("arbitrary", "arbitrary")),
        name="gdn_mixer",
    )(seq_id, start, end, proj, proj, proj, proj, proj, conv_in, state_in, conv_w, dtb, alog, nw)


def _outproj_kernel(a1_ref, a2_ref, w1_ref, w2_ref, xa_ref, xb_ref, o_ref, *, na):
    acc = _dot(a1_ref[...], w1_ref[...]) + _dot(a2_ref[...], w2_ref[...])

    @pl.when(pl.program_id(0) < na)
    def _():
        o_ref[...] = xa_ref[...] + acc

    @pl.when(pl.program_id(0) >= na)
    def _():
        o_ref[...] = xb_ref[...] + acc


def _outproj(a1, a2, w1, w2, xa, xb, tm, tn):
    t, k1 = a1.shape
    k2 = a2.shape[1]
    d = w1.shape[1]
    na = xa.shape[0] // tm
    spec_a, spec_b = _two_source_specs((tm, tn), na, col=d // tn - 1)
    return pl.pallas_call(
        functools.partial(_outproj_kernel, na=na),
        grid=(t // tm, d // tn),
        in_specs=[pl.BlockSpec((tm, k1), lambda i, j: (i, 0)),
                  pl.BlockSpec((tm, k2), lambda i, j: (i, 0)),
                  pl.BlockSpec((k1, tn), lambda i, j: (0, j)),
                  pl.BlockSpec((k2, tn), lambda i, j: (0, j)),
                  spec_a, spec_b],
        out_specs=pl.BlockSpec((tm, tn), lambda i, j: (i, j)),
        out_shape=jax.ShapeDtypeStruct((t, d), F32),
        compiler_params=_params(("arbitrary", "arbitrary")),
        name="outproj",
    )(a1, a2, w1, w2, xa, xb)


def _router_kernel(x_ref, nw_ref, wr_ref, br_ref, idx_ref, gate_ref):
    x = x_ref[...]
    h = (x * lax.rsqrt(jnp.mean(x * x, axis=-1, keepdims=True) + EPS)) * nw_ref[...]
    logits = _dot(h, wr_ref[...], precision=HIGHEST) + br_ref[...]
    lane = lax.broadcasted_iota(I32, logits.shape, 1)
    vals = logits
    idx_out = jnp.zeros(logits.shape, I32)
    top = []
    for kk in range(TOP_K):
        m = jnp.max(vals, axis=-1, keepdims=True)
        sel = jnp.min(jnp.where(vals == m, lane, LANE), axis=-1, keepdims=True)
        idx_out = jnp.where(lane == kk, sel, idx_out)
        top.append(m)
        vals = jnp.where(lane == sel, -jnp.inf, vals)
    es = [jnp.exp(m - top[0]) for m in top]
    den = es[0]
    for e in es[1:]:
        den = den + e
    gate_out = jnp.zeros(logits.shape, F32)
    for kk in range(TOP_K):
        gate_out = jnp.where(lane == kk, es[kk] / den, gate_out)
    idx_ref[...] = idx_out
    gate_ref[...] = gate_out


def _router(x1, nw, wr, br, tm):
    t, d = x1.shape
    return pl.pallas_call(
        _router_kernel,
        grid=(t // tm,),
        in_specs=[pl.BlockSpec((tm, d), lambda i: (i, 0)),
                  pl.BlockSpec((1, d), lambda i: (0, 0)),
                  pl.BlockSpec((d, LANE), lambda i: (0, 0)),
                  pl.BlockSpec((1, LANE), lambda i: (0, 0))],
        out_specs=[pl.BlockSpec((tm, LANE), lambda i: (i, 0)), pl.BlockSpec((tm, LANE), lambda i: (i, 0))],
        out_shape=[jax.ShapeDtypeStruct((t, LANE), I32), jax.ShapeDtypeStruct((t, LANE), F32)],
        compiler_params=_params(("parallel",)),
        name="router",
    )(x1, nw, wr, br)


def _row_copy(src_hbm, dst_vmem, sem, src_row, dst_row):
    return pltpu.make_async_copy(src_hbm.at[pl.ds(src_row, 1), :], dst_vmem.at[pl.ds(dst_row, 1), :], sem)


def _gather_rows(idx_ref, idx0, src_hbm, dst_vmem, sem):
    def body(r, carry):
        _row_copy(src_hbm, dst_vmem, sem, idx_ref[idx0 + r], r).start()
        return carry
    lax.fori_loop(0, dst_vmem.shape[0], body, 0, unroll=GATHER_UNROLL)


def _drain_rows(src_hbm, dst_vmem, sem):
    pltpu.make_async_copy(src_hbm.at[pl.ds(0, dst_vmem.shape[0]), :], dst_vmem, sem).wait()


def _dispatch_kernel(tok_ref, nused_ref, x_hbm, nw_ref, o_ref, buf, sem):
    b = pl.program_id(0)
    bm = o_ref.shape[0]
    n_used = nused_ref[0]
    slot = lax.rem(b, 2)

    @pl.when(jnp.logical_and(b == 0, n_used > 0))
    def _():
        _gather_rows(tok_ref, 0, x_hbm, buf.at[0], sem.at[0])

    @pl.when(b + 1 < n_used)
    def _():
        _gather_rows(tok_ref, (b + 1) * bm, x_hbm, buf.at[1 - slot], sem.at[1 - slot])

    @pl.when(b < n_used)
    def _():
        _drain_rows(x_hbm, buf.at[slot], sem.at[slot])
        x = buf[slot]
        h = (x * lax.rsqrt(jnp.mean(x * x, axis=-1, keepdims=True) + EPS)) * nw_ref[...]
        o_ref[...] = h.astype(o_ref.dtype)

    @pl.when(b >= n_used)
    def _():
        o_ref[...] = jnp.zeros(o_ref.shape, o_ref.dtype)


def _dispatch(slot_tok, n_used, x1, nw, nblk):
    t, d = x1.shape
    grid_spec = pltpu.PrefetchScalarGridSpec(
        num_scalar_prefetch=2,
        grid=(nblk,),
        in_specs=[pl.BlockSpec(memory_space=pl.ANY),
                  pl.BlockSpec((1, d), lambda b, *_: (0, 0))],
        out_specs=pl.BlockSpec((MOE_BM, d), lambda b, *_: (b, 0)),
        scratch_shapes=[pltpu.VMEM((2, MOE_BM, d), F32), pltpu.SemaphoreType.DMA((2,))],
    )
    return pl.pallas_call(
        _dispatch_kernel,
        grid_spec=grid_spec,
        out_shape=jax.ShapeDtypeStruct((nblk * MOE_BM, d), BF16),
        compiler_params=_params(("arbitrary",)),
        name="moe_dispatch",
    )(slot_tok, n_used, x1, nw)


def _resident_rows_matmul(blk0_ref, nblk_ref, nused_ref, src_hbm, dst_hbm,
                          xbuf, obuf, zbuf, sem_in, sem_out, sem_z, compute):
    s = pl.program_id(0)
    n = pl.program_id(1)
    nt = pl.num_programs(1)
    bm = MOE_BM
    tn = obuf.shape[2]
    n_blocks = nblk_ref[s]
    first = blk0_ref[s]
    lin = s * nt + n
    slot = lax.rem(lin, 2)
    col = pl.multiple_of(n * tn, tn)

    def hbm_rows(blk):
        return pl.ds(pl.multiple_of(blk * bm, bm), bm)

    def x_copy(i):
        return pltpu.make_async_copy(src_hbm.at[hbm_rows(first + i), :], xbuf.at[pl.ds(i * bm, bm), :], sem_in)

    def o_copy(slot_, blk, i):
        return pltpu.make_async_copy(obuf.at[slot_, pl.ds(i * bm, bm), :],
                                     dst_hbm.at[hbm_rows(blk), pl.ds(col, tn)], sem_out.at[slot_])

    def for_blocks(count, fn):
        for i in range(MOE_SB_BLOCKS):
            pl.when(i < count)(functools.partial(fn, i))

    def wait_out(step, slot_):
        for_blocks(nblk_ref[lax.div(step, nt)], lambda i: o_copy(slot_, 0, i).wait())

    @pl.when(lin == 0)
    def _():
        zbuf[...] = jnp.zeros(zbuf.shape, zbuf.dtype)

    @pl.when(n == 0)
    def _():
        for_blocks(n_blocks, lambda i: x_copy(i).start())
        for_blocks(n_blocks, lambda i: x_copy(i).wait())

    @pl.when(lin >= 2)
    def _():
        wait_out(lin - 2, slot)

    def matmul(blk_off, m):
        rows = pl.ds(pl.multiple_of(blk_off * bm, bm), m * bm)
        obuf[slot, rows, :] = compute(xbuf[rows, :])

    @pl.when(n_blocks == MOE_SB_BLOCKS)
    def _():
        matmul(0, MOE_SB_BLOCKS)

    @pl.when(jnp.logical_and(n_blocks > 0, n_blocks < MOE_SB_BLOCKS))
    def _():
        top = MOE_PARTIAL_HEIGHTS[0]
        n_top = lax.div(n_blocks, top)
        lax.fori_loop(0, n_top, lambda i, c: (matmul(i * top, top), c)[1], 0)
        done = n_top * top
        for m in MOE_PARTIAL_HEIGHTS[1:]:
            has = lax.rem(lax.div(n_blocks, m), 2) == 1
            pl.when(has)(functools.partial(matmul, done, m))
            done = done + jnp.where(has, m, 0)
    for_blocks(n_blocks, lambda i: o_copy(slot, first + i, i).start())

    @pl.when(lin == pl.num_programs(0) * nt - 1)
    def _():
        for_blocks(n_blocks, lambda i: o_copy(slot, 0, i).wait())

        @pl.when(lin >= 1)
        def _():
            wait_out(lin - 1, 1 - slot)

    @pl.when(s == pl.num_programs(0) - 1)
    def _():
        n_used = nused_ref[0]
        n_spare = dst_hbm.shape[0] // bm - n_used

        def z_copy(i):
            return pltpu.make_async_copy(zbuf, dst_hbm.at[hbm_rows(n_used + i), pl.ds(col, tn)], sem_z)
        lax.fori_loop(0, n_spare, lambda i, c: (z_copy(i).start(), c)[1], 0)
        lax.fori_loop(0, n_spare, lambda i, c: (z_copy(i).wait(), c)[1], 0)


def _resident_call(body, sb_e, sb_blk0, sb_nblk, n_used, src, weights, biases, dff_out, tn, out_dtype, name):
    p, k = src.shape
    nsb = sb_e.shape[0]
    nt = dff_out // tn

    def w_map(s, n, e_ref, b0_ref, nb_ref, nu_ref):
        return (e_ref[s], 0, jnp.where(nb_ref[s] > 0, n, nt - 1))

    grid_spec = pltpu.PrefetchScalarGridSpec(
        num_scalar_prefetch=4,
        grid=(nsb, nt),
        in_specs=([pl.BlockSpec(memory_space=pl.ANY)]
                  + [pl.BlockSpec((1, k, tn), w_map) for _ in weights]
                  + [pl.BlockSpec((1, 1, tn), w_map) for _ in biases]),
        out_specs=pl.BlockSpec(memory_space=pl.ANY),
        scratch_shapes=[pltpu.VMEM((MOE_SB_BLOCKS * MOE_BM, k), BF16),
                        pltpu.VMEM((2, MOE_SB_BLOCKS * MOE_BM, tn), out_dtype),
                        pltpu.VMEM((MOE_BM, tn), out_dtype),
                        pltpu.SemaphoreType.DMA(()), pltpu.SemaphoreType.DMA((2,)), pltpu.SemaphoreType.DMA(())],
    )
    return pl.pallas_call(
        body,
        grid_spec=grid_spec,
        out_shape=jax.ShapeDtypeStruct((p, dff_out), out_dtype),
        compiler_params=_params(("arbitrary", "arbitrary")),
        name=name,
    )(sb_e, sb_blk0, sb_nblk, n_used, src, *weights, *biases)


def _gateup_kernel(e_ref, blk0_ref, nblk_ref, nused_ref, x_hbm, wg_ref, wu_ref, bg_ref, bu_ref, act_hbm,
                   xbuf, obuf, zbuf, sem_in, sem_out, sem_z):
    def compute(x):
        gate = jnp.minimum(_dot(x, wg_ref[0].astype(BF16)) + bg_ref[0], SWIGLU_LIMIT)
        up = jnp.clip(_dot(x, wu_ref[0].astype(BF16)) + bu_ref[0], -SWIGLU_LIMIT, SWIGLU_LIMIT)
        act = gate * jax.nn.sigmoid(SWIGLU_ALPHA * gate) * (up + 1.0)
        return act.astype(obuf.dtype)

    _resident_rows_matmul(blk0_ref, nblk_ref, nused_ref, x_hbm, act_hbm,
                          xbuf, obuf, zbuf, sem_in, sem_out, sem_z, compute)


def _down_kernel(e_ref, blk0_ref, nblk_ref, nused_ref, a_hbm, wd_ref, bd_ref, y_hbm,
                 xbuf, obuf, zbuf, sem_in, sem_out, sem_z):
    def compute(a):
        return _dot(a, wd_ref[0].astype(BF16)) + bd_ref[0]

    _resident_rows_matmul(blk0_ref, nblk_ref, nused_ref, a_hbm, y_hbm,
                          xbuf, obuf, zbuf, sem_in, sem_out, sem_z, compute)


def _combine_kernel(pos_ref, ys_hbm, x_ref, g_ref, nw_ref, o_ref, buf, sem, *, tile0):
    i = pl.program_id(0)
    tt = o_ref.shape[0]
    n = TOP_K * tt
    slot = lax.rem(i, 2)

    def issue(tile, slot_):
        _gather_rows(pos_ref, (tile0 + tile) * n, ys_hbm, buf.at[slot_], sem.at[slot_])

    @pl.when(i == 0)
    def _():
        issue(0, 0)

    @pl.when(i + 1 < pl.num_programs(0))
    def _():
        issue(i + 1, 1 - slot)

    _drain_rows(ys_hbm, buf.at[slot], sem.at[slot])
    g = g_ref[...]
    acc = x_ref[...]
    for kk in range(TOP_K):
        acc = acc + buf[slot, kk * tt:(kk + 1) * tt, :] * g[:, kk:kk + 1]
    y = acc * lax.rsqrt(jnp.mean(acc * acc, axis=-1, keepdims=True) + EPS)
    o_ref[...] = y * nw_ref[...]


def _combine(pos_tiles, ys, x1, gates, nw, tt, tile0, n_tiles):
    d = x1.shape[1]
    grid_spec = pltpu.PrefetchScalarGridSpec(
        num_scalar_prefetch=1,
        grid=(n_tiles,),
        in_specs=[pl.BlockSpec(memory_space=pl.ANY),
                  pl.BlockSpec((tt, d), lambda i, pos: (tile0 + i, 0)),
                  pl.BlockSpec((tt, LANE), lambda i, pos: (tile0 + i, 0)),
                  pl.BlockSpec((1, d), lambda i, pos: (0, 0))],
        out_specs=pl.BlockSpec((tt, d), lambda i, pos: (i, 0)),
        scratch_shapes=[pltpu.VMEM((2, TOP_K * tt, d), F32), pltpu.SemaphoreType.DMA((2,))],
    )
    return pl.pallas_call(
        functools.partial(_combine_kernel, tile0=tile0),
        grid_spec=grid_spec,
        out_shape=jax.ShapeDtypeStruct((n_tiles * tt, d), F32),
        compiler_params=_params(("arbitrary",)),
        name="moe_combine",
    )(pos_tiles, ys, x1, gates, nw)


def _pad_lanes(v, offset):
    out = jnp.zeros((LANE,), F32)
    return lax.dynamic_update_slice(out, v.astype(F32), (offset,)).reshape(1, LANE)


def _ssd_group_layout(a, width):
    lead = a.shape[:-2]
    rows = a.shape[-2]
    rb = width // LANE // SSD_G
    xs = a[..., :width].reshape(*lead, rows, SSD_G, rb, LANE)
    bs = a[..., width:width + SSD_G * SSD_N].reshape(*lead, rows, SSD_G, 1, LANE)
    cs = a[..., width + SSD_G * SSD_N:].reshape(*lead, rows, SSD_G, 1, LANE)
    cat = jnp.concatenate([xs, bs, cs], axis=-2)
    n = cat.ndim
    return jnp.moveaxis(cat, n - 4, n - 2)


def _gdn_block_layout(a, heads, hb):
    lead = a.shape[:-2]
    rows = a.shape[-2]
    r = a.reshape(*lead, rows, 3, heads // hb, hb, LANE)
    n = r.ndim
    r = jnp.moveaxis(r, n - 5, n - 2)
    r = jnp.moveaxis(r, n - 5, n - 4)
    return r.reshape(*lead, heads // hb, 3 * hb, rows, LANE)


def _pick_tn_cb(ncb):
    return INPROJ_TN_CB, (-ncb) % INPROJ_TN_CB


def kernel(x_prompt, x_sample, state_ssd_conv, state_ssd, state_gdn_conv, state_gdn, norm_mix, w_in, ssd_conv_w,
           ssd_conv_b, ssd_dt_bias, ssd_A_log, ssd_D, ssd_norm, gdn_conv_w, gdn_dt_bias, gdn_A_log, gdn_norm, w_out,
           norm_ffn, w_router, b_router, w_gate, b_gate, w_up, b_up, w_down, b_down, norm_final):
    assert w_in.shape[0] == 1, "single layer"
    nb_p, seq_p, d = x_prompt.shape
    nb_s, seq_s, _ = x_sample.shape
    assert seq_p % CHUNK == 0 and seq_s % CHUNK == 0
    ssd_heads = d // SSD_P
    ssd_w = ssd_heads * SSD_P
    ssd_cs = ssd_w + 2 * SSD_G * SSD_N
    gdn_heads = d // GDN_D
    gdn_w = gdn_heads * GDN_D
    t_p = nb_p * seq_p
    t = t_p + nb_s * seq_s
    nseq = nb_p + nb_s

    x_p = x_prompt.reshape(t_p, d)
    x_s = x_sample.reshape(nb_s * seq_s, d)
    seq_len = [seq_p] * nb_p + [seq_s] * nb_s
    seq_id, start, end = [], [], []
    for s, n in enumerate(seq_len):
        for cidx in range(n // CHUNK):
            seq_id.append(s)
            start.append(int(cidx == 0))
            end.append(int(cidx == n // CHUNK - 1))
    seq_id = jnp.asarray(np.array(seq_id, np.int32))
    start = jnp.asarray(np.array(start, np.int32))
    end = jnp.asarray(np.array(end, np.int32))

    o_z, o_xbc, o_dt = 0, ssd_w, ssd_w + ssd_cs
    o_qkv = o_dt + ssd_heads
    o_zg = o_qkv + 3 * gdn_w
    o_b = o_zg + gdn_w
    o_a = o_b + gdn_heads
    w0 = w_in[0]
    n_small = ssd_heads + 2 * gdn_heads
    assert n_small <= LANE
    cb_q = 0
    cb_zg = cb_q + 3 * gdn_w // LANE
    cb_z = cb_zg + gdn_w // LANE
    cb_x = cb_z + ssd_w // LANE
    cb_sm = cb_x + ssd_cs // LANE
    ncb = cb_sm + 1
    tn_cb, pad_cb = _pick_tn_cb(ncb)
    w_perm = jnp.concatenate([
        w0[:, o_qkv:o_qkv + 3 * gdn_w], w0[:, o_zg:o_zg + gdn_w], w0[:, o_z:o_z + ssd_w],
        w0[:, o_xbc:o_xbc + ssd_cs], w0[:, o_dt:o_dt + ssd_heads], w0[:, o_b:o_b + 2 * gdn_heads],
        jnp.zeros((d, LANE - n_small + pad_cb * LANE), F32)], axis=1).astype(BF16)
    lane_beta = ssd_heads
    lane_a = ssd_heads + gdn_heads

    tm_big = _row_tile(t, 1056)
    tm_mid = _row_tile(t, 528)
    tm_src = _row_tile(int(np.gcd(t_p, t - t_p)), 512)
    h = _rmsnorm_cast(x_p, x_s, norm_mix[0], tm_src)
    proj = _inproj(h, w_perm, tm_big, tn_cb)

    def with_zero_prompt(a):
        return jnp.concatenate([jnp.zeros((nb_p,) + a.shape[1:], a.dtype), a], axis=0)

    ssd_conv0 = _ssd_group_layout(with_zero_prompt(state_ssd_conv[0]), ssd_w)
    gdn_hb = min(GDN_HB, gdn_heads)
    gdn_conv0 = _gdn_block_layout(with_zero_prompt(state_gdn_conv[0]), gdn_heads, gdn_hb)
    s0 = with_zero_prompt(state_ssd[0])
    ssd_s0 = s0.reshape(nseq, ssd_heads // 2, 2, SSD_P, SSD_N).transpose(0, 1, 4, 2, 3).reshape(
        nseq, ssd_heads // 2, SSD_N, LANE)
    gdn_s0 = with_zero_prompt(state_gdn[0])

    ssd_cw = _ssd_group_layout(ssd_conv_w[0][None], ssd_w)[0]
    ssd_cw = jnp.swapaxes(ssd_cw, 1, 2)[:, :, :, None, :]
    ssd_cb = _ssd_group_layout(ssd_conv_b[0][None, None], ssd_w)[0]
    d_exp = jnp.repeat(ssd_D[0], SSD_P).reshape(ssd_w // LANE, 1, LANE)
    ssd_nw = ssd_norm[0].reshape(ssd_w // LANE, 1, LANE)
    y_ssd, ssd_s = _ssd_mixer(
        proj, seq_id, start, end, ssd_conv0, ssd_s0, ssd_cw, ssd_cb,
        _pad_lanes(ssd_dt_bias[0], 0), _pad_lanes(ssd_A_log[0], 0), d_exp, ssd_nw,
        t, ssd_w, cb_z, cb_x, cb_x + ssd_w // LANE, cb_x + ssd_w // LANE + SSD_G, cb_sm)

    gdn_cw = _gdn_block_layout(gdn_conv_w[0][None], gdn_heads, gdn_hb)[0]
    gdn_cw = jnp.swapaxes(gdn_cw, 1, 2)[:, :, :, None, :]
    y_gdn, gdn_s = _gdn_mixer(
        proj, seq_id, start, end, gdn_conv0, gdn_s0, gdn_cw,
        _pad_lanes(gdn_dt_bias[0], lane_a), _pad_lanes(gdn_A_log[0], lane_a), gdn_norm[0].reshape(1, LANE),
        t, gdn_heads, cb_q, cb_zg, cb_sm, lane_beta, lane_a)

    w_o = w_out[0].astype(BF16)
    x1 = _outproj(y_ssd, y_gdn, w_o[:ssd_w], w_o[ssd_w:], x_p, x_s, tm_src, min(512, d))

    wr = jnp.concatenate([w_router[0], jnp.zeros((d, LANE - N_EXPERTS), F32)], axis=1)
    br = jnp.concatenate([b_router[0], jnp.full((LANE - N_EXPERTS,), NEG_BIG, F32)]).reshape(1, LANE)
    nffn = norm_ffn[0].reshape(1, d)
    idx_pad, gate_pad = _router(x1, nffn, wr, br, tm_mid)
    top_idx = idx_pad[:, :TOP_K]
    tk = t * TOP_K
    flat_e = top_idx.reshape(tk)
    order = jnp.argsort(flat_e).astype(I32)
    sorted_e = flat_e[order]
    counts = jnp.sum((flat_e[:, None] == jnp.arange(N_EXPERTS, dtype=I32)[None, :]).astype(I32), axis=0)
    starts = jnp.cumsum(counts) - counts
    pcounts = (counts + MOE_BM - 1) // MOE_BM * MOE_BM
    pends = jnp.cumsum(pcounts)
    pstarts = pends - pcounts
    dest = (pstarts[sorted_e] + (jnp.arange(tk, dtype=I32) - starts[sorted_e])).astype(I32)
    nblk = -(-tk // MOE_BM) + N_EXPERTS
    pos = jnp.zeros((tk,), I32).at[order].set(dest)
    blk_e = jnp.minimum(jnp.searchsorted(pends, jnp.arange(nblk, dtype=I32) * MOE_BM, side='right'),
                        N_EXPERTS - 1).astype(I32)
    n_used = (pends[-1] // MOE_BM).astype(I32).reshape(1)
    blk0 = (pstarts // MOE_BM).astype(I32)
    nblk_e = (pcounts // MOE_BM).astype(I32)
    slot = jnp.arange(nblk * MOE_BM, dtype=I32)
    slot_e = jnp.repeat(blk_e, MOE_BM)
    slot_off = slot - pstarts[slot_e].astype(I32)
    slot_src = jnp.clip(starts[slot_e].astype(I32) + slot_off, 0, tk - 1)
    slot_tok = jnp.where(slot_off < counts[slot_e], (order // TOP_K)[slot_src], 0).astype(I32)
    nsb_e = (nblk_e + MOE_SB_BLOCKS - 1) // MOE_SB_BLOCKS
    sb_ends = jnp.cumsum(nsb_e)
    sb_starts = sb_ends - nsb_e
    n_sb = sb_ends[-1]
    sb_i = jnp.arange(N_EXPERTS + nblk // MOE_SB_BLOCKS, dtype=I32)
    sb_c = jnp.minimum(sb_i, n_sb - 1)
    sb_e = jnp.minimum(jnp.searchsorted(sb_ends, sb_c, side='right'), N_EXPERTS - 1).astype(I32)
    sb_j = sb_c - sb_starts[sb_e]
    sb_blk0 = (blk0[sb_e] + sb_j * MOE_SB_BLOCKS).astype(I32)
    sb_nblk = jnp.where(sb_i < n_sb, jnp.clip(nblk_e[sb_e] - sb_j * MOE_SB_BLOCKS, 0, MOE_SB_BLOCKS), 0).astype(I32)

    xs = _dispatch(slot_tok, n_used, x1, nffn, nblk)
    tn_ff = min(MOE_TN, d)
    act = _resident_call(_gateup_kernel, sb_e, sb_blk0, sb_nblk, n_used, xs, (w_gate[0], w_up[0]),
                         (b_gate[0][:, None, :], b_up[0][:, None, :]), w_gate.shape[3], tn_ff, BF16, "moe_gateup")
    ys = _resident_call(_down_kernel, sb_e, sb_blk0, sb_nblk, n_used, act, (w_down[0],),
                        (b_down[0][:, None, :],), d, tn_ff, F32, "moe_down")

    tt = CHUNK
    pos_tiles = pos.reshape(t // tt, tt, TOP_K).transpose(0, 2, 1).reshape(tk)
    nfin = norm_final.reshape(1, d)
    y_prompt = _combine(pos_tiles, ys, x1, gate_pad, nfin, tt, 0, t_p // tt).reshape(nb_p, seq_p, d)
    y_sample = _combine(pos_tiles, ys, x1, gate_pad, nfin, tt, t_p // tt, (t - t_p) // tt).reshape(nb_s, seq_s, d)

    def last_rows(cb0, ncols):
        nblk_c = ncols // LANE
        ends = np.cumsum(seq_len)
        rows = jnp.concatenate(
            [lax.slice(proj, (cb0, int(e) - (CONV_K - 1), 0), (cb0 + nblk_c, int(e), LANE)) for e in ends], axis=1)
        a = rows.reshape(nblk_c, nseq, CONV_K - 1, LANE).transpose(1, 2, 0, 3).reshape(nseq, CONV_K - 1, ncols)
        return a[:nb_p][None], a[nb_p:][None]

    ssd_conv_p, ssd_conv_s = last_rows(cb_x, ssd_cs)
    gdn_conv_p, gdn_conv_s = last_rows(cb_q, 3 * gdn_w)
    ssd_state = ssd_s.reshape(nseq, ssd_heads // 2, SSD_N, 2, SSD_P).transpose(0, 1, 3, 4, 2).reshape(
        nseq, ssd_heads, SSD_P, SSD_N)
    return (y_prompt, y_sample,
            ssd_conv_p, ssd_state[:nb_p][None], gdn_conv_p, gdn_s[:nb_p][None],
            ssd_conv_s, ssd_state[nb_p:][None], gdn_conv_s, gdn_s[nb_p:][None])
```

```python
import functools

import numpy as np
import jax
import jax.numpy as jnp
from jax import lax
from jax.experimental import pallas as pl
from jax.experimental.pallas import tpu as pltpu

F32 = jnp.float32
BF16 = jnp.bfloat16
I32 = jnp.int32

LANE = 128
SUBLANE = 8
VMEM_LIMIT = 60 * 1024 * 1024

CHUNK = 64
CONV_K = 4
SSD_P = 64
SSD_N = 128
SSD_G = 8
GDN_D = 128
N_EXPERTS = 32
TOP_K = 4
SWIGLU_LIMIT = 7.0
SWIGLU_ALPHA = 1.702
EPS = 1e-6
NEG_BIG = -1e30

MOE_BM = 256
MOE_SB_BLOCKS = 9
MOE_PARTIAL_HEIGHTS = (4, 2, 1)
MOE_TN = 256
GATHER_UNROLL = 8
INPROJ_TN_CB = 6
GDN_HB = 32
SSD_GS = 8
HIGHEST = lax.Precision.HIGHEST


def _row_tile(n, target, mult=16):
    best = None
    for t in range(mult, min(n, target) + 1, mult):
        if n % t == 0:
            best = t
    assert best is not None, (n, target)
    return best


def _params(sem):
    return pltpu.CompilerParams(dimension_semantics=sem, vmem_limit_bytes=VMEM_LIMIT)


def _silu(x):
    return x * jax.nn.sigmoid(x)


def _softplus(x):
    return jnp.maximum(x, 0.0) + jnp.log1p(jnp.exp(-jnp.abs(x)))


def _dot(a, b, **kw):
    return jnp.dot(a, b, preferred_element_type=F32, **kw)


def _split3(a):
    hi = a.astype(BF16)
    r = a - hi.astype(F32)
    mid = r.astype(BF16)
    lo = (r - mid.astype(F32)).astype(BF16)
    return hi, mid, lo


def _two_source_specs(block, na, col=None):
    if col is None:
        ia = lambda i, *_: (jnp.minimum(i, na - 1), 0)
        ib = lambda i, *_: (jnp.maximum(i - na, 0), 0)
    else:
        ia = lambda i, j, *_: (jnp.minimum(i, na - 1), jnp.where(i < na, j, col))
        ib = lambda i, j, *_: (jnp.maximum(i - na, 0), jnp.where(i >= na, j, 0))
    return pl.BlockSpec(block, ia), pl.BlockSpec(block, ib)


def _rmsnorm_cast_kernel(xa_ref, xb_ref, w_ref, o_ref, *, na):
    def body(x_ref):
        x = x_ref[...]
        xn = x * lax.rsqrt(jnp.mean(x * x, axis=-1, keepdims=True) + EPS)
        o_ref[...] = (xn * w_ref[...]).astype(o_ref.dtype)

    pl.when(pl.program_id(0) < na)(lambda: body(xa_ref))
    pl.when(pl.program_id(0) >= na)(lambda: body(xb_ref))


def _rmsnorm_cast(xa, xb, w, tm):
    ta, d = xa.shape
    tb = xb.shape[0]
    na = ta // tm
    spec_a, spec_b = _two_source_specs((tm, d), na)
    return pl.pallas_call(
        functools.partial(_rmsnorm_cast_kernel, na=na),
        grid=(na + tb // tm,),
        in_specs=[spec_a, spec_b, pl.BlockSpec((1, d), lambda i: (0, 0))],
        out_specs=pl.BlockSpec((tm, d), lambda i: (i, 0)),
        out_shape=jax.ShapeDtypeStruct((ta + tb, d), BF16),
        compiler_params=_params(("arbitrary",)),
        name="rmsnorm_cast",
    )(xa, xb, w.reshape(1, d))


def _inproj_kernel(a_ref, w_ref, o_ref):
    ncb = o_ref.shape[0]
    step = 2 if ncb % 2 == 0 else 1
    for j in range(0, ncb, step):
        acc = _dot(a_ref[...], w_ref[:, j * LANE:(j + step) * LANE])
        for s in range(step):
            o_ref[j + s] = acc[:, s * LANE:(s + 1) * LANE]


def _inproj(a, w, tm, tn_cb):
    t, d = a.shape
    n = w.shape[1]
    ncb = n // LANE
    return pl.pallas_call(
        _inproj_kernel,
        grid=(t // tm, ncb // tn_cb),
        in_specs=[pl.BlockSpec((tm, d), lambda i, j: (i, 0), pipeline_mode=pl.Buffered(1)),
                  pl.BlockSpec((d, tn_cb * LANE), lambda i, j: (0, j))],
        out_specs=pl.BlockSpec((tn_cb, tm, LANE), lambda i, j: (j, i, 0)),
        out_shape=jax.ShapeDtypeStruct((ncb, t, LANE), F32),
        compiler_params=_params(("arbitrary", "arbitrary")),
        name="inproj",
    )(a, w)


def _causal_conv(u, ext_scr, conv_in, taps, is_start):
    L = u.shape[1]
    base = SUBLANE - (CONV_K - 1)

    @pl.when(is_start)
    def _():
        ext_scr[:, base:SUBLANE, :] = conv_in()

    ext_scr[:, SUBLANE:SUBLANE + L, :] = u
    acc = ext_scr[:, base:base + L, :] * taps(0)
    for j in range(1, CONV_K):
        acc = acc + ext_scr[:, base + j:base + j + L, :] * taps(j)
    ext_scr[:, base:SUBLANE, :] = ext_scr[:, base + L:SUBLANE + L, :]
    return acc


def _cumsum_rows(a, incl):
    m = incl.astype(BF16)
    hi, mid, lo = _split3(a)
    return _dot(m, hi) + (_dot(m, mid) + _dot(m, lo))


def _select_dot(a, onehot):
    oh = onehot.astype(BF16)
    hi, mid, lo = _split3(a)
    return _dot(hi, oh) + (_dot(mid, oh) + _dot(lo, oh))


def _ssd_kernel(seq_ref, start_ref, end_ref,
                z_ref, x_ref, b_ref, c_ref, sm_ref, cin_ref, sin_ref, cw_ref, cb_ref,
                dtb_ref, alog_ref, d_ref, nw_ref,
                y_ref, sout_ref,
                ext_scr, s_scr):
    gi0 = pl.program_id(0)
    c = pl.program_id(1)
    gs = b_ref.shape[0]
    rb = x_ref.shape[0] // gs
    nb = rb + 2
    L = x_ref.shape[1]
    is_start = start_ref[c] == 1

    @pl.when(is_start)
    def _():
        s_scr[...] = sin_ref[0]

    u = jnp.concatenate(
        [p for gi in range(gs) for p in (x_ref[gi * rb:(gi + 1) * rb], b_ref[gi:gi + 1], c_ref[gi:gi + 1])], axis=0)
    conv = _causal_conv(u, ext_scr, lambda: cin_ref[0].reshape(gs * nb, CONV_K - 1, LANE),
                        lambda j: cw_ref[:, j].reshape(gs * nb, 1, LANE), is_start)
    uc = _silu(conv + cb_ref[...].reshape(gs * nb, 1, LANE))

    sm = sm_ref[0]
    dt_all = _softplus(sm + dtb_ref[...])
    a_all = dt_all * (-jnp.exp(alog_ref[...]))
    ti = lax.broadcasted_iota(I32, (L, L), 0)
    si = lax.broadcasted_iota(I32, (L, L), 1)
    cum_all = _cumsum_rows(a_all, ti >= si)
    ej = lax.broadcasted_iota(I32, (LANE, gs * rb * LANE), 0)
    ec = lax.broadcasted_iota(I32, (LANE, gs * rb * LANE), 1)
    expand = ej == gi0 * (gs * 2 * rb) + jnp.right_shift(ec, 6)
    dtx = _select_dot(dt_all, expand)
    cumx = _select_dot(cum_all, expand)

    t2 = lax.broadcasted_iota(I32, (L, LANE), 0)
    l2 = lax.broadcasted_iota(I32, (L, LANE), 1)
    s2 = jnp.bitwise_and(l2, SSD_P - 1)
    diag2 = (t2 == s2).astype(F32)
    causal2 = t2 >= s2
    left = l2 < SSD_P

    for gi in range(gs):
        bm = uc[gi * nb + rb]
        cm = uc[gi * nb + rb + 1]
        b2 = jnp.concatenate([bm, bm], axis=0).astype(BF16)
        cm_b = cm.astype(BF16)
        bm_b = bm.astype(BF16)
        cb2 = lax.dot_general(cm_b, b2, (((1,), (1,)), ((), ())), preferred_element_type=F32)
        ygs = []
        ms = jnp.zeros((L, 1), F32)
        for j in range(rb):
            jj = gi * rb + j
            ccol = cumx[:, jj * LANE:(jj + 1) * LANE]
            dtc = dtx[:, jj * LANE:(jj + 1) * LANE]
            crow = jnp.sum(ccol * diag2, axis=0, keepdims=True)
            dec = jnp.exp(jnp.where(causal2, ccol - crow, NEG_BIG))
            ww = (cb2 * dec).astype(BF16)
            xb = uc[gi * nb + j]
            xdt = xb * dtc
            xbd = jnp.concatenate([jnp.where(left, xdt, 0.0), jnp.where(left, 0.0, xdt)], axis=0).astype(BF16)
            y = _dot(ww, xbd)
            sj = s_scr[jj]
            y = y + _dot(cm_b, sj.astype(BF16)) * jnp.exp(ccol)
            y = y + d_ref[jj] * xb
            cl = ccol[L - 1:L, :]
            xw = (xdt * jnp.exp(cl - ccol)).astype(BF16)
            s_scr[jj] = sj * jnp.exp(cl) + lax.dot_general(
                bm_b, xw, (((0,), (0,)), ((), ())), preferred_element_type=F32)
            yg = y * _silu(z_ref[jj])
            ms = ms + jnp.sum(yg * yg, axis=-1, keepdims=True)
            ygs.append(yg)
        inv = lax.rsqrt(ms / (rb * LANE) + EPS)
        for j in range(rb):
            jj = gi * rb + j
            y_ref[:, jj * LANE:(jj + 1) * LANE] = ((ygs[j] * inv) * nw_ref[jj]).astype(y_ref.dtype)

    @pl.when(end_ref[c] == 1)
    def _():
        sout_ref[0] = s_scr[...]


def _ssd_mixer(proj, seq_id, start, end, conv_in, state_in, conv_w, conv_b, dtb, alog, d_exp, nw,
               t, width, cb_z, cb_x, cb_b, cb_c, cb_sm):
    rb = width // LANE // SSD_G
    nb = rb + 2
    nc = t // CHUNK
    nseq = state_in.shape[0]
    L = CHUNK

    def im(f):
        return lambda g, c, s, st, en: f(g, c, s)

    gs = SSD_GS
    grb = gs * rb
    grid_spec = pltpu.PrefetchScalarGridSpec(
        num_scalar_prefetch=3,
        grid=(SSD_G // gs, nc),
        in_specs=[
            pl.BlockSpec((grb, L, LANE), im(lambda g, c, s: (cb_z // grb + g, c, 0))),
            pl.BlockSpec((grb, L, LANE), im(lambda g, c, s: (cb_x // grb + g, c, 0))),
            pl.BlockSpec((gs, L, LANE), im(lambda g, c, s: (cb_b // gs + g, c, 0))),
            pl.BlockSpec((gs, L, LANE), im(lambda g, c, s: (cb_c // gs + g, c, 0))),
            pl.BlockSpec((1, L, LANE), im(lambda g, c, s: (cb_sm, c, 0))),
            pl.BlockSpec((1, gs, nb, CONV_K - 1, LANE), im(lambda g, c, s: (s[c], g, 0, 0, 0))),
            pl.BlockSpec((1, grb, SSD_N, LANE), im(lambda g, c, s: (s[c], g, 0, 0))),
            pl.BlockSpec((gs, CONV_K, nb, 1, LANE), im(lambda g, c, s: (g, 0, 0, 0, 0))),
            pl.BlockSpec((gs, nb, 1, LANE), im(lambda g, c, s: (g, 0, 0, 0))),
            pl.BlockSpec((1, LANE), im(lambda g, c, s: (0, 0))),
            pl.BlockSpec((1, LANE), im(lambda g, c, s: (0, 0))),
            pl.BlockSpec((grb, 1, LANE), im(lambda g, c, s: (g, 0, 0))),
            pl.BlockSpec((grb, 1, LANE), im(lambda g, c, s: (g, 0, 0))),
        ],
        out_specs=[
            pl.BlockSpec((L, grb * LANE), im(lambda g, c, s: (c, g))),
            pl.BlockSpec((1, grb, SSD_N, LANE), im(lambda g, c, s: (s[c], g, 0, 0))),
        ],
        scratch_shapes=[pltpu.VMEM((gs * nb, L + SUBLANE, LANE), F32), pltpu.VMEM((grb, SSD_N, LANE), F32)],
    )
    assert cb_z % grb == 0 and cb_x % grb == 0 and cb_b % gs == 0 and cb_c % gs == 0 and SSD_G % gs == 0
    return pl.pallas_call(
        _ssd_kernel,
        grid_spec=grid_spec,
        out_shape=[jax.ShapeDtypeStruct((t, width), BF16),
                   jax.ShapeDtypeStruct((nseq, width // LANE, SSD_N, LANE), F32)],
        compiler_params=_params(("arbitrary", "arbitrary")),
        name="ssd_mixer",
    )(seq_id, start, end, proj, proj, proj, proj, proj, conv_in, state_in, conv_w, conv_b, dtb, alog, d_exp, nw)


def _bdot(a, b, ca, cb):
    return lax.dot_general(a, b, (((ca,), (cb,)), ((0,), (0,))), preferred_element_type=F32)


def _hi_lo(a):
    hi = a.astype(BF16)
    return hi, (a - hi.astype(F32)).astype(BF16)


def _bdot3_shared_rhs(lhs_list, b):
    n = len(lhs_list)
    rows = lhs_list[0].shape[1]
    pieces = [_hi_lo(a) for a in lhs_list]
    bh, bl = _hi_lo(b)
    his = [p[0] for p in pieces]
    t_hi = _bdot(jnp.concatenate(his + [p[1] for p in pieces], axis=1), bh, 2, 1)
    t_lo = _bdot(jnp.concatenate(his, axis=1), bl, 2, 1) if n > 1 else _bdot(his[0], bl, 2, 1)
    out = []
    for i in range(n):
        sl = lambda t, j: t[:, j * rows:(j + 1) * rows]
        out.append(sl(t_hi, i) + (sl(t_hi, n + i) + sl(t_lo, i)))
    return out


def _gdn_kernel(seq_ref, start_ref, end_ref,
                q_ref, k_ref, v_ref, z_ref, sm_ref, cin_ref, sin_ref, cw_ref,
                dtb_ref, alog_ref, nw_ref,
                y_ref, sout_ref,
                ext_scr, s_scr, *, lane_beta, lane_a):
    hb_i = pl.program_id(0)
    c = pl.program_id(1)
    hb = q_ref.shape[0]
    L = q_ref.shape[1]
    is_start = start_ref[c] == 1

    @pl.when(is_start)
    def _():
        s_scr[...] = sin_ref[0]

    u = jnp.concatenate([q_ref[...], k_ref[...], v_ref[...]], axis=0)
    uc = _silu(_causal_conv(u, ext_scr, lambda: cin_ref[0, 0], lambda j: cw_ref[0, j], is_start))
    q = uc[:hb]
    k = uc[hb:2 * hb]
    v = uc[2 * hb:]
    q = q * (lax.rsqrt(jnp.sum(q * q, axis=-1, keepdims=True) + EPS) * (GDN_D ** -0.5))
    k = k * lax.rsqrt(jnp.sum(k * k, axis=-1, keepdims=True) + EPS)

    sm = sm_ref[0]
    beta_all = jax.nn.sigmoid(sm)
    g_all = -jnp.exp(alog_ref[...]) * _softplus(sm + dtb_ref[...])
    ti = lax.broadcasted_iota(I32, (L, L), 0)
    si = lax.broadcasted_iota(I32, (L, L), 1)
    incl = ti >= si
    strict = ti > si
    gam_all = _cumsum_rows(g_all, incl)
    ej = lax.broadcasted_iota(I32, (LANE, hb * LANE), 0)
    ec = jnp.right_shift(lax.broadcasted_iota(I32, (LANE, hb * LANE), 1), 7) + hb_i * hb
    betax = _select_dot(beta_all, ej == ec + lane_beta)
    gamx = _select_dot(gam_all, ej == ec + lane_a)
    beta_c = jnp.stack([betax[:, h * LANE:(h + 1) * LANE] for h in range(hb)])
    gam_c = jnp.stack([gamx[:, h * LANE:(h + 1) * LANE] for h in range(hb)])

    t2 = lax.broadcasted_iota(I32, (L, LANE), 0)
    l2 = lax.broadcasted_iota(I32, (L, LANE), 1)
    diag2 = (t2 == l2).astype(F32)
    gam_r = jnp.sum(gam_c * diag2, axis=1, keepdims=True)[:, :, :L]
    gam_t = gam_c[:, :, :L]
    gam_m = jnp.exp(jnp.where(incl, gam_t - gam_r, NEG_BIG))

    kb = k.astype(BF16)
    kk = _bdot(kb, kb, 2, 2)
    a_mat = jnp.where(strict, beta_c[:, :, :L] * kk * gam_m, 0.0)
    n_pow = -a_mat
    x_inv = jnp.where(ti == si, 1.0, 0.0) + n_pow
    (n_pow,) = _bdot3_shared_rhs([n_pow], n_pow)
    span = 4
    while span < L:
        xp, n_next = _bdot3_shared_rhs([x_inv, n_pow], n_pow)
        x_inv = x_inv + xp
        n_pow = n_next
        span *= 2
    x_inv = x_inv + _bdot3_shared_rhs([x_inv], n_pow)[0]

    eg = jnp.exp(gam_c)
    rhs = jnp.concatenate([v * beta_c, k * (beta_c * eg)], axis=-1)
    (sol,) = _bdot3_shared_rhs([x_inv], rhs)
    u_ = sol[:, :, :GDN_D]
    w_ = sol[:, :, GDN_D:]
    s_prev = s_scr[...]
    s_b = s_prev.astype(BF16)
    v_new = u_ - _bdot(w_.astype(BF16), s_b, 2, 1)
    vn_b = v_new.astype(BF16)
    qk = _bdot(q.astype(BF16), kb, 2, 2) * gam_m
    o = _bdot((q * eg).astype(BF16), s_b, 2, 1) + _bdot(qk.astype(BF16), vn_b, 2, 1)
    gl = gam_c[:, L - 1:L, :]
    kt = (k * jnp.exp(gl - gam_c)).astype(BF16)
    for h in range(hb):
        upd = lax.dot_general(kt[h], vn_b[h], (((0,), (0,)), ((), ())), preferred_element_type=F32)
        s_scr[h] = s_prev[h] * jnp.exp(gl[h]) + upd

    o = o * lax.rsqrt(jnp.mean(o * o, axis=-1, keepdims=True) + EPS)
    o = (o * nw_ref[...]) * _silu(z_ref[...])
    for h in range(hb):
        y_ref[:, h * LANE:(h + 1) * LANE] = o[h].astype(y_ref.dtype)

    @pl.when(end_ref[c] == 1)
    def _():
        sout_ref[0] = s_scr[...]


def _gdn_mixer(proj, seq_id, start, end, conv_in, state_in, conv_w, dtb, alog, nw,
               t, heads, cb_q, cb_z, cb_sm, lane_beta, lane_a):
    hb = min(GDN_HB, heads)
    assert heads % hb == 0 and cb_q % hb == 0 and cb_z % hb == 0
    nhb = heads // hb
    nc = t // CHUNK
    nseq = state_in.shape[0]
    L = CHUNK

    def im(f):
        return lambda h, c, s, st, en: f(h, c, s)

    grid_spec = pltpu.PrefetchScalarGridSpec(
        num_scalar_prefetch=3,
        grid=(nhb, nc),
        in_specs=[
            pl.BlockSpec((hb, L, LANE), im(lambda h, c, s: (cb_q // hb + h, c, 0))),
            pl.BlockSpec((hb, L, LANE), im(lambda h, c, s: ((cb_q + heads) // hb + h, c, 0))),
            pl.BlockSpec((hb, L, LANE), im(lambda h, c, s: ((cb_q + 2 * heads) // hb + h, c, 0))),
            pl.BlockSpec((hb, L, LANE), im(lambda h, c, s: (cb_z // hb + h, c, 0))),
            pl.BlockSpec((1, L, LANE), im(lambda h, c, s: (cb_sm, c, 0))),
            pl.BlockSpec((1, 1, 3 * hb, CONV_K - 1, LANE), im(lambda h, c, s: (s[c], h, 0, 0, 0))),
            pl.BlockSpec((1, hb, GDN_D, GDN_D), im(lambda h, c, s: (s[c], h, 0, 0))),
            pl.BlockSpec((1, CONV_K, 3 * hb, 1, LANE), im(lambda h, c, s: (h, 0, 0, 0, 0))),
            pl.BlockSpec((1, LANE), im(lambda h, c, s: (0, 0))),
            pl.BlockSpec((1, LANE), im(lambda h, c, s: (0, 0))),
            pl.BlockSpec((1, LANE), im(lambda h, c, s: (0, 0))),
        ],
        out_specs=[
            pl.BlockSpec((L, hb * LANE), im(lambda h, c, s: (c, h))),
            pl.BlockSpec((1, hb, GDN_D, GDN_D), im(lambda h, c, s: (s[c], h, 0, 0))),
        ],
        scratch_shapes=[pltpu.VMEM((3 * hb, L + SUBLANE, LANE), F32), pltpu.VMEM((hb, GDN_D, GDN_D), F32)],
    )
    return pl.pallas_call(
        functools.partial(_gdn_kernel, lane_beta=lane_beta, lane_a=lane_a),
        grid_spec=grid_spec,
        out_shape=[jax.ShapeDtypeStruct((t, heads * GDN_D), BF16),
                   jax.ShapeDtypeStruct((nseq, heads, GDN_D, GDN_D), F32)],
        compiler_params=_params(("arbitrary", "arbitrary")),
        name="gdn_mixer",
    )(seq_id, start, end, proj, proj, proj, proj, proj, conv_in, state_in, conv_w, dtb, alog, nw)


def _outproj_kernel(a1_ref, a2_ref, w1_ref, w2_ref, xa_ref, xb_ref, o_ref, *, na):
    acc = _dot(a1_ref[...], w1_ref[...]) + _dot(a2_ref[...], w2_ref[...])

    @pl.when(pl.program_id(0) < na)
    def _():
        o_ref[...] = xa_ref[...] + acc

    @pl.when(pl.program_id(0) >= na)
    def _():
        o_ref[...] = xb_ref[...] + acc


def _outproj(a1, a2, w1, w2, xa, xb, tm, tn):
    t, k1 = a1.shape
    k2 = a2.shape[1]
    d = w1.shape[1]
    na = xa.shape[0] // tm
    spec_a, spec_b = _two_source_specs((tm, tn), na, col=d // tn - 1)
    return pl.pallas_call(
        functools.partial(_outproj_kernel, na=na),
        grid=(t // tm, d // tn),
        in_specs=[pl.BlockSpec((tm, k1), lambda i, j: (i, 0)),
                  pl.BlockSpec((tm, k2), lambda i, j: (i, 0)),
                  pl.BlockSpec((k1, tn), lambda i, j: (0, j)),
                  pl.BlockSpec((k2, tn), lambda i, j: (0, j)),
                  spec_a, spec_b],
        out_specs=pl.BlockSpec((tm, tn), lambda i, j: (i, j)),
        out_shape=jax.ShapeDtypeStruct((t, d), F32),
        compiler_params=_params(("arbitrary", "arbitrary")),
        name="outproj",
    )(a1, a2, w1, w2, xa, xb)


def _router_kernel(x_ref, nw_ref, wr_ref, br_ref, idx_ref, gate_ref):
    x = x_ref[...]
    h = (x * lax.rsqrt(jnp.mean(x * x, axis=-1, keepdims=True) + EPS)) * nw_ref[...]
    logits = _dot(h, wr_ref[...], precision=HIGHEST) + br_ref[...]
    lane = lax.broadcasted_iota(I32, logits.shape, 1)
    vals = logits
    idx_out = jnp.zeros(logits.shape, I32)
    top = []
    for kk in range(TOP_K):
        m = jnp.max(vals, axis=-1, keepdims=True)
        sel = jnp.min(jnp.where(vals == m, lane, LANE), axis=-1, keepdims=True)
        idx_out = jnp.where(lane == kk, sel, idx_out)
        top.append(m)
        vals = jnp.where(lane == sel, -jnp.inf, vals)
    es = [jnp.exp(m - top[0]) for m in top]
    den = es[0]
    for e in es[1:]:
        den = den + e
    gate_out = jnp.zeros(logits.shape, F32)
    for kk in range(TOP_K):
        gate_out = jnp.where(lane == kk, es[kk] / den, gate_out)
    idx_ref[...] = idx_out
    gate_ref[...] = gate_out


def _router(x1, nw, wr, br, tm):
    t, d = x1.shape
    return pl.pallas_call(
        _router_kernel,
        grid=(t // tm,),
        in_specs=[pl.BlockSpec((tm, d), lambda i: (i, 0)),
                  pl.BlockSpec((1, d), lambda i: (0, 0)),
                  pl.BlockSpec((d, LANE), lambda i: (0, 0)),
                  pl.BlockSpec((1, LANE), lambda i: (0, 0))],
        out_specs=[pl.BlockSpec((tm, LANE), lambda i: (i, 0)), pl.BlockSpec((tm, LANE), lambda i: (i, 0))],
        out_shape=[jax.ShapeDtypeStruct((t, LANE), I32), jax.ShapeDtypeStruct((t, LANE), F32)],
        compiler_params=_params(("parallel",)),
        name="router",
    )(x1, nw, wr, br)


def _row_copy(src_hbm, dst_vmem, sem, src_row, dst_row):
    return pltpu.make_async_copy(src_hbm.at[pl.ds(src_row, 1), :], dst_vmem.at[pl.ds(dst_row, 1), :], sem)


def _gather_rows(idx_ref, idx0, src_hbm, dst_vmem, sem):
    def body(r, carry):
        _row_copy(src_hbm, dst_vmem, sem, idx_ref[idx0 + r], r).start()
        return carry
    lax.fori_loop(0, dst_vmem.shape[0], body, 0, unroll=GATHER_UNROLL)


def _drain_rows(src_hbm, dst_vmem, sem):
    pltpu.make_async_copy(src_hbm.at[pl.ds(0, dst_vmem.shape[0]), :], dst_vmem, sem).wait()


def _dispatch_kernel(tok_ref, nused_ref, x_hbm, nw_ref, o_ref, buf, sem):
    b = pl.program_id(0)
    bm = o_ref.shape[0]
    n_used = nused_ref[0]
    slot = lax.rem(b, 2)

    @pl.when(jnp.logical_and(b == 0, n_used > 0))
    def _():
        _gather_rows(tok_ref, 0, x_hbm, buf.at[0], sem.at[0])

    @pl.when(b + 1 < n_used)
    def _():
        _gather_rows(tok_ref, (b + 1) * bm, x_hbm, buf.at[1 - slot], sem.at[1 - slot])

    @pl.when(b < n_used)
    def _():
        _drain_rows(x_hbm, buf.at[slot], sem.at[slot])
        x = buf[slot]
        h = (x * lax.rsqrt(jnp.mean(x * x, axis=-1, keepdims=True) + EPS)) * nw_ref[...]
        o_ref[...] = h.astype(o_ref.dtype)

    @pl.when(b >= n_used)
    def _():
        o_ref[...] = jnp.zeros(o_ref.shape, o_ref.dtype)


def _dispatch(slot_tok, n_used, x1, nw, nblk):
    t, d = x1.shape
    grid_spec = pltpu.PrefetchScalarGridSpec(
        num_scalar_prefetch=2,
        grid=(nblk,),
        in_specs=[pl.BlockSpec(memory_space=pl.ANY),
                  pl.BlockSpec((1, d), lambda b, *_: (0, 0))],
        out_specs=pl.BlockSpec((MOE_BM, d), lambda b, *_: (b, 0)),
        scratch_shapes=[pltpu.VMEM((2, MOE_BM, d), F32), pltpu.SemaphoreType.DMA((2,))],
    )
    return pl.pallas_call(
        _dispatch_kernel,
        grid_spec=grid_spec,
        out_shape=jax.ShapeDtypeStruct((nblk * MOE_BM, d), BF16),
        compiler_params=_params(("arbitrary",)),
        name="moe_dispatch",
    )(slot_tok, n_used, x1, nw)


def _resident_rows_matmul(blk0_ref, nblk_ref, nused_ref, src_hbm, dst_hbm,
                          xbuf, obuf, zbuf, sem_in, sem_out, sem_z, compute):
    s = pl.program_id(0)
    n = pl.program_id(1)
    nt = pl.num_programs(1)
    bm = MOE_BM
    tn = obuf.shape[2]
    n_blocks = nblk_ref[s]
    first = blk0_ref[s]
    lin = s * nt + n
    slot = lax.rem(lin, 2)
    col = pl.multiple_of(n * tn, tn)

    def hbm_rows(blk):
        return pl.ds(pl.multiple_of(blk * bm, bm), bm)

    def x_copy(i):
        return pltpu.make_async_copy(src_hbm.at[hbm_rows(first + i), :], xbuf.at[pl.ds(i * bm, bm), :], sem_in)

    def o_copy(slot_, blk, i):
        return pltpu.make_async_copy(obuf.at[slot_, pl.ds(i * bm, bm), :],
                                     dst_hbm.at[hbm_rows(blk), pl.ds(col, tn)], sem_out.at[slot_])

    def for_blocks(count, fn):
        for i in range(MOE_SB_BLOCKS):
            pl.when(i < count)(functools.partial(fn, i))

    def wait_out(step, slot_):
        for_blocks(nblk_ref[lax.div(step, nt)], lambda i: o_copy(slot_, 0, i).wait())

    @pl.when(lin == 0)
    def _():
        zbuf[...] = jnp.zeros(zbuf.shape, zbuf.dtype)

    @pl.when(n == 0)
    def _():
        for_blocks(n_blocks, lambda i: x_copy(i).start())
        for_blocks(n_blocks, lambda i: x_copy(i).wait())

    @pl.when(lin >= 2)
    def _():
        wait_out(lin - 2, slot)

    def matmul(blk_off, m):
        rows = pl.ds(pl.multiple_of(blk_off * bm, bm), m * bm)
        obuf[slot, rows, :] = compute(lambda: xbuf[rows, :])

    @pl.when(n_blocks == MOE_SB_BLOCKS)
    def _():
        matmul(0, MOE_SB_BLOCKS)

    @pl.when(jnp.logical_and(n_blocks > 0, n_blocks < MOE_SB_BLOCKS))
    def _():
        top = MOE_PARTIAL_HEIGHTS[0]
        n_top = lax.div(n_blocks, top)
        lax.fori_loop(0, n_top, lambda i, c: (matmul(i * top, top), c)[1], 0)
        done = n_top * top
        for m in MOE_PARTIAL_HEIGHTS[1:]:
            has = lax.rem(lax.div(n_blocks, m), 2) == 1
            pl.when(has)(functools.partial(matmul, done, m))
            done = done + jnp.where(has, m, 0)
    for_blocks(n_blocks, lambda i: o_copy(slot, first + i, i).start())

    @pl.when(lin == pl.num_programs(0) * nt - 1)
    def _():
        for_blocks(n_blocks, lambda i: o_copy(slot, 0, i).wait())

        @pl.when(lin >= 1)
        def _():
            wait_out(lin - 1, 1 - slot)

    @pl.when(s == pl.num_programs(0) - 1)
    def _():
        n_used = nused_ref[0]
        n_spare = dst_hbm.shape[0] // bm - n_used

        def z_copy(i):
            return pltpu.make_async_copy(zbuf, dst_hbm.at[hbm_rows(n_used + i), pl.ds(col, tn)], sem_z)
        lax.fori_loop(0, n_spare, lambda i, c: (z_copy(i).start(), c)[1], 0)
        lax.fori_loop(0, n_spare, lambda i, c: (z_copy(i).wait(), c)[1], 0)


def _resident_call(body, sb_e, sb_blk0, sb_nblk, n_used, src, weights, biases, dff_out, tn, out_dtype, name):
    p, k = src.shape
    nsb = sb_e.shape[0]
    nt = dff_out // tn

    def w_map(s, n, e_ref, b0_ref, nb_ref, nu_ref):
        return (e_ref[s], 0, jnp.where(nb_ref[s] > 0, n, nt - 1))

    grid_spec = pltpu.PrefetchScalarGridSpec(
        num_scalar_prefetch=4,
        grid=(nsb, nt),
        in_specs=([pl.BlockSpec(memory_space=pl.ANY)]
                  + [pl.BlockSpec((1, k, tn), w_map) for _ in weights]
                  + [pl.BlockSpec((1, 1, tn), w_map) for _ in biases]),
        out_specs=pl.BlockSpec(memory_space=pl.ANY),
        scratch_shapes=[pltpu.VMEM((MOE_SB_BLOCKS * MOE_BM, k), BF16),
                        pltpu.VMEM((2, MOE_SB_BLOCKS * MOE_BM, tn), out_dtype),
                        pltpu.VMEM((MOE_BM, tn), out_dtype),
                        pltpu.SemaphoreType.DMA(()), pltpu.SemaphoreType.DMA((2,)), pltpu.SemaphoreType.DMA(())],
    )
    return pl.pallas_call(
        body,
        grid_spec=grid_spec,
        out_shape=jax.ShapeDtypeStruct((p, dff_out), out_dtype),
        compiler_params=_params(("arbitrary", "arbitrary")),
        name=name,
    )(sb_e, sb_blk0, sb_nblk, n_used, src, *weights, *biases)


def _gateup_kernel(e_ref, blk0_ref, nblk_ref, nused_ref, x_hbm, wg_ref, wu_ref, bg_ref, bu_ref, act_hbm,
                   xbuf, obuf, zbuf, sem_in, sem_out, sem_z):
    def compute(x):
        gate = jnp.minimum(_dot(x(), wg_ref[0].astype(BF16)) + bg_ref[0], SWIGLU_LIMIT)
        up = jnp.clip(_dot(x(), wu_ref[0].astype(BF16)) + bu_ref[0], -SWIGLU_LIMIT, SWIGLU_LIMIT)
        act = gate * jax.nn.sigmoid(SWIGLU_ALPHA * gate) * (up + 1.0)
        return act.astype(obuf.dtype)

    _resident_rows_matmul(blk0_ref, nblk_ref, nused_ref, x_hbm, act_hbm,
                          xbuf, obuf, zbuf, sem_in, sem_out, sem_z, compute)


def _down_kernel(e_ref, blk0_ref, nblk_ref, nused_ref, a_hbm, wd_ref, bd_ref, y_hbm,
                 xbuf, obuf, zbuf, sem_in, sem_out, sem_z):
    def compute(a):
        return _dot(a(), wd_ref[0].astype(BF16)) + bd_ref[0]

    _resident_rows_matmul(blk0_ref, nblk_ref, nused_ref, a_hbm, y_hbm,
                          xbuf, obuf, zbuf, sem_in, sem_out, sem_z, compute)


def _combine_kernel(pos_ref, ys_hbm, x_ref, g_ref, nw_ref, o_ref, buf, sem, *, tile0):
    i = pl.program_id(0)
    tt = o_ref.shape[0]
    n = TOP_K * tt
    slot = lax.rem(i, 2)

    def issue(tile, slot_):
        _gather_rows(pos_ref, (tile0 + tile) * n, ys_hbm, buf.at[slot_], sem.at[slot_])

    @pl.when(i == 0)
    def _():
        issue(0, 0)

    @pl.when(i + 1 < pl.num_programs(0))
    def _():
        issue(i + 1, 1 - slot)

    _drain_rows(ys_hbm, buf.at[slot], sem.at[slot])
    g = g_ref[...]
    acc = x_ref[...]
    for kk in range(TOP_K):
        acc = acc + buf[slot, kk * tt:(kk + 1) * tt, :] * g[:, kk:kk + 1]
    y = acc * lax.rsqrt(jnp.mean(acc * acc, axis=-1, keepdims=True) + EPS)
    o_ref[...] = y * nw_ref[...]


def _combine(pos_tiles, ys, x1, gates, nw, tt, tile0, n_tiles):
    d = x1.shape[1]
    grid_spec = pltpu.PrefetchScalarGridSpec(
        num_scalar_prefetch=1,
        grid=(n_tiles,),
        in_specs=[pl.BlockSpec(memory_space=pl.ANY),
                  pl.BlockSpec((tt, d), lambda i, pos: (tile0 + i, 0)),
                  pl.BlockSpec((tt, LANE), lambda i, pos: (tile0 + i, 0)),
                  pl.BlockSpec((1, d), lambda i, pos: (0, 0))],
        out_specs=pl.BlockSpec((tt, d), lambda i, pos: (i, 0)),
        scratch_shapes=[pltpu.VMEM((2, TOP_K * tt, d), F32), pltpu.SemaphoreType.DMA((2,))],
    )
    return pl.pallas_call(
        functools.partial(_combine_kernel, tile0=tile0),
        grid_spec=grid_spec,
        out_shape=jax.ShapeDtypeStruct((n_tiles * tt, d), F32),
        compiler_params=_params(("arbitrary",)),
        name="moe_combine",
    )(pos_tiles, ys, x1, gates, nw)


def _pad_lanes(v, offset):
    out = jnp.zeros((LANE,), F32)
    return lax.dynamic_update_slice(out, v.astype(F32), (offset,)).reshape(1, LANE)


def _ssd_group_layout(a, width):
    lead = a.shape[:-2]
    rows = a.shape[-2]
    rb = width // LANE // SSD_G
    xs = a[..., :width].reshape(*lead, rows, SSD_G, rb, LANE)
    bs = a[..., width:width + SSD_G * SSD_N].reshape(*lead, rows, SSD_G, 1, LANE)
    cs = a[..., width + SSD_G * SSD_N:].reshape(*lead, rows, SSD_G, 1, LANE)
    cat = jnp.concatenate([xs, bs, cs], axis=-2)
    n = cat.ndim
    return jnp.moveaxis(cat, n - 4, n - 2)


def _gdn_block_layout(a, heads, hb):
    lead = a.shape[:-2]
    rows = a.shape[-2]
    r = a.reshape(*lead, rows, 3, heads // hb, hb, LANE)
    n = r.ndim
    r = jnp.moveaxis(r, n - 5, n - 2)
    r = jnp.moveaxis(r, n - 5, n - 4)
    return r.reshape(*lead, heads // hb, 3 * hb, rows, LANE)


def _pick_tn_cb(ncb):
    return INPROJ_TN_CB, (-ncb) % INPROJ_TN_CB


def kernel(x_prompt, x_sample, state_ssd_conv, state_ssd, state_gdn_conv, state_gdn, norm_mix, w_in, ssd_conv_w,
           ssd_conv_b, ssd_dt_bias, ssd_A_log, ssd_D, ssd_norm, gdn_conv_w, gdn_dt_bias, gdn_A_log, gdn_norm, w_out,
           norm_ffn, w_router, b_router, w_gate, b_gate, w_up, b_up, w_down, b_down, norm_final):
    assert w_in.shape[0] == 1, "single layer"
    nb_p, seq_p, d = x_prompt.shape
    nb_s, seq_s, _ = x_sample.shape
    assert seq_p % CHUNK == 0 and seq_s % CHUNK == 0
    ssd_heads = d // SSD_P
    ssd_w = ssd_heads * SSD_P
    ssd_cs = ssd_w + 2 * SSD_G * SSD_N
    gdn_heads = d // GDN_D
    gdn_w = gdn_heads * GDN_D
    t_p = nb_p * seq_p
    t = t_p + nb_s * seq_s
    nseq = nb_p + nb_s

    x_p = x_prompt.reshape(t_p, d)
    x_s = x_sample.reshape(nb_s * seq_s, d)
    seq_len = [seq_p] * nb_p + [seq_s] * nb_s
    seq_id, start, end = [], [], []
    for s, n in enumerate(seq_len):
        for cidx in range(n // CHUNK):
            seq_id.append(s)
            start.append(int(cidx == 0))
            end.append(int(cidx == n // CHUNK - 1))
    seq_id = jnp.asarray(np.array(seq_id, np.int32))
    start = jnp.asarray(np.array(start, np.int32))
    end = jnp.asarray(np.array(end, np.int32))

    o_z, o_xbc, o_dt = 0, ssd_w, ssd_w + ssd_cs
    o_qkv = o_dt + ssd_heads
    o_zg = o_qkv + 3 * gdn_w
    o_b = o_zg + gdn_w
    o_a = o_b + gdn_heads
    w0 = w_in[0].astype(BF16)
    n_small = ssd_heads + 2 * gdn_heads
    assert n_small <= LANE
    cb_q = 0
    cb_zg = cb_q + 3 * gdn_w // LANE
    cb_z = cb_zg + gdn_w // LANE
    cb_x = cb_z + ssd_w // LANE
    cb_sm = cb_x + ssd_cs // LANE
    ncb = cb_sm + 1
    tn_cb, pad_cb = _pick_tn_cb(ncb)
    w_perm = jnp.concatenate([
        w0[:, o_qkv:o_qkv + 3 * gdn_w], w0[:, o_zg:o_zg + gdn_w], w0[:, o_z:o_z + ssd_w],
        w0[:, o_xbc:o_xbc + ssd_cs], w0[:, o_dt:o_dt + ssd_heads], w0[:, o_b:o_b + 2 * gdn_heads],
        jnp.zeros((d, LANE - n_small + pad_cb * LANE), BF16)], axis=1)
    lane_beta = ssd_heads
    lane_a = ssd_heads + gdn_heads

    tm_big = _row_tile(t, 2112)
    tm_mid = _row_tile(t, 528)
    tm_src = _row_tile(int(np.gcd(t_p, t - t_p)), 512)
    h = _rmsnorm_cast(x_p, x_s, norm_mix[0], tm_src)
    proj = _inproj(h, w_perm, tm_big, tn_cb)

    def with_zero_prompt(a):
        return jnp.concatenate([jnp.zeros((nb_p,) + a.shape[1:], a.dtype), a], axis=0)

    ssd_conv0 = _ssd_group_layout(with_zero_prompt(state_ssd_conv[0]), ssd_w)
    gdn_hb = min(GDN_HB, gdn_heads)
    gdn_conv0 = _gdn_block_layout(with_zero_prompt(state_gdn_conv[0]), gdn_heads, gdn_hb)
    s0 = with_zero_prompt(state_ssd[0])
    ssd_s0 = s0.reshape(nseq, ssd_heads // 2, 2, SSD_P, SSD_N).transpose(0, 1, 4, 2, 3).reshape(
        nseq, ssd_heads // 2, SSD_N, LANE)
    gdn_s0 = with_zero_prompt(state_gdn[0])

    ssd_cw = _ssd_group_layout(ssd_conv_w[0][None], ssd_w)[0]
    ssd_cw = jnp.swapaxes(ssd_cw, 1, 2)[:, :, :, None, :]
    ssd_cb = _ssd_group_layout(ssd_conv_b[0][None, None], ssd_w)[0]
    d_exp = jnp.repeat(ssd_D[0], SSD_P).reshape(ssd_w // LANE, 1, LANE)
    ssd_nw = ssd_norm[0].reshape(ssd_w // LANE, 1, LANE)
    y_ssd, ssd_s = _ssd_mixer(
        proj, seq_id, start, end, ssd_conv0, ssd_s0, ssd_cw, ssd_cb,
        _pad_lanes(ssd_dt_bias[0], 0), _pad_lanes(ssd_A_log[0], 0), d_exp, ssd_nw,
        t, ssd_w, cb_z, cb_x, cb_x + ssd_w // LANE, cb_x + ssd_w // LANE + SSD_G, cb_sm)

    gdn_cw = _gdn_block_layout(gdn_conv_w[0][None], gdn_heads, gdn_hb)[0]
    gdn_cw = jnp.swapaxes(gdn_cw, 1, 2)[:, :, :, None, :]
    y_gdn, gdn_s = _gdn_mixer(
        proj, seq_id, start, end, gdn_conv0, gdn_s0, gdn_cw,
        _pad_lanes(gdn_dt_bias[0], lane_a), _pad_lanes(gdn_A_log[0], lane_a), gdn_norm[0].reshape(1, LANE),
        t, gdn_heads, cb_q, cb_zg, cb_sm, lane_beta, lane_a)

    w_o = w_out[0].astype(BF16)
    x1 = _outproj(y_ssd, y_gdn, w_o[:ssd_w], w_o[ssd_w:], x_p, x_s, tm_src, min(512, d))

    wr = jnp.concatenate([w_router[0], jnp.zeros((d, LANE - N_EXPERTS), F32)], axis=1)
    br = jnp.concatenate([b_router[0], jnp.full((LANE - N_EXPERTS,), NEG_BIG, F32)]).reshape(1, LANE)
    nffn = norm_ffn[0].reshape(1, d)
    idx_pad, gate_pad = _router(x1, nffn, wr, br, tm_mid)
    top_idx = idx_pad[:, :TOP_K]
    tk = t * TOP_K
    flat_e = top_idx.reshape(tk)
    order = jnp.argsort(flat_e).astype(I32)
    sorted_e = flat_e[order]
    counts = jnp.sum((flat_e[:, None] == jnp.arange(N_EXPERTS, dtype=I32)[None, :]).astype(I32), axis=0)
    starts = jnp.cumsum(counts) - counts
    pcounts = (counts + MOE_BM - 1) // MOE_BM * MOE_BM
    pends = jnp.cumsum(pcounts)
    pstarts = pends - pcounts
    dest = (pstarts[sorted_e] + (jnp.arange(tk, dtype=I32) - starts[sorted_e])).astype(I32)
    nblk = -(-tk // MOE_BM) + N_EXPERTS
    pos = jnp.zeros((tk,), I32).at[order].set(dest)
    def count_le(ends, v):
        return jnp.sum((ends[None, :] <= v[:, None]).astype(I32), axis=1)

    blk_e = jnp.minimum(count_le(pends, jnp.arange(nblk, dtype=I32) * MOE_BM), N_EXPERTS - 1).astype(I32)
    n_used = (pends[-1] // MOE_BM).astype(I32).reshape(1)
    blk0 = (pstarts // MOE_BM).astype(I32)
    nblk_e = (pcounts // MOE_BM).astype(I32)
    slot = jnp.arange(nblk * MOE_BM, dtype=I32)
    slot_e = jnp.repeat(blk_e, MOE_BM)
    slot_off = slot - pstarts[slot_e].astype(I32)
    slot_src = jnp.clip(starts[slot_e].astype(I32) + slot_off, 0, tk - 1)
    slot_tok = jnp.where(slot_off < counts[slot_e], (order // TOP_K)[slot_src], 0).astype(I32)
    nsb_e = (nblk_e + MOE_SB_BLOCKS - 1) // MOE_SB_BLOCKS
    sb_ends = jnp.cumsum(nsb_e)
    sb_starts = sb_ends - nsb_e
    n_sb = sb_ends[-1]
    sb_i = jnp.arange(N_EXPERTS + nblk // MOE_SB_BLOCKS, dtype=I32)
    sb_c = jnp.minimum(sb_i, n_sb - 1)
    sb_e = jnp.minimum(count_le(sb_ends, sb_c), N_EXPERTS - 1).astype(I32)
    sb_j = sb_c - sb_starts[sb_e]
    sb_blk0 = (blk0[sb_e] + sb_j * MOE_SB_BLOCKS).astype(I32)
    sb_nblk = jnp.where(sb_i < n_sb, jnp.clip(nblk_e[sb_e] - sb_j * MOE_SB_BLOCKS, 0, MOE_SB_BLOCKS), 0).astype(I32)

    xs = _dispatch(slot_tok, n_used, x1, nffn, nblk)
    tn_ff = min(MOE_TN, d)
    act = _resident_call(_gateup_kernel, sb_e, sb_blk0, sb_nblk, n_used, xs, (w_gate[0], w_up[0]),
                         (b_gate[0][:, None, :], b_up[0][:, None, :]), w_gate.shape[3], tn_ff, BF16, "moe_gateup")
    ys = _resident_call(_down_kernel, sb_e, sb_blk0, sb_nblk, n_used, act, (w_down[0],),
                        (b_down[0][:, None, :],), d, tn_ff, F32, "moe_down")

    tt = CHUNK
    pos_tiles = pos.reshape(t // tt, tt, TOP_K).transpose(0, 2, 1).reshape(tk)
    nfin = norm_final.reshape(1, d)
    y_prompt = _combine(pos_tiles, ys, x1, gate_pad, nfin, tt, 0, t_p // tt).reshape(nb_p, seq_p, d)
    y_sample = _combine(pos_tiles, ys, x1, gate_pad, nfin, tt, t_p // tt, (t - t_p) // tt).reshape(nb_s, seq_s, d)

    def last_rows(cb0, ncols):
        nblk_c = ncols // LANE
        ends = np.cumsum(seq_len)
        rows = jnp.concatenate(
            [lax.slice(proj, (cb0, int(e) - (CONV_K - 1), 0), (cb0 + nblk_c, int(e), LANE)) for e in ends], axis=1)
        a = rows.reshape(nblk_c, nseq, CONV_K - 1, LANE).transpose(1, 2, 0, 3).reshape(nseq, CONV_K - 1, ncols)
        return a[:nb_p][None], a[nb_p:][None]

    ssd_conv_p, ssd_conv_s = last_rows(cb_x, ssd_cs)
    gdn_conv_p, gdn_conv_s = last_rows(cb_q, 3 * gdn_w)
    ssd_state = ssd_s.reshape(nseq, ssd_heads // 2, SSD_N, 2, SSD_P).transpose(0, 1, 3, 4, 2).reshape(
        nseq, ssd_heads, SSD_P, SSD_N)
    return (y_prompt, y_sample,
            ssd_conv_p, ssd_state[:nb_p][None], gdn_conv_p, gdn_s[:nb_p][None],
            ssd_conv_s, ssd_state[nb_p:][None], gdn_conv_s, gdn_s[nb_p:][None])
```

```python
import functools

import numpy as np
import jax
import jax.numpy as jnp
from jax import lax
from jax.experimental import pallas as pl
from jax.experimental.pallas import tpu as pltpu

F32 = jnp.float32
BF16 = jnp.bfloat16
I32 = jnp.int32

LANE = 128
SUBLANE = 8
VMEM_LIMIT = 60 * 1024 * 1024

CHUNK = 64
CONV_K = 4
SSD_P = 64
SSD_N = 128
SSD_G = 8
GDN_D = 128
N_EXPERTS = 32
TOP_K = 4
SWIGLU_LIMIT = 7.0
SWIGLU_ALPHA = 1.702
EPS = 1e-6
NEG_BIG = -1e30

MOE_BM = 256
MOE_SB_BLOCKS = 9
MOE_PARTIAL_HEIGHTS = (4, 2, 1)
MOE_TN = 256
MOE_TN_DOWN = 512
GATHER_UNROLL = 8
ROW_CHUNK = 16
INPROJ_TN_CB = 10
GDN_HB = 32
SSD_GS = 8
HIGHEST = lax.Precision.HIGHEST


def _row_tile(n, target, mult=16):
    best = None
    for t in range(mult, min(n, target) + 1, mult):
        if n % t == 0:
            best = t
    assert best is not None, (n, target)
    return best


def _params(sem):
    return pltpu.CompilerParams(dimension_semantics=sem, vmem_limit_bytes=VMEM_LIMIT)


def _silu(x):
    return x * jax.nn.sigmoid(x)


def _softplus(x):
    return jnp.maximum(x, 0.0) + jnp.log1p(jnp.exp(-jnp.abs(x)))


def _dot(a, b, **kw):
    return jnp.dot(a, b, preferred_element_type=F32, **kw)


def _split3(a):
    hi = a.astype(BF16)
    r = a - hi.astype(F32)
    mid = r.astype(BF16)
    lo = (r - mid.astype(F32)).astype(BF16)
    return hi, mid, lo


def _two_source_specs(block, na, col=None):
    if col is None:
        ia = lambda i, *_: (jnp.minimum(i, na - 1), 0)
        ib = lambda i, *_: (jnp.maximum(i - na, 0), 0)
    else:
        ia = lambda i, j, *_: (jnp.minimum(i, na - 1), jnp.where(i < na, j, col))
        ib = lambda i, j, *_: (jnp.maximum(i - na, 0), jnp.where(i >= na, j, 0))
    return pl.BlockSpec(block, ia), pl.BlockSpec(block, ib)


def _rmsnorm_cast_kernel(xa_ref, xb_ref, w_ref, o_ref, *, na):
    def body(x_ref):
        x = x_ref[...]
        xn = x * lax.rsqrt(jnp.mean(x * x, axis=-1, keepdims=True) + EPS)
        o_ref[...] = (xn * w_ref[...]).astype(o_ref.dtype)

    pl.when(pl.program_id(0) < na)(lambda: body(xa_ref))
    pl.when(pl.program_id(0) >= na)(lambda: body(xb_ref))


def _rmsnorm_cast(xa, xb, w, tm):
    ta, d = xa.shape
    tb = xb.shape[0]
    na = ta // tm
    spec_a, spec_b = _two_source_specs((tm, d), na)
    return pl.pallas_call(
        functools.partial(_rmsnorm_cast_kernel, na=na),
        grid=(na + tb // tm,),
        in_specs=[spec_a, spec_b, pl.BlockSpec((1, d), lambda i: (0, 0))],
        out_specs=pl.BlockSpec((tm, d), lambda i: (i, 0)),
        out_shape=jax.ShapeDtypeStruct((ta + tb, d), BF16),
        compiler_params=_params(("arbitrary",)),
        name="rmsnorm_cast",
    )(xa, xb, w.reshape(1, d))


def _wperm_kernel(main_ref, right_ref, dt_ref, ba_ref, o_ref, *, t_b, t_a, shift, n_dt, n_small):
    j = pl.program_id(0)
    d, ct = o_ref.shape
    rc = 64

    def by_rows(fn):
        def body(i, carry):
            rows = pl.ds(pl.multiple_of(i * rc, rc), rc)
            o_ref[rows, :] = fn(rows).astype(o_ref.dtype)
            return carry
        lax.fori_loop(0, d // rc, body, 0)

    @pl.when(j < t_b)
    def _():
        by_rows(lambda rows: jnp.concatenate([main_ref[rows, :], right_ref[rows, :]], axis=1)[:, shift:shift + ct])

    @pl.when(jnp.logical_and(j >= t_b, j < t_b + t_a))
    def _():
        by_rows(lambda rows: main_ref[rows, :])

    @pl.when(j == t_b + t_a)
    def _():
        def small(rows):
            lane = lax.broadcasted_iota(I32, (rc, LANE), 1)
            blk = jnp.where(lane < n_dt, dt_ref[rows, :], jnp.where(lane < n_small, ba_ref[rows, :], 0.0))
            return jnp.concatenate([blk, jnp.zeros((rc, ct - LANE), F32)], axis=1)
        by_rows(small)

    @pl.when(j > t_b + t_a)
    def _():
        o_ref[...] = jnp.zeros(o_ref.shape, o_ref.dtype)


def _permute_cast_weight(w, o_b_part, w_b_part, w_a_part, o_dt, o_ba, n_dt, n_small, n_out):
    d = w.shape[0]
    ct = 2 * LANE
    shift = o_b_part % LANE
    base = o_b_part - shift
    assert base % ct == 0 and w_b_part % ct == 0 and w_a_part % ct == 0 and n_out % ct == 0
    assert o_dt % LANE == 0 and o_ba % LANE == n_dt and n_out >= w_b_part + w_a_part + ct
    t_b, t_a = w_b_part // ct, w_a_part // ct

    def main_map(j):
        return (0, jnp.where(j < t_b, base // ct + j, jnp.where(j < t_b + t_a, j - t_b, 0)))

    return pl.pallas_call(
        functools.partial(_wperm_kernel, t_b=t_b, t_a=t_a, shift=shift, n_dt=n_dt, n_small=n_small),
        grid=(n_out // ct,),
        in_specs=[pl.BlockSpec((d, ct), main_map),
                  pl.BlockSpec((d, LANE), lambda j: (0, base // LANE + 2 * (jnp.minimum(j, t_b - 1) + 1))),
                  pl.BlockSpec((d, LANE), lambda j: (0, o_dt // LANE)),
                  pl.BlockSpec((d, LANE), lambda j: (0, o_ba // LANE))],
        out_specs=pl.BlockSpec((d, ct), lambda j: (0, j)),
        out_shape=jax.ShapeDtypeStruct((d, n_out), BF16),
        compiler_params=_params(("arbitrary",)),
        name="inproj_weight_layout",
    )(w, w, w, w)


def _inproj_kernel(a_ref, w_ref, o_ref):
    ncb = o_ref.shape[0]
    step = 2 if ncb % 2 == 0 else 1
    for j in range(0, ncb, step):
        acc = _dot(a_ref[...], w_ref[:, j * LANE:(j + step) * LANE])
        for s in range(step):
            o_ref[j + s] = acc[:, s * LANE:(s + 1) * LANE]


def _inproj(a, w, tm, tn_cb):
    t, d = a.shape
    n = w.shape[1]
    ncb = n // LANE
    return pl.pallas_call(
        _inproj_kernel,
        grid=(t // tm, ncb // tn_cb),
        in_specs=[pl.BlockSpec((tm, d), lambda i, j: (i, 0)),
                  pl.BlockSpec((d, tn_cb * LANE), lambda i, j: (0, j))],
        out_specs=pl.BlockSpec((tn_cb, tm, LANE), lambda i, j: (j, i, 0)),
        out_shape=jax.ShapeDtypeStruct((ncb, t, LANE), F32),
        compiler_params=_params(("arbitrary", "arbitrary")),
        name="inproj",
    )(a, w)


def _causal_conv(u, ext_scr, conv_in, taps, is_start):
    L = u.shape[1]
    base = SUBLANE - (CONV_K - 1)

    @pl.when(is_start)
    def _():
        ext_scr[:, base:SUBLANE, :] = conv_in()

    ext_scr[:, SUBLANE:SUBLANE + L, :] = u
    acc = ext_scr[:, base:base + L, :] * taps(0)
    for j in range(1, CONV_K):
        acc = acc + ext_scr[:, base + j:base + j + L, :] * taps(j)
    ext_scr[:, base:SUBLANE, :] = ext_scr[:, base + L:SUBLANE + L, :]
    return acc


def _cumsum_rows(a, incl):
    m = incl.astype(BF16)
    hi, mid, lo = _split3(a)
    return _dot(m, hi) + (_dot(m, mid) + _dot(m, lo))


def _select_dot(a, onehot):
    oh = onehot.astype(BF16)
    hi, mid, lo = _split3(a)
    return _dot(hi, oh) + (_dot(mid, oh) + _dot(lo, oh))


def _ssd_kernel(seq_ref, start_ref, end_ref,
                z_ref, x_ref, b_ref, c_ref, sm_ref, cin_ref, sin_ref, cw_ref, cb_ref,
                dtb_ref, alog_ref, d_ref, nw_ref,
                y_ref, sout_ref,
                ext_scr, s_scr):
    gi0 = pl.program_id(0)
    c = pl.program_id(1)
    gs = b_ref.shape[0]
    rb = x_ref.shape[0] // gs
    nb = rb + 2
    L = x_ref.shape[1]
    is_start = start_ref[c] == 1

    @pl.when(is_start)
    def _():
        s_scr[...] = sin_ref[0]

    u = jnp.concatenate(
        [p for gi in range(gs) for p in (x_ref[gi * rb:(gi + 1) * rb], b_ref[gi:gi + 1], c_ref[gi:gi + 1])], axis=0)
    conv = _causal_conv(u, ext_scr, lambda: cin_ref[0].reshape(gs * nb, CONV_K - 1, LANE),
                        lambda j: cw_ref[:, j].reshape(gs * nb, 1, LANE), is_start)
    uc = _silu(conv + cb_ref[...].reshape(gs * nb, 1, LANE))

    sm = sm_ref[0]
    dt_all = _softplus(sm + dtb_ref[...])
    a_all = dt_all * (-jnp.exp(alog_ref[...]))
    ti = lax.broadcasted_iota(I32, (L, L), 0)
    si = lax.broadcasted_iota(I32, (L, L), 1)
    cum_all = _cumsum_rows(a_all, ti >= si)
    ej = lax.broadcasted_iota(I32, (LANE, gs * rb * LANE), 0)
    ec = lax.broadcasted_iota(I32, (LANE, gs * rb * LANE), 1)
    expand = ej == gi0 * (gs * 2 * rb) + jnp.right_shift(ec, 6)
    dtx = _select_dot(dt_all, expand)
    cumx = _select_dot(cum_all, expand)

    t2 = lax.broadcasted_iota(I32, (L, LANE), 0)
    l2 = lax.broadcasted_iota(I32, (L, LANE), 1)
    s2 = jnp.bitwise_and(l2, SSD_P - 1)
    diag2 = (t2 == s2).astype(F32)
    causal2 = t2 >= s2
    left = l2 < SSD_P

    for gi in range(gs):
        bm = uc[gi * nb + rb]
        cm = uc[gi * nb + rb + 1]
        b2 = jnp.concatenate([bm, bm], axis=0).astype(BF16)
        cm_b = cm.astype(BF16)
        bm_b = bm.astype(BF16)
        cb2 = lax.dot_general(cm_b, b2, (((1,), (1,)), ((), ())), preferred_element_type=F32)
        ygs = []
        ms = jnp.zeros((L, 1), F32)
        for j in range(rb):
            jj = gi * rb + j
            ccol = cumx[:, jj * LANE:(jj + 1) * LANE]
            dtc = dtx[:, jj * LANE:(jj + 1) * LANE]
            crow = jnp.sum(ccol * diag2, axis=0, keepdims=True)
            dec = jnp.exp(jnp.where(causal2, ccol - crow, NEG_BIG))
            ww = (cb2 * dec).astype(BF16)
            xb = uc[gi * nb + j]
            xdt = xb * dtc
            xbd = jnp.concatenate([jnp.where(left, xdt, 0.0), jnp.where(left, 0.0, xdt)], axis=0).astype(BF16)
            y = _dot(ww, xbd)
            sj = s_scr[jj]
            y = y + _dot(cm_b, sj.astype(BF16)) * jnp.exp(ccol)
            y = y + d_ref[jj] * xb
            cl = ccol[L - 1:L, :]
            xw = (xdt * jnp.exp(cl - ccol)).astype(BF16)
            s_scr[jj] = sj * jnp.exp(cl) + lax.dot_general(
                bm_b, xw, (((0,), (0,)), ((), ())), preferred_element_type=F32)
            yg = y * _silu(z_ref[jj])
            ms = ms + jnp.sum(yg * yg, axis=-1, keepdims=True)
            ygs.append(yg)
        inv = lax.rsqrt(ms / (rb * LANE) + EPS)
        for j in range(rb):
            jj = gi * rb + j
            y_ref[:, jj * LANE:(jj + 1) * LANE] = ((ygs[j] * inv) * nw_ref[jj]).astype(y_ref.dtype)

    @pl.when(end_ref[c] == 1)
    def _():
        sout_ref[0] = s_scr[...]


def _ssd_mixer(proj, seq_id, start, end, conv_in, state_in, conv_w, conv_b, dtb, alog, d_exp, nw,
               t, width, cb_z, cb_x, cb_b, cb_c, cb_sm):
    rb = width // LANE // SSD_G
    nb = rb + 2
    nc = t // CHUNK
    nseq = state_in.shape[0]
    L = CHUNK

    def im(f):
        return lambda g, c, s, st, en: f(g, c, s)

    gs = SSD_GS
    grb = gs * rb
    grid_spec = pltpu.PrefetchScalarGridSpec(
        num_scalar_prefetch=3,
        grid=(SSD_G // gs, nc),
        in_specs=[
            pl.BlockSpec((grb, L, LANE), im(lambda g, c, s: (cb_z // grb + g, c, 0))),
            pl.BlockSpec((grb, L, LANE), im(lambda g, c, s: (cb_x // grb + g, c, 0))),
            pl.BlockSpec((gs, L, LANE), im(lambda g, c, s: (cb_b // gs + g, c, 0))),
            pl.BlockSpec((gs, L, LANE), im(lambda g, c, s: (cb_c // gs + g, c, 0))),
            pl.BlockSpec((1, L, LANE), im(lambda g, c, s: (cb_sm, c, 0))),
            pl.BlockSpec((1, gs, nb, CONV_K - 1, LANE), im(lambda g, c, s: (s[c], g, 0, 0, 0))),
            pl.BlockSpec((1, grb, SSD_N, LANE), im(lambda g, c, s: (s[c], g, 0, 0))),
            pl.BlockSpec((gs, CONV_K, nb, 1, LANE), im(lambda g, c, s: (g, 0, 0, 0, 0))),
            pl.BlockSpec((gs, nb, 1, LANE), im(lambda g, c, s: (g, 0, 0, 0))),
            pl.BlockSpec((1, LANE), im(lambda g, c, s: (0, 0))),
            pl.BlockSpec((1, LANE), im(lambda g, c, s: (0, 0))),
            pl.BlockSpec((grb, 1, LANE), im(lambda g, c, s: (g, 0, 0))),
            pl.BlockSpec((grb, 1, LANE), im(lambda g, c, s: (g, 0, 0))),
        ],
        out_specs=[
            pl.BlockSpec((L, grb * LANE), im(lambda g, c, s: (c, g))),
            pl.BlockSpec((1, grb, SSD_N, LANE), im(lambda g, c, s: (s[c], g, 0, 0))),
        ],
        scratch_shapes=[pltpu.VMEM((gs * nb, L + SUBLANE, LANE), F32), pltpu.VMEM((grb, SSD_N, LANE), F32)],
    )
    assert cb_z % grb == 0 and cb_x % grb == 0 and cb_b % gs == 0 and cb_c % gs == 0 and SSD_G % gs == 0
    return pl.pallas_call(
        _ssd_kernel,
        grid_spec=grid_spec,
        out_shape=[jax.ShapeDtypeStruct((t, width), BF16),
                   jax.ShapeDtypeStruct((nseq, width // LANE, SSD_N, LANE), F32)],
        compiler_params=_params(("arbitrary", "arbitrary")),
        name="ssd_mixer",
    )(seq_id, start, end, proj, proj, proj, proj, proj, conv_in, state_in, conv_w, conv_b, dtb, alog, d_exp, nw)


def _bdot(a, b, ca, cb):
    return lax.dot_general(a, b, (((ca,), (cb,)), ((0,), (0,))), preferred_element_type=F32)


def _hi_lo(a):
    hi = a.astype(BF16)
    return hi, (a - hi.astype(F32)).astype(BF16)


def _bdot3_shared_rhs(lhs_list, b):
    n = len(lhs_list)
    rows = lhs_list[0].shape[1]
    pieces = [_hi_lo(a) for a in lhs_list]
    bh, bl = _hi_lo(b)
    his = [p[0] for p in pieces]
    t_hi = _bdot(jnp.concatenate(his + [p[1] for p in pieces], axis=1), bh, 2, 1)
    t_lo = _bdot(jnp.concatenate(his, axis=1), bl, 2, 1) if n > 1 else _bdot(his[0], bl, 2, 1)
    out = []
    for i in range(n):
        sl = lambda t, j: t[:, j * rows:(j + 1) * rows]
        out.append(sl(t_hi, i) + (sl(t_hi, n + i) + sl(t_lo, i)))
    return out


def _gdn_kernel(seq_ref, start_ref, end_ref,
                q_ref, k_ref, v_ref, z_ref, sm_ref, cin_ref, sin_ref, cw_ref,
                dtb_ref, alog_ref, nw_ref,
                y_ref, sout_ref,
                ext_scr, s_scr, *, lane_beta, lane_a):
    hb_i = pl.program_id(0)
    c = pl.program_id(1)
    hb = q_ref.shape[0]
    L = q_ref.shape[1]
    is_start = start_ref[c] == 1

    @pl.when(is_start)
    def _():
        s_scr[...] = sin_ref[0]

    u = jnp.concatenate([q_ref[...], k_ref[...], v_ref[...]], axis=0)
    uc = _silu(_causal_conv(u, ext_scr, lambda: cin_ref[0, 0], lambda j: cw_ref[0, j], is_start))
    q = uc[:hb]
    k = uc[hb:2 * hb]
    v = uc[2 * hb:]
    q = q * (lax.rsqrt(jnp.sum(q * q, axis=-1, keepdims=True) + EPS) * (GDN_D ** -0.5))
    k = k * lax.rsqrt(jnp.sum(k * k, axis=-1, keepdims=True) + EPS)

    sm = sm_ref[0]
    beta_all = jax.nn.sigmoid(sm)
    g_all = -jnp.exp(alog_ref[...]) * _softplus(sm + dtb_ref[...])
    ti = lax.broadcasted_iota(I32, (L, L), 0)
    si = lax.broadcasted_iota(I32, (L, L), 1)
    incl = ti >= si
    strict = ti > si
    gam_all = _cumsum_rows(g_all, incl)
    ej = lax.broadcasted_iota(I32, (LANE, hb * LANE), 0)
    ec = jnp.right_shift(lax.broadcasted_iota(I32, (LANE, hb * LANE), 1), 7) + hb_i * hb
    betax = _select_dot(beta_all, ej == ec + lane_beta)
    gamx = _select_dot(gam_all, ej == ec + lane_a)
    beta_c = jnp.stack([betax[:, h * LANE:(h + 1) * LANE] for h in range(hb)])
    gam_c = jnp.stack([gamx[:, h * LANE:(h + 1) * LANE] for h in range(hb)])

    t2 = lax.broadcasted_iota(I32, (L, LANE), 0)
    l2 = lax.broadcasted_iota(I32, (L, LANE), 1)
    diag2 = (t2 == l2).astype(F32)
    gam_r = jnp.sum(gam_c * diag2, axis=1, keepdims=True)[:, :, :L]
    gam_t = gam_c[:, :, :L]
    gam_m = jnp.exp(jnp.where(incl, gam_t - gam_r, NEG_BIG))

    kb = k.astype(BF16)
    kk = _bdot(kb, kb, 2, 2)
    a_mat = jnp.where(strict, beta_c[:, :, :L] * kk * gam_m, 0.0)
    n_pow = -a_mat
    x_inv = jnp.where(ti == si, 1.0, 0.0) + n_pow
    (n_pow,) = _bdot3_shared_rhs([n_pow], n_pow)
    span = 4
    while span < L:
        xp, n_next = _bdot3_shared_rhs([x_inv, n_pow], n_pow)
        x_inv = x_inv + xp
        n_pow = n_next
        span *= 2
    x_inv = x_inv + _bdot3_shared_rhs([x_inv], n_pow)[0]

    eg = jnp.exp(gam_c)
    rhs = jnp.concatenate([v * beta_c, k * (beta_c * eg)], axis=-1)
    (sol,) = _bdot3_shared_rhs([x_inv], rhs)
    u_ = sol[:, :, :GDN_D]
    w_ = sol[:, :, GDN_D:]
    s_prev = s_scr[...]
    s_b = s_prev.astype(BF16)
    v_new = u_ - _bdot(w_.astype(BF16), s_b, 2, 1)
    vn_b = v_new.astype(BF16)
    qk = _bdot(q.astype(BF16), kb, 2, 2) * gam_m
    o = _bdot((q * eg).astype(BF16), s_b, 2, 1) + _bdot(qk.astype(BF16), vn_b, 2, 1)
    gl = gam_c[:, L - 1:L, :]
    kt = (k * jnp.exp(gl - gam_c)).astype(BF16)
    for h in range(hb):
        upd = lax.dot_general(kt[h], vn_b[h], (((0,), (0,)), ((), ())), preferred_element_type=F32)
        s_scr[h] = s_prev[h] * jnp.exp(gl[h]) + upd

    o = o * lax.rsqrt(jnp.mean(o * o, axis=-1, keepdims=True) + EPS)
    o = (o * nw_ref[...]) * _silu(z_ref[...])
    for h in range(hb):
        y_ref[:, h * LANE:(h + 1) * LANE] = o[h].astype(y_ref.dtype)

    @pl.when(end_ref[c] == 1)
    def _():
        sout_ref[0] = s_scr[...]


def _gdn_mixer(proj, seq_id, start, end, conv_in, state_in, conv_w, dtb, alog, nw,
               t, heads, cb_q, cb_z, cb_sm, lane_beta, lane_a):
    hb = min(GDN_HB, heads)
    assert heads % hb == 0 and cb_q % hb == 0 and cb_z % hb == 0
    nhb = heads // hb
    nc = t // CHUNK
    nseq = state_in.shape[0]
    L = CHUNK

    def im(f):
        return lambda h, c, s, st, en: f(h, c, s)

    grid_spec = pltpu.PrefetchScalarGridSpec(
        num_scalar_prefetch=3,
        grid=(nhb, nc),
        in_specs=[
            pl.BlockSpec((hb, L, LANE), im(lambda h, c, s: (cb_q // hb + h, c, 0))),
            pl.BlockSpec((hb, L, LANE), im(lambda h, c, s: ((cb_q + heads) // hb + h, c, 0))),
            pl.BlockSpec((hb, L, LANE), im(lambda h, c, s: ((cb_q + 2 * heads) // hb + h, c, 0))),
            pl.BlockSpec((hb, L, LANE), im(lambda h, c, s: (cb_z // hb + h, c, 0))),
            pl.BlockSpec((1, L, LANE), im(lambda h, c, s: (cb_sm, c, 0))),
            pl.BlockSpec((1, 1, 3 * hb, CONV_K - 1, LANE), im(lambda h, c, s: (s[c], h, 0, 0, 0))),
            pl.BlockSpec((1, hb, GDN_D, GDN_D), im(lambda h, c, s: (s[c], h, 0, 0))),
            pl.BlockSpec((1, CONV_K, 3 * hb, 1, LANE), im(lambda h, c, s: (h, 0, 0, 0, 0))),
            pl.BlockSpec((1, LANE), im(lambda h, c, s: (0, 0))),
            pl.BlockSpec((1, LANE), im(lambda h, c, s: (0, 0))),
            pl.BlockSpec((1, LANE), im(lambda h, c, s: (0, 0))),
        ],
        out_specs=[
            pl.BlockSpec((L, hb * LANE), im(lambda h, c, s: (c, h))),
            pl.BlockSpec((1, hb, GDN_D, GDN_D), im(lambda h, c, s: (s[c], h, 0, 0))),
        ],
        scratch_shapes=[pltpu.VMEM((3 * hb, L + SUBLANE, LANE), F32), pltpu.VMEM((hb, GDN_D, GDN_D), F32)],
    )
    return pl.pallas_call(
        functools.partial(_gdn_kernel, lane_beta=lane_beta, lane_a=lane_a),
        grid_spec=grid_spec,
        out_shape=[jax.ShapeDtypeStruct((t, heads * GDN_D), BF16),
                   jax.ShapeDtypeStruct((nseq, heads, GDN_D, GDN_D), F32)],
        compiler_params=_params(("arbitrary", "arbitrary")),
        name="gdn_mixer",
    )(seq_id, start, end, proj, proj, proj, proj, proj, conv_in, state_in, conv_w, dtb, alog, nw)


def _outproj_kernel(a1_ref, a2_ref, w1_ref, w2_ref, xa_ref, xb_ref, o_ref, *, na):
    acc = _dot(a1_ref[...], w1_ref[...]) + _dot(a2_ref[...], w2_ref[...])

    @pl.when(pl.program_id(0) < na)
    def _():
        o_ref[...] = xa_ref[...] + acc

    @pl.when(pl.program_id(0) >= na)
    def _():
        o_ref[...] = xb_ref[...] + acc


def _outproj(a1, a2, w1, w2, xa, xb, tm, tn):
    t, k1 = a1.shape
    k2 = a2.shape[1]
    d = w1.shape[1]
    na = xa.shape[0] // tm
    spec_a, spec_b = _two_source_specs((tm, tn), na, col=d // tn - 1)
    return pl.pallas_call(
        functools.partial(_outproj_kernel, na=na),
        grid=(t // tm, d // tn),
        in_specs=[pl.BlockSpec((tm, k1), lambda i, j: (i, 0)),
                  pl.BlockSpec((tm, k2), lambda i, j: (i, 0)),
                  pl.BlockSpec((k1, tn), lambda i, j: (0, j)),
                  pl.BlockSpec((k2, tn), lambda i, j: (0, j)),
                  spec_a, spec_b],
        out_specs=pl.BlockSpec((tm, tn), lambda i, j: (i, j)),
        out_shape=jax.ShapeDtypeStruct((t, d), F32),
        compiler_params=_params(("arbitrary", "arbitrary")),
        name="outproj",
    )(a1, a2, w1, w2, xa, xb)


def _router_kernel(x_ref, nw_ref, wr_ref, br_ref, idx_ref, gate_ref):
    x = x_ref[...]
    h = (x * lax.rsqrt(jnp.mean(x * x, axis=-1, keepdims=True) + EPS)) * nw_ref[...]
    logits = _dot(h, wr_ref[...], precision=HIGHEST) + br_ref[...]
    lane = lax.broadcasted_iota(I32, logits.shape, 1)
    vals = logits
    idx_out = jnp.zeros(logits.shape, I32)
    top = []
    for kk in range(TOP_K):
        m = jnp.max(vals, axis=-1, keepdims=True)
        sel = jnp.min(jnp.where(vals == m, lane, LANE), axis=-1, keepdims=True)
        idx_out = jnp.where(lane == kk, sel, idx_out)
        top.append(m)
        vals = jnp.where(lane == sel, -jnp.inf, vals)
    es = [jnp.exp(m - top[0]) for m in top]
    den = es[0]
    for e in es[1:]:
        den = den + e
    gate_out = jnp.zeros(logits.shape, F32)
    for kk in range(TOP_K):
        gate_out = jnp.where(lane == kk, es[kk] / den, gate_out)
    idx_ref[...] = idx_out
    gate_ref[...] = gate_out


def _router(x1, nw, wr, br, tm):
    t, d = x1.shape
    return pl.pallas_call(
        _router_kernel,
        grid=(t // tm,),
        in_specs=[pl.BlockSpec((tm, d), lambda i: (i, 0)),
                  pl.BlockSpec((1, d), lambda i: (0, 0)),
                  pl.BlockSpec((d, LANE), lambda i: (0, 0)),
                  pl.BlockSpec((1, LANE), lambda i: (0, 0))],
        out_specs=[pl.BlockSpec((tm, LANE), lambda i: (i, 0)), pl.BlockSpec((tm, LANE), lambda i: (i, 0))],
        out_shape=[jax.ShapeDtypeStruct((t, LANE), I32), jax.ShapeDtypeStruct((t, LANE), F32)],
        compiler_params=_params(("parallel",)),
        name="router",
    )(x1, nw, wr, br)


def _row_copy(src_hbm, dst_vmem, sem, src_row, dst_row):
    return pltpu.make_async_copy(src_hbm.at[pl.ds(src_row, 1), :], dst_vmem.at[pl.ds(dst_row, 1), :], sem)


def _gather_rows(idx_ref, idx0, src_hbm, dst_vmem, sem):
    def body(r, carry):
        _row_copy(src_hbm, dst_vmem, sem, idx_ref[idx0 + r], r).start()
        return carry
    lax.fori_loop(0, dst_vmem.shape[0], body, 0, unroll=GATHER_UNROLL)


def _drain_rows(src_hbm, dst_vmem, sem):
    pltpu.make_async_copy(src_hbm.at[pl.ds(0, dst_vmem.shape[0]), :], dst_vmem, sem).wait()


def _dispatch_kernel(tok_ref, nused_ref, x_hbm, nw_ref, o_ref, buf, sem):
    b = pl.program_id(0)
    bm = o_ref.shape[0]
    n_used = nused_ref[0]
    slot = lax.rem(b, 2)

    @pl.when(jnp.logical_and(b == 0, n_used > 0))
    def _():
        _gather_rows(tok_ref, 0, x_hbm, buf.at[0], sem.at[0])

    @pl.when(b + 1 < n_used)
    def _():
        _gather_rows(tok_ref, (b + 1) * bm, x_hbm, buf.at[1 - slot], sem.at[1 - slot])

    @pl.when(b < n_used)
    def _():
        _drain_rows(x_hbm, buf.at[slot], sem.at[slot])

        def chunk(i, carry):
            rows = pl.ds(pl.multiple_of(i * ROW_CHUNK, ROW_CHUNK), ROW_CHUNK)
            x = buf[slot, rows, :]
            h = (x * lax.rsqrt(jnp.mean(x * x, axis=-1, keepdims=True) + EPS)) * nw_ref[...]
            o_ref[rows, :] = h.astype(o_ref.dtype)
            return carry
        lax.fori_loop(0, bm // ROW_CHUNK, chunk, 0)

    @pl.when(b >= n_used)
    def _():
        o_ref[...] = jnp.zeros(o_ref.shape, o_ref.dtype)


def _dispatch(slot_tok, n_used, x1, nw, nblk):
    t, d = x1.shape
    grid_spec = pltpu.PrefetchScalarGridSpec(
        num_scalar_prefetch=2,
        grid=(nblk,),
        in_specs=[pl.BlockSpec(memory_space=pl.ANY),
                  pl.BlockSpec((1, d), lambda b, *_: (0, 0))],
        out_specs=pl.BlockSpec((MOE_BM, d), lambda b, *_: (b, 0)),
        scratch_shapes=[pltpu.VMEM((2, MOE_BM, d), F32), pltpu.SemaphoreType.DMA((2,))],
    )
    return pl.pallas_call(
        _dispatch_kernel,
        grid_spec=grid_spec,
        out_shape=jax.ShapeDtypeStruct((nblk * MOE_BM, d), BF16),
        compiler_params=_params(("arbitrary",)),
        name="moe_dispatch",
    )(slot_tok, n_used, x1, nw)


def _resident_rows_matmul(blk0_ref, nblk_ref, nused_ref, src_hbm, dst_hbm,
                          xbuf, obuf, zbuf, sem_in, sem_out, sem_z, compute):
    s = pl.program_id(0)
    n = pl.program_id(1)
    nt = pl.num_programs(1)
    bm = MOE_BM
    tn = obuf.shape[2]
    n_blocks = nblk_ref[s]
    first = blk0_ref[s]
    lin = s * nt + n
    slot = lax.rem(lin, 2)
    col = pl.multiple_of(n * tn, tn)

    def hbm_rows(blk):
        return pl.ds(pl.multiple_of(blk * bm, bm), bm)

    def x_copy(i):
        return pltpu.make_async_copy(src_hbm.at[hbm_rows(first + i), :], xbuf.at[pl.ds(i * bm, bm), :], sem_in)

    def o_copy(slot_, blk, i):
        return pltpu.make_async_copy(obuf.at[slot_, pl.ds(i * bm, bm), :],
                                     dst_hbm.at[hbm_rows(blk), pl.ds(col, tn)], sem_out.at[slot_])

    def for_blocks(count, fn):
        for i in range(MOE_SB_BLOCKS):
            pl.when(i < count)(functools.partial(fn, i))

    def wait_out(step, slot_):
        for_blocks(nblk_ref[lax.div(step, nt)], lambda i: o_copy(slot_, 0, i).wait())

    @pl.when(lin == 0)
    def _():
        zbuf[...] = jnp.zeros(zbuf.shape, zbuf.dtype)

    @pl.when(n == 0)
    def _():
        for_blocks(n_blocks, lambda i: x_copy(i).start())
        for_blocks(n_blocks, lambda i: x_copy(i).wait())

    @pl.when(lin >= 2)
    def _():
        wait_out(lin - 2, slot)

    def matmul(blk_off, m):
        rows = pl.ds(pl.multiple_of(blk_off * bm, bm), m * bm)
        obuf[slot, rows, :] = compute(lambda: xbuf[rows, :])

    @pl.when(n_blocks == MOE_SB_BLOCKS)
    def _():
        matmul(0, MOE_SB_BLOCKS)

    @pl.when(jnp.logical_and(n_blocks > 0, n_blocks < MOE_SB_BLOCKS))
    def _():
        top = MOE_PARTIAL_HEIGHTS[0]
        n_top = lax.div(n_blocks, top)
        lax.fori_loop(0, n_top, lambda i, c: (matmul(i * top, top), c)[1], 0)
        done = n_top * top
        for m in MOE_PARTIAL_HEIGHTS[1:]:
            has = lax.rem(lax.div(n_blocks, m), 2) == 1
            pl.when(has)(functools.partial(matmul, done, m))
            done = done + jnp.where(has, m, 0)
    for_blocks(n_blocks, lambda i: o_copy(slot, first + i, i).start())

    @pl.when(lin == pl.num_programs(0) * nt - 1)
    def _():
        for_blocks(n_blocks, lambda i: o_copy(slot, 0, i).wait())

        @pl.when(lin >= 1)
        def _():
            wait_out(lin - 1, 1 - slot)

    @pl.when(s == pl.num_programs(0) - 1)
    def _():
        n_used = nused_ref[0]
        n_spare = dst_hbm.shape[0] // bm - n_used

        def z_copy(i):
            return pltpu.make_async_copy(zbuf, dst_hbm.at[hbm_rows(n_used + i), pl.ds(col, tn)], sem_z)
        lax.fori_loop(0, n_spare, lambda i, c: (z_copy(i).start(), c)[1], 0)
        lax.fori_loop(0, n_spare, lambda i, c: (z_copy(i).wait(), c)[1], 0)


def _resident_call(body, sb_e, sb_blk0, sb_nblk, n_used, src, weights, biases, dff_out, tn, out_dtype, name):
    p, k = src.shape
    nsb = sb_e.shape[0]
    nt = dff_out // tn

    def w_map(s, n, e_ref, b0_ref, nb_ref, nu_ref):
        return (e_ref[s], 0, jnp.where(nb_ref[s] > 0, n, nt - 1))

    grid_spec = pltpu.PrefetchScalarGridSpec(
        num_scalar_prefetch=4,
        grid=(nsb, nt),
        in_specs=([pl.BlockSpec(memory_space=pl.ANY)]
                  + [pl.BlockSpec((1, k, tn), w_map) for _ in weights]
                  + [pl.BlockSpec((1, 1, tn), w_map) for _ in biases]),
        out_specs=pl.BlockSpec(memory_space=pl.ANY),
        scratch_shapes=[pltpu.VMEM((MOE_SB_BLOCKS * MOE_BM, k), BF16),
                        pltpu.VMEM((2, MOE_SB_BLOCKS * MOE_BM, tn), out_dtype),
                        pltpu.VMEM((MOE_BM, tn), out_dtype),
                        pltpu.SemaphoreType.DMA(()), pltpu.SemaphoreType.DMA((2,)), pltpu.SemaphoreType.DMA(())],
    )
    return pl.pallas_call(
        body,
        grid_spec=grid_spec,
        out_shape=jax.ShapeDtypeStruct((p, dff_out), out_dtype),
        compiler_params=_params(("arbitrary", "arbitrary")),
        name=name,
    )(sb_e, sb_blk0, sb_nblk, n_used, src, *weights, *biases)


def _gateup_kernel(e_ref, blk0_ref, nblk_ref, nused_ref, x_hbm, wg_ref, wu_ref, bg_ref, bu_ref, act_hbm,
                   xbuf, obuf, zbuf, sem_in, sem_out, sem_z):
    def compute(x):
        gate = jnp.minimum(_dot(x(), wg_ref[0].astype(BF16)) + bg_ref[0], SWIGLU_LIMIT)
        up = jnp.clip(_dot(x(), wu_ref[0].astype(BF16)) + bu_ref[0], -SWIGLU_LIMIT, SWIGLU_LIMIT)
        act = gate * jax.nn.sigmoid(SWIGLU_ALPHA * gate) * (up + 1.0)
        return act.astype(obuf.dtype)

    _resident_rows_matmul(blk0_ref, nblk_ref, nused_ref, x_hbm, act_hbm,
                          xbuf, obuf, zbuf, sem_in, sem_out, sem_z, compute)


def _down_kernel(e_ref, blk0_ref, nblk_ref, nused_ref, a_hbm, wd_ref, bd_ref, y_hbm,
                 xbuf, obuf, zbuf, sem_in, sem_out, sem_z):
    def compute(a):
        return _dot(a(), wd_ref[0].astype(BF16)) + bd_ref[0]

    _resident_rows_matmul(blk0_ref, nblk_ref, nused_ref, a_hbm, y_hbm,
                          xbuf, obuf, zbuf, sem_in, sem_out, sem_z, compute)


def _combine_kernel(pos_ref, ys_hbm, x_ref, g_ref, nw_ref, o_ref, buf, sem, *, tile0):
    i = pl.program_id(0)
    tt = o_ref.shape[0]
    n = TOP_K * tt
    slot = lax.rem(i, 2)

    def issue(tile, slot_):
        _gather_rows(pos_ref, (tile0 + tile) * n, ys_hbm, buf.at[slot_], sem.at[slot_])

    @pl.when(i == 0)
    def _():
        issue(0, 0)

    @pl.when(i + 1 < pl.num_programs(0))
    def _():
        issue(i + 1, 1 - slot)

    _drain_rows(ys_hbm, buf.at[slot], sem.at[slot])

    def chunk(i, carry):
        r0 = pl.multiple_of(i * ROW_CHUNK, ROW_CHUNK)
        rows = pl.ds(r0, ROW_CHUNK)
        g = g_ref[rows, :]
        acc = x_ref[rows, :]
        for kk in range(TOP_K):
            acc = acc + buf[slot, pl.ds(kk * tt + r0, ROW_CHUNK), :] * g[:, kk:kk + 1]
        y = acc * lax.rsqrt(jnp.mean(acc * acc, axis=-1, keepdims=True) + EPS)
        o_ref[rows, :] = y * nw_ref[...]
        return carry
    lax.fori_loop(0, tt // ROW_CHUNK, chunk, 0)


def _combine(pos_tiles, ys, x1, gates, nw, tt, tile0, n_tiles):
    d = x1.shape[1]
    grid_spec = pltpu.PrefetchScalarGridSpec(
        num_scalar_prefetch=1,
        grid=(n_tiles,),
        in_specs=[pl.BlockSpec(memory_space=pl.ANY),
                  pl.BlockSpec((tt, d), lambda i, pos: (tile0 + i, 0)),
                  pl.BlockSpec((tt, LANE), lambda i, pos: (tile0 + i, 0)),
                  pl.BlockSpec((1, d), lambda i, pos: (0, 0))],
        out_specs=pl.BlockSpec((tt, d), lambda i, pos: (i, 0)),
        scratch_shapes=[pltpu.VMEM((2, TOP_K * tt, d), F32), pltpu.SemaphoreType.DMA((2,))],
    )
    return pl.pallas_call(
        functools.partial(_combine_kernel, tile0=tile0),
        grid_spec=grid_spec,
        out_shape=jax.ShapeDtypeStruct((n_tiles * tt, d), F32),
        compiler_params=_params(("arbitrary",)),
        name="moe_combine",
    )(pos_tiles, ys, x1, gates, nw)


def _pad_lanes(v, offset):
    out = jnp.zeros((LANE,), F32)
    return lax.dynamic_update_slice(out, v.astype(F32), (offset,)).reshape(1, LANE)


def _ssd_group_layout(a, width):
    lead = a.shape[:-2]
    rows = a.shape[-2]
    rb = width // LANE // SSD_G
    xs = a[..., :width].reshape(*lead, rows, SSD_G, rb, LANE)
    bs = a[..., width:width + SSD_G * SSD_N].reshape(*lead, rows, SSD_G, 1, LANE)
    cs = a[..., width + SSD_G * SSD_N:].reshape(*lead, rows, SSD_G, 1, LANE)
    cat = jnp.concatenate([xs, bs, cs], axis=-2)
    n = cat.ndim
    return jnp.moveaxis(cat, n - 4, n - 2)


def _gdn_block_layout(a, heads, hb):
    lead = a.shape[:-2]
    rows = a.shape[-2]
    r = a.reshape(*lead, rows, 3, heads // hb, hb, LANE)
    n = r.ndim
    r = jnp.moveaxis(r, n - 5, n - 2)
    r = jnp.moveaxis(r, n - 5, n - 4)
    return r.reshape(*lead, heads // hb, 3 * hb, rows, LANE)


def _pick_tn_cb(ncb):
    return INPROJ_TN_CB, (-ncb) % INPROJ_TN_CB


def kernel(x_prompt, x_sample, state_ssd_conv, state_ssd, state_gdn_conv, state_gdn, norm_mix, w_in, ssd_conv_w,
           ssd_conv_b, ssd_dt_bias, ssd_A_log, ssd_D, ssd_norm, gdn_conv_w, gdn_dt_bias, gdn_A_log, gdn_norm, w_out,
           norm_ffn, w_router, b_router, w_gate, b_gate, w_up, b_up, w_down, b_down, norm_final):
    assert w_in.shape[0] == 1, "single layer"
    nb_p, seq_p, d = x_prompt.shape
    nb_s, seq_s, _ = x_sample.shape
    assert seq_p % CHUNK == 0 and seq_s % CHUNK == 0
    ssd_heads = d // SSD_P
    ssd_w = ssd_heads * SSD_P
    ssd_cs = ssd_w + 2 * SSD_G * SSD_N
    gdn_heads = d // GDN_D
    gdn_w = gdn_heads * GDN_D
    t_p = nb_p * seq_p
    t = t_p + nb_s * seq_s
    nseq = nb_p + nb_s

    x_p = x_prompt.reshape(t_p, d)
    x_s = x_sample.reshape(nb_s * seq_s, d)
    seq_len = [seq_p] * nb_p + [seq_s] * nb_s
    seq_id, start, end = [], [], []
    for s, n in enumerate(seq_len):
        for cidx in range(n // CHUNK):
            seq_id.append(s)
            start.append(int(cidx == 0))
            end.append(int(cidx == n // CHUNK - 1))
    seq_id = jnp.asarray(np.array(seq_id, np.int32))
    start = jnp.asarray(np.array(start, np.int32))
    end = jnp.asarray(np.array(end, np.int32))

    o_z, o_xbc, o_dt = 0, ssd_w, ssd_w + ssd_cs
    o_qkv = o_dt + ssd_heads
    o_zg = o_qkv + 3 * gdn_w
    o_b = o_zg + gdn_w
    o_a = o_b + gdn_heads
    n_small = ssd_heads + 2 * gdn_heads
    assert n_small <= LANE
    cb_q = 0
    cb_zg = cb_q + 3 * gdn_w // LANE
    cb_z = cb_zg + gdn_w // LANE
    cb_x = cb_z + ssd_w // LANE
    cb_sm = cb_x + ssd_cs // LANE
    ncb = cb_sm + 1
    tn_cb, pad_cb = _pick_tn_cb(ncb)
    assert o_zg == o_qkv + 3 * gdn_w and o_xbc == o_z + ssd_w and o_a == o_b + gdn_heads
    w_perm = _permute_cast_weight(w_in[0], o_qkv, 4 * gdn_w, ssd_w + ssd_cs, o_dt, o_b, ssd_heads, n_small,
                                  (ncb + pad_cb) * LANE)
    lane_beta = ssd_heads
    lane_a = ssd_heads + gdn_heads

    tm_big = _row_tile(t, 1056)
    tm_mid = _row_tile(t, 528)
    tm_src = _row_tile(int(np.gcd(t_p, t - t_p)), 512)
    h = _rmsnorm_cast(x_p, x_s, norm_mix[0], tm_src)
    proj = _inproj(h, w_perm, tm_big, tn_cb)

    def with_zero_prompt(a):
        return jnp.concatenate([jnp.zeros((nb_p,) + a.shape[1:], a.dtype), a], axis=0)

    ssd_conv0 = _ssd_group_layout(with_zero_prompt(state_ssd_conv[0]), ssd_w)
    gdn_hb = min(GDN_HB, gdn_heads)
    gdn_conv0 = _gdn_block_layout(with_zero_prompt(state_gdn_conv[0]), gdn_heads, gdn_hb)
    s0 = with_zero_prompt(state_ssd[0])
    ssd_s0 = s0.reshape(nseq, ssd_heads // 2, 2, SSD_P, SSD_N).transpose(0, 1, 4, 2, 3).reshape(
        nseq, ssd_heads // 2, SSD_N, LANE)
    gdn_s0 = with_zero_prompt(state_gdn[0])

    ssd_cw = _ssd_group_layout(ssd_conv_w[0][None], ssd_w)[0]
    ssd_cw = jnp.swapaxes(ssd_cw, 1, 2)[:, :, :, None, :]
    ssd_cb = _ssd_group_layout(ssd_conv_b[0][None, None], ssd_w)[0]
    d_exp = jnp.repeat(ssd_D[0], SSD_P).reshape(ssd_w // LANE, 1, LANE)
    ssd_nw = ssd_norm[0].reshape(ssd_w // LANE, 1, LANE)
    y_ssd, ssd_s = _ssd_mixer(
        proj, seq_id, start, end, ssd_conv0, ssd_s0, ssd_cw, ssd_cb,
        _pad_lanes(ssd_dt_bias[0], 0), _pad_lanes(ssd_A_log[0], 0), d_exp, ssd_nw,
        t, ssd_w, cb_z, cb_x, cb_x + ssd_w // LANE, cb_x + ssd_w // LANE + SSD_G, cb_sm)

    gdn_cw = _gdn_block_layout(gdn_conv_w[0][None], gdn_heads, gdn_hb)[0]
    gdn_cw = jnp.swapaxes(gdn_cw, 1, 2)[:, :, :, None, :]
    y_gdn, gdn_s = _gdn_mixer(
        proj, seq_id, start, end, gdn_conv0, gdn_s0, gdn_cw,
        _pad_lanes(gdn_dt_bias[0], lane_a), _pad_lanes(gdn_A_log[0], lane_a), gdn_norm[0].reshape(1, LANE),
        t, gdn_heads, cb_q, cb_zg, cb_sm, lane_beta, lane_a)

    w_o = w_out[0].astype(BF16)
    x1 = _outproj(y_ssd, y_gdn, w_o[:ssd_w], w_o[ssd_w:], x_p, x_s, tm_src, min(512, d))

    wr = jnp.concatenate([w_router[0], jnp.zeros((d, LANE - N_EXPERTS), F32)], axis=1)
    br = jnp.concatenate([b_router[0], jnp.full((LANE - N_EXPERTS,), NEG_BIG, F32)]).reshape(1, LANE)
    nffn = norm_ffn[0].reshape(1, d)
    idx_pad, gate_pad = _router(x1, nffn, wr, br, tm_mid)
    top_idx = idx_pad[:, :TOP_K]
    tk = t * TOP_K
    flat_e = top_idx.reshape(tk)
    order = jnp.argsort(flat_e).astype(I32)
    sorted_e = flat_e[order]
    counts = jnp.sum((flat_e[:, None] == jnp.arange(N_EXPERTS, dtype=I32)[None, :]).astype(I32), axis=0)
    starts = jnp.cumsum(counts) - counts
    pcounts = (counts + MOE_BM - 1) // MOE_BM * MOE_BM
    pends = jnp.cumsum(pcounts)
    pstarts = pends - pcounts
    dest = (pstarts[sorted_e] + (jnp.arange(tk, dtype=I32) - starts[sorted_e])).astype(I32)
    nblk = -(-tk // MOE_BM) + N_EXPERTS
    pos = jnp.zeros((tk,), I32).at[order].set(dest)
    def count_le(ends, v):
        return jnp.sum((ends[None, :] <= v[:, None]).astype(I32), axis=1)

    blk_e = jnp.minimum(count_le(pends, jnp.arange(nblk, dtype=I32) * MOE_BM), N_EXPERTS - 1).astype(I32)
    n_used = (pends[-1] // MOE_BM).astype(I32).reshape(1)
    blk0 = (pstarts // MOE_BM).astype(I32)
    nblk_e = (pcounts // MOE_BM).astype(I32)
    slot = jnp.arange(nblk * MOE_BM, dtype=I32)
    slot_e = jnp.repeat(blk_e, MOE_BM)
    slot_off = slot - pstarts[slot_e].astype(I32)
    slot_src = jnp.clip(starts[slot_e].astype(I32) + slot_off, 0, tk - 1)
    slot_tok = jnp.where(slot_off < counts[slot_e], (order // TOP_K)[slot_src], 0).astype(I32)
    nsb_e = (nblk_e + MOE_SB_BLOCKS - 1) // MOE_SB_BLOCKS
    sb_ends = jnp.cumsum(nsb_e)
    sb_starts = sb_ends - nsb_e
    n_sb = sb_ends[-1]
    sb_i = jnp.arange(N_EXPERTS + nblk // MOE_SB_BLOCKS, dtype=I32)
    sb_c = jnp.minimum(sb_i, n_sb - 1)
    sb_e = jnp.minimum(count_le(sb_ends, sb_c), N_EXPERTS - 1).astype(I32)
    sb_j = sb_c - sb_starts[sb_e]
    sb_blk0 = (blk0[sb_e] + sb_j * MOE_SB_BLOCKS).astype(I32)
    sb_nblk = jnp.where(sb_i < n_sb, jnp.clip(nblk_e[sb_e] - sb_j * MOE_SB_BLOCKS, 0, MOE_SB_BLOCKS), 0).astype(I32)

    xs = _dispatch(slot_tok, n_used, x1, nffn, nblk)
    act = _resident_call(_gateup_kernel, sb_e, sb_blk0, sb_nblk, n_used, xs, (w_gate[0], w_up[0]),
                         (b_gate[0][:, None, :], b_up[0][:, None, :]), w_gate.shape[3], min(MOE_TN, d), BF16,
                         "moe_gateup")
    ys = _resident_call(_down_kernel, sb_e, sb_blk0, sb_nblk, n_used, act, (w_down[0],),
                        (b_down[0][:, None, :],), d, min(MOE_TN_DOWN, d), F32, "moe_down")

    tt = CHUNK
    pos_tiles = pos.reshape(t // tt, tt, TOP_K).transpose(0, 2, 1).reshape(tk)
    nfin = norm_final.reshape(1, d)
    y_prompt = _combine(pos_tiles, ys, x1, gate_pad, nfin, tt, 0, t_p // tt).reshape(nb_p, seq_p, d)
    y_sample = _combine(pos_tiles, ys, x1, gate_pad, nfin, tt, t_p // tt, (t - t_p) // tt).reshape(nb_s, seq_s, d)

    def last_rows(cb0, ncols):
        nblk_c = ncols // LANE
        ends = np.cumsum(seq_len)
        rows = jnp.concatenate(
            [lax.slice(proj, (cb0, int(e) - (CONV_K - 1), 0), (cb0 + nblk_c, int(e), LANE)) for e in ends], axis=1)
        a = rows.reshape(nblk_c, nseq, CONV_K - 1, LANE).transpose(1, 2, 0, 3).reshape(nseq, CONV_K - 1, ncols)
        return a[:nb_p][None], a[nb_p:][None]

    ssd_conv_p, ssd_conv_s = last_rows(cb_x, ssd_cs)
    gdn_conv_p, gdn_conv_s = last_rows(cb_q, 3 * gdn_w)
    ssd_state = ssd_s.reshape(nseq, ssd_heads // 2, SSD_N, 2, SSD_P).transpose(0, 1, 3, 4, 2).reshape(
        nseq, ssd_heads, SSD_P, SSD_N)
    return (y_prompt, y_sample,
            ssd_conv_p, ssd_state[:nb_p][None], gdn_conv_p, gdn_s[:nb_p][None],
            ssd_conv_s, ssd_state[nb_p:][None], gdn_conv_s, gdn_s[nb_p:][None])
```

```python
import functools

import numpy as np
import jax
import jax.numpy as jnp
from jax import lax
from jax.experimental import pallas as pl
from jax.experimental.pallas import tpu as pltpu

F32 = jnp.float32
BF16 = jnp.bfloat16
I32 = jnp.int32

LANE = 128
SUBLANE = 8
VMEM_LIMIT = 60 * 1024 * 1024

CHUNK = 64
CONV_K = 4
SSD_P = 64
SSD_N = 128
SSD_G = 8
GDN_D = 128
N_EXPERTS = 32
TOP_K = 4
SWIGLU_LIMIT = 7.0
SWIGLU_ALPHA = 1.702
EPS = 1e-6
NEG_BIG = -1e30

MOE_BM = 256
MOE_SB_BLOCKS = 9
MOE_PARTIAL_HEIGHTS = (4, 2, 1)
MOE_TN = 256
MOE_TN_DOWN = 512
GATHER_UNROLL = 8
INPROJ_TN_CB = 10
GDN_HB = 32
SSD_GS = 8
HIGHEST = lax.Precision.HIGHEST


def _row_tile(n, target, mult=16):
    best = None
    for t in range(mult, min(n, target) + 1, mult):
        if n % t == 0:
            best = t
    assert best is not None, (n, target)
    return best


def _params(sem):
    return pltpu.CompilerParams(dimension_semantics=sem, vmem_limit_bytes=VMEM_LIMIT)


def _silu(x):
    return x * jax.nn.sigmoid(x)


def _softplus(x):
    return jnp.maximum(x, 0.0) + jnp.log1p(jnp.exp(-jnp.abs(x)))


def _dot(a, b, **kw):
    return jnp.dot(a, b, preferred_element_type=F32, **kw)


def _split3(a):
    hi = a.astype(BF16)
    r = a - hi.astype(F32)
    mid = r.astype(BF16)
    lo = (r - mid.astype(F32)).astype(BF16)
    return hi, mid, lo


def _two_source_specs(block, na, col=None):
    if col is None:
        ia = lambda i, *_: (jnp.minimum(i, na - 1), 0)
        ib = lambda i, *_: (jnp.maximum(i - na, 0), 0)
    else:
        ia = lambda i, j, *_: (jnp.minimum(i, na - 1), jnp.where(i < na, j, col))
        ib = lambda i, j, *_: (jnp.maximum(i - na, 0), jnp.where(i >= na, j, 0))
    return pl.BlockSpec(block, ia), pl.BlockSpec(block, ib)


def _rmsnorm_cast_kernel(xa_ref, xb_ref, w_ref, o_ref, *, na):
    def body(x_ref):
        x = x_ref[...]
        xn = x * lax.rsqrt(jnp.mean(x * x, axis=-1, keepdims=True) + EPS)
        o_ref[...] = (xn * w_ref[...]).astype(o_ref.dtype)

    pl.when(pl.program_id(0) < na)(lambda: body(xa_ref))
    pl.when(pl.program_id(0) >= na)(lambda: body(xb_ref))


def _rmsnorm_cast(xa, xb, w, tm):
    ta, d = xa.shape
    tb = xb.shape[0]
    na = ta // tm
    spec_a, spec_b = _two_source_specs((tm, d), na)
    return pl.pallas_call(
        functools.partial(_rmsnorm_cast_kernel, na=na),
        grid=(na + tb // tm,),
        in_specs=[spec_a, spec_b, pl.BlockSpec((1, d), lambda i: (0, 0))],
        out_specs=pl.BlockSpec((tm, d), lambda i: (i, 0)),
        out_shape=jax.ShapeDtypeStruct((ta + tb, d), BF16),
        compiler_params=_params(("arbitrary",)),
        name="rmsnorm_cast",
    )(xa, xb, w.reshape(1, d))


def _wperm_kernel(main_ref, right_ref, dt_ref, ba_ref, o_ref, *, t_b, t_a, shift, n_dt, n_small):
    j = pl.program_id(0)
    d, ct = o_ref.shape
    rc = 64

    def by_rows(fn):
        def body(i, carry):
            rows = pl.ds(pl.multiple_of(i * rc, rc), rc)
            o_ref[rows, :] = fn(rows).astype(o_ref.dtype)
            return carry
        lax.fori_loop(0, d // rc, body, 0)

    @pl.when(j < t_b)
    def _():
        by_rows(lambda rows: jnp.concatenate([main_ref[rows, :], right_ref[rows, :]], axis=1)[:, shift:shift + ct])

    @pl.when(jnp.logical_and(j >= t_b, j < t_b + t_a))
    def _():
        by_rows(lambda rows: main_ref[rows, :])

    @pl.when(j == t_b + t_a)
    def _():
        def small(rows):
            lane = lax.broadcasted_iota(I32, (rc, LANE), 1)
            blk = jnp.where(lane < n_dt, dt_ref[rows, :], jnp.where(lane < n_small, ba_ref[rows, :], 0.0))
            return jnp.concatenate([blk, jnp.zeros((rc, ct - LANE), F32)], axis=1)
        by_rows(small)

    @pl.when(j > t_b + t_a)
    def _():
        o_ref[...] = jnp.zeros(o_ref.shape, o_ref.dtype)


def _permute_cast_weight(w, o_b_part, w_b_part, w_a_part, o_dt, o_ba, n_dt, n_small, n_out):
    d = w.shape[0]
    ct = 2 * LANE
    shift = o_b_part % LANE
    base = o_b_part - shift
    assert base % ct == 0 and w_b_part % ct == 0 and w_a_part % ct == 0 and n_out % ct == 0
    assert o_dt % LANE == 0 and o_ba % LANE == n_dt and n_out >= w_b_part + w_a_part + ct
    t_b, t_a = w_b_part // ct, w_a_part // ct

    def main_map(j):
        return (0, jnp.where(j < t_b, base // ct + j, jnp.where(j < t_b + t_a, j - t_b, 0)))

    return pl.pallas_call(
        functools.partial(_wperm_kernel, t_b=t_b, t_a=t_a, shift=shift, n_dt=n_dt, n_small=n_small),
        grid=(n_out // ct,),
        in_specs=[pl.BlockSpec((d, ct), main_map),
                  pl.BlockSpec((d, LANE), lambda j: (0, base // LANE + 2 * (jnp.minimum(j, t_b - 1) + 1))),
                  pl.BlockSpec((d, LANE), lambda j: (0, o_dt // LANE)),
                  pl.BlockSpec((d, LANE), lambda j: (0, o_ba // LANE))],
        out_specs=pl.BlockSpec((d, ct), lambda j: (0, j)),
        out_shape=jax.ShapeDtypeStruct((d, n_out), BF16),
        compiler_params=_params(("arbitrary",)),
        name="inproj_weight_layout",
    )(w, w, w, w)


def _inproj_kernel(a_ref, w_ref, o_ref):
    ncb = o_ref.shape[0]
    step = 2 if ncb % 2 == 0 else 1
    for j in range(0, ncb, step):
        acc = _dot(a_ref[...], w_ref[:, j * LANE:(j + step) * LANE])
        for s in range(step):
            o_ref[j + s] = acc[:, s * LANE:(s + 1) * LANE]


def _inproj(a, w, tm, tn_cb):
    t, d = a.shape
    n = w.shape[1]
    ncb = n // LANE
    return pl.pallas_call(
        _inproj_kernel,
        grid=(t // tm, ncb // tn_cb),
        in_specs=[pl.BlockSpec((tm, d), lambda i, j: (i, 0)),
                  pl.BlockSpec((d, tn_cb * LANE), lambda i, j: (0, j))],
        out_specs=pl.BlockSpec((tn_cb, tm, LANE), lambda i, j: (j, i, 0)),
        out_shape=jax.ShapeDtypeStruct((ncb, t, LANE), F32),
        compiler_params=_params(("arbitrary", "arbitrary")),
        name="inproj",
    )(a, w)


def _causal_conv(u, ext_scr, conv_in, taps, is_start):
    L = u.shape[1]
    base = SUBLANE - (CONV_K - 1)

    @pl.when(is_start)
    def _():
        ext_scr[:, base:SUBLANE, :] = conv_in()

    ext_scr[:, SUBLANE:SUBLANE + L, :] = u
    acc = ext_scr[:, base:base + L, :] * taps(0)
    for j in range(1, CONV_K):
        acc = acc + ext_scr[:, base + j:base + j + L, :] * taps(j)
    ext_scr[:, base:SUBLANE, :] = ext_scr[:, base + L:SUBLANE + L, :]
    return acc


def _cumsum_rows(a, incl):
    m = incl.astype(BF16)
    hi, mid, lo = _split3(a)
    return _dot(m, hi) + (_dot(m, mid) + _dot(m, lo))


def _select_dot(a, onehot):
    oh = onehot.astype(BF16)
    hi, mid, lo = _split3(a)
    return _dot(hi, oh) + (_dot(mid, oh) + _dot(lo, oh))


def _ssd_kernel(seq_ref, start_ref, end_ref,
                z_ref, x_ref, b_ref, c_ref, sm_ref, cin_ref, sin_ref, cw_ref, cb_ref,
                dtb_ref, alog_ref, d_ref, nw_ref,
                y_ref, sout_ref,
                ext_scr, s_scr):
    gi0 = pl.program_id(0)
    c = pl.program_id(1)
    gs = b_ref.shape[0]
    rb = x_ref.shape[0] // gs
    nb = rb + 2
    L = x_ref.shape[1]
    is_start = start_ref[c] == 1

    @pl.when(is_start)
    def _():
        s_scr[...] = sin_ref[0]

    u = jnp.concatenate(
        [p for gi in range(gs) for p in (x_ref[gi * rb:(gi + 1) * rb], b_ref[gi:gi + 1], c_ref[gi:gi + 1])], axis=0)
    conv = _causal_conv(u, ext_scr, lambda: cin_ref[0].reshape(gs * nb, CONV_K - 1, LANE),
                        lambda j: cw_ref[:, j].reshape(gs * nb, 1, LANE), is_start)
    uc = _silu(conv + cb_ref[...].reshape(gs * nb, 1, LANE))

    sm = sm_ref[0]
    dt_all = _softplus(sm + dtb_ref[...])
    a_all = dt_all * (-jnp.exp(alog_ref[...]))
    ti = lax.broadcasted_iota(I32, (L, L), 0)
    si = lax.broadcasted_iota(I32, (L, L), 1)
    cum_all = _cumsum_rows(a_all, ti >= si)
    ej = lax.broadcasted_iota(I32, (LANE, gs * rb * LANE), 0)
    ec = lax.broadcasted_iota(I32, (LANE, gs * rb * LANE), 1)
    expand = ej == gi0 * (gs * 2 * rb) + jnp.right_shift(ec, 6)
    dtx = _select_dot(dt_all, expand)
    cumx = _select_dot(cum_all, expand)

    t2 = lax.broadcasted_iota(I32, (L, LANE), 0)
    l2 = lax.broadcasted_iota(I32, (L, LANE), 1)
    s2 = jnp.bitwise_and(l2, SSD_P - 1)
    diag2 = (t2 == s2).astype(F32)
    causal2 = t2 >= s2
    left = l2 < SSD_P

    for gi in range(gs):
        bm = uc[gi * nb + rb]
        cm = uc[gi * nb + rb + 1]
        b2 = jnp.concatenate([bm, bm], axis=0).astype(BF16)
        cm_b = cm.astype(BF16)
        bm_b = bm.astype(BF16)
        cb2 = lax.dot_general(cm_b, b2, (((1,), (1,)), ((), ())), preferred_element_type=F32)
        ygs = []
        ms = jnp.zeros((L, 1), F32)
        for j in range(rb):
            jj = gi * rb + j
            ccol = cumx[:, jj * LANE:(jj + 1) * LANE]
            dtc = dtx[:, jj * LANE:(jj + 1) * LANE]
            crow = jnp.sum(ccol * diag2, axis=0, keepdims=True)
            dec = jnp.exp(jnp.where(causal2, ccol - crow, NEG_BIG))
            ww = (cb2 * dec).astype(BF16)
            xb = uc[gi * nb + j]
            xdt = xb * dtc
            xbd = jnp.concatenate([jnp.where(left, xdt, 0.0), jnp.where(left, 0.0, xdt)], axis=0).astype(BF16)
            y = _dot(ww, xbd)
            sj = s_scr[jj]
            y = y + _dot(cm_b, sj.astype(BF16)) * jnp.exp(ccol)
            y = y + d_ref[jj] * xb
            cl = ccol[L - 1:L, :]
            xw = (xdt * jnp.exp(cl - ccol)).astype(BF16)
            s_scr[jj] = sj * jnp.exp(cl) + lax.dot_general(
                bm_b, xw, (((0,), (0,)), ((), ())), preferred_element_type=F32)
            yg = y * _silu(z_ref[jj])
            ms = ms + jnp.sum(yg * yg, axis=-1, keepdims=True)
            ygs.append(yg)
        inv = lax.rsqrt(ms / (rb * LANE) + EPS)
        for j in range(rb):
            jj = gi * rb + j
            y_ref[:, jj * LANE:(jj + 1) * LANE] = ((ygs[j] * inv) * nw_ref[jj]).astype(y_ref.dtype)

    @pl.when(end_ref[c] == 1)
    def _():
        sout_ref[0] = s_scr[...]


def _ssd_mixer(proj, seq_id, start, end, conv_in, state_in, conv_w, conv_b, dtb, alog, d_exp, nw,
               t, width, cb_z, cb_x, cb_b, cb_c, cb_sm):
    rb = width // LANE // SSD_G
    nb = rb + 2
    nc = t // CHUNK
    nseq = state_in.shape[0]
    L = CHUNK

    def im(f):
        return lambda g, c, s, st, en: f(g, c, s)

    gs = SSD_GS
    grb = gs * rb
    grid_spec = pltpu.PrefetchScalarGridSpec(
        num_scalar_prefetch=3,
        grid=(SSD_G // gs, nc),
        in_specs=[
            pl.BlockSpec((grb, L, LANE), im(lambda g, c, s: (cb_z // grb + g, c, 0))),
            pl.BlockSpec((grb, L, LANE), im(lambda g, c, s: (cb_x // grb + g, c, 0))),
            pl.BlockSpec((gs, L, LANE), im(lambda g, c, s: (cb_b // gs + g, c, 0))),
            pl.BlockSpec((gs, L, LANE), im(lambda g, c, s: (cb_c // gs + g, c, 0))),
            pl.BlockSpec((1, L, LANE), im(lambda g, c, s: (cb_sm, c, 0))),
            pl.BlockSpec((1, gs, nb, CONV_K - 1, LANE), im(lambda g, c, s: (s[c], g, 0, 0, 0))),
            pl.BlockSpec((1, grb, SSD_N, LANE), im(lambda g, c, s: (s[c], g, 0, 0))),
            pl.BlockSpec((gs, CONV_K, nb, 1, LANE), im(lambda g, c, s: (g, 0, 0, 0, 0))),
            pl.BlockSpec((gs, nb, 1, LANE), im(lambda g, c, s: (g, 0, 0, 0))),
            pl.BlockSpec((1, LANE), im(lambda g, c, s: (0, 0))),
            pl.BlockSpec((1, LANE), im(lambda g, c, s: (0, 0))),
            pl.BlockSpec((grb, 1, LANE), im(lambda g, c, s: (g, 0, 0))),
            pl.BlockSpec((grb, 1, LANE), im(lambda g, c, s: (g, 0, 0))),
        ],
        out_specs=[
            pl.BlockSpec((L, grb * LANE), im(lambda g, c, s: (c, g))),
            pl.BlockSpec((1, grb, SSD_N, LANE), im(lambda g, c, s: (s[c], g, 0, 0))),
        ],
        scratch_shapes=[pltpu.VMEM((gs * nb, L + SUBLANE, LANE), F32), pltpu.VMEM((grb, SSD_N, LANE), F32)],
    )
    assert cb_z % grb == 0 and cb_x % grb == 0 and cb_b % gs == 0 and cb_c % gs == 0 and SSD_G % gs == 0
    return pl.pallas_call(
        _ssd_kernel,
        grid_spec=grid_spec,
        out_shape=[jax.ShapeDtypeStruct((t, width), BF16),
                   jax.ShapeDtypeStruct((nseq, width // LANE, SSD_N, LANE), F32)],
        compiler_params=_params(("arbitrary", "arbitrary")),
        name="ssd_mixer",
    )(seq_id, start, end, proj, proj, proj, proj, proj, conv_in, state_in, conv_w, conv_b, dtb, alog, d_exp, nw)


def _bdot(a, b, ca, cb):
    return lax.dot_general(a, b, (((ca,), (cb,)), ((0,), (0,))), preferred_element_type=F32)


def _hi_lo(a):
    hi = a.astype(BF16)
    return hi, (a - hi.astype(F32)).astype(BF16)


def _bdot3_shared_rhs(lhs_list, b):
    n = len(lhs_list)
    rows = lhs_list[0].shape[1]
    pieces = [_hi_lo(a) for a in lhs_list]
    bh, bl = _hi_lo(b)
    his = [p[0] for p in pieces]
    t_hi = _bdot(jnp.concatenate(his + [p[1] for p in pieces], axis=1), bh, 2, 1)
    t_lo = _bdot(jnp.concatenate(his, axis=1), bl, 2, 1) if n > 1 else _bdot(his[0], bl, 2, 1)
    out = []
    for i in range(n):
        sl = lambda t, j: t[:, j * rows:(j + 1) * rows]
        out.append(sl(t_hi, i) + (sl(t_hi, n + i) + sl(t_lo, i)))
    return out


def _gdn_kernel(seq_ref, start_ref, end_ref,
                q_ref, k_ref, v_ref, z_ref, sm_ref, cin_ref, sin_ref, cw_ref,
                dtb_ref, alog_ref, nw_ref,
                y_ref, sout_ref,
                ext_scr, s_scr, *, lane_beta, lane_a):
    hb_i = pl.program_id(0)
    c = pl.program_id(1)
    hb = q_ref.shape[0]
    L = q_ref.shape[1]
    is_start = start_ref[c] == 1

    @pl.when(is_start)
    def _():
        s_scr[...] = sin_ref[0]

    u = jnp.concatenate([q_ref[...], k_ref[...], v_ref[...]], axis=0)
    uc = _silu(_causal_conv(u, ext_scr, lambda: cin_ref[0, 0], lambda j: cw_ref[0, j], is_start))
    q = uc[:hb]
    k = uc[hb:2 * hb]
    v = uc[2 * hb:]
    q = q * (lax.rsqrt(jnp.sum(q * q, axis=-1, keepdims=True) + EPS) * (GDN_D ** -0.5))
    k = k * lax.rsqrt(jnp.sum(k * k, axis=-1, keepdims=True) + EPS)

    sm = sm_ref[0]
    beta_all = jax.nn.sigmoid(sm)
    g_all = -jnp.exp(alog_ref[...]) * _softplus(sm + dtb_ref[...])
    ti = lax.broadcasted_iota(I32, (L, L), 0)
    si = lax.broadcasted_iota(I32, (L, L), 1)
    incl = ti >= si
    strict = ti > si
    gam_all = _cumsum_rows(g_all, incl)
    ej = lax.broadcasted_iota(I32, (LANE, hb * LANE), 0)
    ec = jnp.right_shift(lax.broadcasted_iota(I32, (LANE, hb * LANE), 1), 7) + hb_i * hb
    betax = _select_dot(beta_all, ej == ec + lane_beta)
    gamx = _select_dot(gam_all, ej == ec + lane_a)
    beta_c = jnp.stack([betax[:, h * LANE:(h + 1) * LANE] for h in range(hb)])
    gam_c = jnp.stack([gamx[:, h * LANE:(h + 1) * LANE] for h in range(hb)])

    t2 = lax.broadcasted_iota(I32, (L, LANE), 0)
    l2 = lax.broadcasted_iota(I32, (L, LANE), 1)
    diag2 = (t2 == l2).astype(F32)
    gam_r = jnp.sum(gam_c * diag2, axis=1, keepdims=True)[:, :, :L]
    gam_t = gam_c[:, :, :L]
    gam_m = jnp.exp(jnp.where(incl, gam_t - gam_r, NEG_BIG))

    kb = k.astype(BF16)
    kk = _bdot(kb, kb, 2, 2)
    a_mat = jnp.where(strict, beta_c[:, :, :L] * kk * gam_m, 0.0)
    n_pow = -a_mat
    x_inv = jnp.where(ti == si, 1.0, 0.0) + n_pow
    (n_pow,) = _bdot3_shared_rhs([n_pow], n_pow)
    span = 4
    while span < L:
        xp, n_next = _bdot3_shared_rhs([x_inv, n_pow], n_pow)
        x_inv = x_inv + xp
        n_pow = n_next
        span *= 2
    x_inv = x_inv + _bdot3_shared_rhs([x_inv], n_pow)[0]

    eg = jnp.exp(gam_c)
    rhs = jnp.concatenate([v * beta_c, k * (beta_c * eg)], axis=-1)
    (sol,) = _bdot3_shared_rhs([x_inv], rhs)
    u_ = sol[:, :, :GDN_D]
    w_ = sol[:, :, GDN_D:]
    s_prev = s_scr[...]
    s_b = s_prev.astype(BF16)
    v_new = u_ - _bdot(w_.astype(BF16), s_b, 2, 1)
    vn_b = v_new.astype(BF16)
    qk = _bdot(q.astype(BF16), kb, 2, 2) * gam_m
    o = _bdot((q * eg).astype(BF16), s_b, 2, 1) + _bdot(qk.astype(BF16), vn_b, 2, 1)
    gl = gam_c[:, L - 1:L, :]
    kt = (k * jnp.exp(gl - gam_c)).astype(BF16)
    for h in range(hb):
        upd = lax.dot_general(kt[h], vn_b[h], (((0,), (0,)), ((), ())), preferred_element_type=F32)
        s_scr[h] = s_prev[h] * jnp.exp(gl[h]) + upd

    o = o * lax.rsqrt(jnp.mean(o * o, axis=-1, keepdims=True) + EPS)
    o = (o * nw_ref[...]) * _silu(z_ref[...])
    for h in range(hb):
        y_ref[:, h * LANE:(h + 1) * LANE] = o[h].astype(y_ref.dtype)

    @pl.when(end_ref[c] == 1)
    def _():
        sout_ref[0] = s_scr[...]


def _gdn_mixer(proj, seq_id, start, end, conv_in, state_in, conv_w, dtb, alog, nw,
               t, heads, cb_q, cb_z, cb_sm, lane_beta, lane_a):
    hb = min(GDN_HB, heads)
    assert heads % hb == 0 and cb_q % hb == 0 and cb_z % hb == 0
    nhb = heads // hb
    nc = t // CHUNK
    nseq = state_in.shape[0]
    L = CHUNK

    def im(f):
        return lambda h, c, s, st, en: f(h, c, s)

    grid_spec = pltpu.PrefetchScalarGridSpec(
        num_scalar_prefetch=3,
        grid=(nhb, nc),
        in_specs=[
            pl.BlockSpec((hb, L, LANE), im(lambda h, c, s: (cb_q // hb + h, c, 0))),
            pl.BlockSpec((hb, L, LANE), im(lambda h, c, s: ((cb_q + heads) // hb + h, c, 0))),
            pl.BlockSpec((hb, L, LANE), im(lambda h, c, s: ((cb_q + 2 * heads) // hb + h, c, 0))),
            pl.BlockSpec((hb, L, LANE), im(lambda h, c, s: (cb_z // hb + h, c, 0))),
            pl.BlockSpec((1, L, LANE), im(lambda h, c, s: (cb_sm, c, 0))),
            pl.BlockSpec((1, 1, 3 * hb, CONV_K - 1, LANE), im(lambda h, c, s: (s[c], h, 0, 0, 0))),
            pl.BlockSpec((1, hb, GDN_D, GDN_D), im(lambda h, c, s: (s[c], h, 0, 0))),
            pl.BlockSpec((1, CONV_K, 3 * hb, 1, LANE), im(lambda h, c, s: (h, 0, 0, 0, 0))),
            pl.BlockSpec((1, LANE), im(lambda h, c, s: (0, 0))),
            pl.BlockSpec((1, LANE), im(lambda h, c, s: (0, 0))),
            pl.BlockSpec((1, LANE), im(lambda h, c, s: (0, 0))),
        ],
        out_specs=[
            pl.BlockSpec((L, hb * LANE), im(lambda h, c, s: (c, h))),
            pl.BlockSpec((1, hb, GDN_D, GDN_D), im(lambda h, c, s: (s[c], h, 0, 0))),
        ],
        scratch_shapes=[pltpu.VMEM((3 * hb, L + SUBLANE, LANE), F32), pltpu.VMEM((hb, GDN_D, GDN_D), F32)],
    )
    return pl.pallas_call(
        functools.partial(_gdn_kernel, lane_beta=lane_beta, lane_a=lane_a),
        grid_spec=grid_spec,
        out_shape=[jax.ShapeDtypeStruct((t, heads * GDN_D), BF16),
                   jax.ShapeDtypeStruct((nseq, heads, GDN_D, GDN_D), F32)],
        compiler_params=_params(("arbitrary", "arbitrary")),
        name="gdn_mixer",
    )(seq_id, start, end, proj, proj, proj, proj, proj, conv_in, state_in, conv_w, dtb, alog, nw)


def _outproj_kernel(a1_ref, a2_ref, w1_ref, w2_ref, xa_ref, xb_ref, o_ref, *, na):
    acc = _dot(a1_ref[...], w1_ref[...]) + _dot(a2_ref[...], w2_ref[...])

    @pl.when(pl.program_id(0) < na)
    def _():
        o_ref[...] = xa_ref[...] + acc

    @pl.when(pl.program_id(0) >= na)
    def _():
        o_ref[...] = xb_ref[...] + acc


def _outproj(a1, a2, w1, w2, xa, xb, tm, tn):
    t, k1 = a1.shape
    k2 = a2.shape[1]
    d = w1.shape[1]
    na = xa.shape[0] // tm
    spec_a, spec_b = _two_source_specs((tm, tn), na, col=d // tn - 1)
    return pl.pallas_call(
        functools.partial(_outproj_kernel, na=na),
        grid=(t // tm, d // tn),
        in_specs=[pl.BlockSpec((tm, k1), lambda i, j: (i, 0)),
                  pl.BlockSpec((tm, k2), lambda i, j: (i, 0)),
                  pl.BlockSpec((k1, tn), lambda i, j: (0, j)),
                  pl.BlockSpec((k2, tn), lambda i, j: (0, j)),
                  spec_a, spec_b],
        out_specs=pl.BlockSpec((tm, tn), lambda i, j: (i, j)),
        out_shape=jax.ShapeDtypeStruct((t, d), F32),
        compiler_params=_params(("arbitrary", "arbitrary")),
        name="outproj",
    )(a1, a2, w1, w2, xa, xb)


def _router_kernel(x_ref, nw_ref, wr_ref, br_ref, idx_ref, gate_ref, rank_ref, run_scr):
    @pl.when(pl.program_id(0) == 0)
    def _():
        run_scr[...] = jnp.zeros(run_scr.shape, run_scr.dtype)

    x = x_ref[...]
    h = (x * lax.rsqrt(jnp.mean(x * x, axis=-1, keepdims=True) + EPS)) * nw_ref[...]
    logits = _dot(h, wr_ref[...], precision=HIGHEST) + br_ref[...]
    lane = lax.broadcasted_iota(I32, logits.shape, 1)
    vals = logits
    idx_out = jnp.zeros(logits.shape, I32)
    top = []
    sels = []
    for kk in range(TOP_K):
        m = jnp.max(vals, axis=-1, keepdims=True)
        sel = jnp.min(jnp.where(vals == m, lane, LANE), axis=-1, keepdims=True)
        idx_out = jnp.where(lane == kk, sel, idx_out)
        top.append(m)
        sels.append(sel)
        vals = jnp.where(lane == sel, -jnp.inf, vals)

    tm = logits.shape[0]
    picked = jnp.zeros(logits.shape, F32)
    for sel in sels:
        picked = picked + (lane == sel).astype(F32)
    ti = lax.broadcasted_iota(I32, (tm, tm), 0)
    si = lax.broadcasted_iota(I32, (tm, tm), 1)
    before = _dot((ti > si).astype(BF16), picked.astype(BF16)) + run_scr[...]
    rank_out = jnp.zeros(logits.shape, I32)
    for kk in range(TOP_K):
        r = jnp.sum(jnp.where(lane == sels[kk], before, 0.0), axis=-1, keepdims=True)
        rank_out = jnp.where(lane == kk, r.astype(I32), rank_out)
    rank_ref[...] = rank_out
    run_scr[...] = run_scr[...] + jnp.sum(picked, axis=0, keepdims=True)
    es = [jnp.exp(m - top[0]) for m in top]
    den = es[0]
    for e in es[1:]:
        den = den + e
    gate_out = jnp.zeros(logits.shape, F32)
    for kk in range(TOP_K):
        gate_out = jnp.where(lane == kk, es[kk] / den, gate_out)
    idx_ref[...] = idx_out
    gate_ref[...] = gate_out


def _router(x1, nw, wr, br, tm):
    t, d = x1.shape
    return pl.pallas_call(
        _router_kernel,
        grid=(t // tm,),
        in_specs=[pl.BlockSpec((tm, d), lambda i: (i, 0)),
                  pl.BlockSpec((1, d), lambda i: (0, 0)),
                  pl.BlockSpec((d, LANE), lambda i: (0, 0)),
                  pl.BlockSpec((1, LANE), lambda i: (0, 0))],
        out_specs=[pl.BlockSpec((tm, LANE), lambda i: (i, 0)) for _ in range(3)],
        out_shape=[jax.ShapeDtypeStruct((t, LANE), I32), jax.ShapeDtypeStruct((t, LANE), F32),
                   jax.ShapeDtypeStruct((t, LANE), I32)],
        scratch_shapes=[pltpu.VMEM((1, LANE), F32)],
        compiler_params=_params(("arbitrary",)),
        name="router",
    )(x1, nw, wr, br)


def _row_copy(src_hbm, dst_vmem, sem, src_row, dst_row):
    return pltpu.make_async_copy(src_hbm.at[pl.ds(src_row, 1), :], dst_vmem.at[pl.ds(dst_row, 1), :], sem)


def _gather_rows(idx_ref, idx0, src_hbm, dst_vmem, sem):
    def body(r, carry):
        _row_copy(src_hbm, dst_vmem, sem, idx_ref[idx0 + r], r).start()
        return carry
    lax.fori_loop(0, dst_vmem.shape[0], body, 0, unroll=GATHER_UNROLL)


def _drain_rows(src_hbm, dst_vmem, sem):
    pltpu.make_async_copy(src_hbm.at[pl.ds(0, dst_vmem.shape[0]), :], dst_vmem, sem).wait()


def _dispatch_kernel(tok_ref, nused_ref, x_hbm, nw_ref, o_ref, buf, sem):
    b = pl.program_id(0)
    bm = o_ref.shape[0]
    n_used = nused_ref[0]
    slot = lax.rem(b, 2)

    @pl.when(jnp.logical_and(b == 0, n_used > 0))
    def _():
        _gather_rows(tok_ref, 0, x_hbm, buf.at[0], sem.at[0])

    @pl.when(b + 1 < n_used)
    def _():
        _gather_rows(tok_ref, (b + 1) * bm, x_hbm, buf.at[1 - slot], sem.at[1 - slot])

    @pl.when(b < n_used)
    def _():
        _drain_rows(x_hbm, buf.at[slot], sem.at[slot])

        x = buf[slot]
        h = (x * lax.rsqrt(jnp.mean(x * x, axis=-1, keepdims=True) + EPS)) * nw_ref[...]
        o_ref[...] = h.astype(o_ref.dtype)

    @pl.when(b >= n_used)
    def _():
        o_ref[...] = jnp.zeros(o_ref.shape, o_ref.dtype)


def _dispatch(slot_tok, n_used, x1, nw, nblk):
    t, d = x1.shape
    grid_spec = pltpu.PrefetchScalarGridSpec(
        num_scalar_prefetch=2,
        grid=(nblk,),
        in_specs=[pl.BlockSpec(memory_space=pl.ANY),
                  pl.BlockSpec((1, d), lambda b, *_: (0, 0))],
        out_specs=pl.BlockSpec((MOE_BM, d), lambda b, *_: (b, 0)),
        scratch_shapes=[pltpu.VMEM((2, MOE_BM, d), F32), pltpu.SemaphoreType.DMA((2,))],
    )
    return pl.pallas_call(
        _dispatch_kernel,
        grid_spec=grid_spec,
        out_shape=jax.ShapeDtypeStruct((nblk * MOE_BM, d), BF16),
        compiler_params=_params(("arbitrary",)),
        name="moe_dispatch",
    )(slot_tok, n_used, x1, nw)


def _resident_rows_matmul(blk0_ref, nblk_ref, nused_ref, src_hbm, dst_hbm,
                          xbuf, obuf, zbuf, sem_in, sem_out, sem_z, compute):
    s = pl.program_id(0)
    n = pl.program_id(1)
    nt = pl.num_programs(1)
    bm = MOE_BM
    tn = obuf.shape[2]
    n_blocks = nblk_ref[s]
    first = blk0_ref[s]
    lin = s * nt + n
    slot = lax.rem(lin, 2)
    col = pl.multiple_of(n * tn, tn)

    def hbm_rows(blk):
        return pl.ds(pl.multiple_of(blk * bm, bm), bm)

    def x_copy(i):
        return pltpu.make_async_copy(src_hbm.at[hbm_rows(first + i), :], xbuf.at[pl.ds(i * bm, bm), :], sem_in)

    def o_copy(slot_, blk, i):
        return pltpu.make_async_copy(obuf.at[slot_, pl.ds(i * bm, bm), :],
                                     dst_hbm.at[hbm_rows(blk), pl.ds(col, tn)], sem_out.at[slot_])

    def for_blocks(count, fn):
        for i in range(MOE_SB_BLOCKS):
            pl.when(i < count)(functools.partial(fn, i))

    def wait_out(step, slot_):
        for_blocks(nblk_ref[lax.div(step, nt)], lambda i: o_copy(slot_, 0, i).wait())

    @pl.when(lin == 0)
    def _():
        zbuf[...] = jnp.zeros(zbuf.shape, zbuf.dtype)

    @pl.when(n == 0)
    def _():
        for_blocks(n_blocks, lambda i: x_copy(i).start())
        for_blocks(n_blocks, lambda i: x_copy(i).wait())

    @pl.when(lin >= 2)
    def _():
        wait_out(lin - 2, slot)

    def matmul(blk_off, m):
        rows = pl.ds(pl.multiple_of(blk_off * bm, bm), m * bm)
        obuf[slot, rows, :] = compute(lambda: xbuf[rows, :])

    @pl.when(n_blocks == MOE_SB_BLOCKS)
    def _():
        matmul(0, MOE_SB_BLOCKS)

    @pl.when(jnp.logical_and(n_blocks > 0, n_blocks < MOE_SB_BLOCKS))
    def _():
        top = MOE_PARTIAL_HEIGHTS[0]
        n_top = lax.div(n_blocks, top)
        lax.fori_loop(0, n_top, lambda i, c: (matmul(i * top, top), c)[1], 0)
        done = n_top * top
        for m in MOE_PARTIAL_HEIGHTS[1:]:
            has = lax.rem(lax.div(n_blocks, m), 2) == 1
            pl.when(has)(functools.partial(matmul, done, m))
            done = done + jnp.where(has, m, 0)
    for_blocks(n_blocks, lambda i: o_copy(slot, first + i, i).start())

    @pl.when(lin == pl.num_programs(0) * nt - 1)
    def _():
        for_blocks(n_blocks, lambda i: o_copy(slot, 0, i).wait())

        @pl.when(lin >= 1)
        def _():
            wait_out(lin - 1, 1 - slot)

    @pl.when(s == pl.num_programs(0) - 1)
    def _():
        n_used = nused_ref[0]
        n_spare = dst_hbm.shape[0] // bm - n_used

        def z_copy(i):
            return pltpu.make_async_copy(zbuf, dst_hbm.at[hbm_rows(n_used + i), pl.ds(col, tn)], sem_z)
        lax.fori_loop(0, n_spare, lambda i, c: (z_copy(i).start(), c)[1], 0)
        lax.fori_loop(0, n_spare, lambda i, c: (z_copy(i).wait(), c)[1], 0)


def _resident_call(body, sb_e, sb_blk0, sb_nblk, n_used, src, weights, biases, dff_out, tn, out_dtype, name):
    p, k = src.shape
    nsb = sb_e.shape[0]
    nt = dff_out // tn

    def w_map(s, n, e_ref, b0_ref, nb_ref, nu_ref):
        return (e_ref[s], 0, jnp.where(nb_ref[s] > 0, n, nt - 1))

    grid_spec = pltpu.PrefetchScalarGridSpec(
        num_scalar_prefetch=4,
        grid=(nsb, nt),
        in_specs=([pl.BlockSpec(memory_space=pl.ANY)]
                  + [pl.BlockSpec((1, k, tn), w_map) for _ in weights]
                  + [pl.BlockSpec((1, 1, tn), w_map) for _ in biases]),
        out_specs=pl.BlockSpec(memory_space=pl.ANY),
        scratch_shapes=[pltpu.VMEM((MOE_SB_BLOCKS * MOE_BM, k), BF16),
                        pltpu.VMEM((2, MOE_SB_BLOCKS * MOE_BM, tn), out_dtype),
                        pltpu.VMEM((MOE_BM, tn), out_dtype),
                        pltpu.SemaphoreType.DMA(()), pltpu.SemaphoreType.DMA((2,)), pltpu.SemaphoreType.DMA(())],
    )
    return pl.pallas_call(
        body,
        grid_spec=grid_spec,
        out_shape=jax.ShapeDtypeStruct((p, dff_out), out_dtype),
        compiler_params=_params(("arbitrary", "arbitrary")),
        name=name,
    )(sb_e, sb_blk0, sb_nblk, n_used, src, *weights, *biases)


def _gateup_kernel(e_ref, blk0_ref, nblk_ref, nused_ref, x_hbm, wg_ref, wu_ref, bg_ref, bu_ref, act_hbm,
                   xbuf, obuf, zbuf, sem_in, sem_out, sem_z):
    def compute(x):
        gate = jnp.minimum(_dot(x(), wg_ref[0].astype(BF16)) + bg_ref[0], SWIGLU_LIMIT)
        up = jnp.clip(_dot(x(), wu_ref[0].astype(BF16)) + bu_ref[0], -SWIGLU_LIMIT, SWIGLU_LIMIT)
        act = gate * jax.nn.sigmoid(SWIGLU_ALPHA * gate) * (up + 1.0)
        return act.astype(obuf.dtype)

    _resident_rows_matmul(blk0_ref, nblk_ref, nused_ref, x_hbm, act_hbm,
                          xbuf, obuf, zbuf, sem_in, sem_out, sem_z, compute)


def _down_kernel(e_ref, blk0_ref, nblk_ref, nused_ref, a_hbm, wd_ref, bd_ref, y_hbm,
                 xbuf, obuf, zbuf, sem_in, sem_out, sem_z):
    def compute(a):
        return _dot(a(), wd_ref[0].astype(BF16)) + bd_ref[0]

    _resident_rows_matmul(blk0_ref, nblk_ref, nused_ref, a_hbm, y_hbm,
                          xbuf, obuf, zbuf, sem_in, sem_out, sem_z, compute)


def _combine_kernel(pos_ref, ys_hbm, x_ref, g_ref, nw_ref, o_ref, buf, sem, *, tile0):
    i = pl.program_id(0)
    tt = o_ref.shape[0]
    n = TOP_K * tt
    slot = lax.rem(i, 2)

    def issue(tile, slot_):
        _gather_rows(pos_ref, (tile0 + tile) * n, ys_hbm, buf.at[slot_], sem.at[slot_])

    @pl.when(i == 0)
    def _():
        issue(0, 0)

    @pl.when(i + 1 < pl.num_programs(0))
    def _():
        issue(i + 1, 1 - slot)

    _drain_rows(ys_hbm, buf.at[slot], sem.at[slot])

    g = g_ref[...]
    acc = x_ref[...]
    for kk in range(TOP_K):
        acc = acc + buf[slot, kk * tt:(kk + 1) * tt, :] * g[:, kk:kk + 1]
    y = acc * lax.rsqrt(jnp.mean(acc * acc, axis=-1, keepdims=True) + EPS)
    o_ref[...] = y * nw_ref[...]


def _combine(pos_tiles, ys, x1, gates, nw, tt, tile0, n_tiles):
    d = x1.shape[1]
    grid_spec = pltpu.PrefetchScalarGridSpec(
        num_scalar_prefetch=1,
        grid=(n_tiles,),
        in_specs=[pl.BlockSpec(memory_space=pl.ANY),
                  pl.BlockSpec((tt, d), lambda i, pos: (tile0 + i, 0)),
                  pl.BlockSpec((tt, LANE), lambda i, pos: (tile0 + i, 0)),
                  pl.BlockSpec((1, d), lambda i, pos: (0, 0))],
        out_specs=pl.BlockSpec((tt, d), lambda i, pos: (i, 0)),
        scratch_shapes=[pltpu.VMEM((2, TOP_K * tt, d), F32), pltpu.SemaphoreType.DMA((2,))],
    )
    return pl.pallas_call(
        functools.partial(_combine_kernel, tile0=tile0),
        grid_spec=grid_spec,
        out_shape=jax.ShapeDtypeStruct((n_tiles * tt, d), F32),
        compiler_params=_params(("arbitrary",)),
        name="moe_combine",
    )(pos_tiles, ys, x1, gates, nw)


def _pad_lanes(v, offset):
    out = jnp.zeros((LANE,), F32)
    return lax.dynamic_update_slice(out, v.astype(F32), (offset,)).reshape(1, LANE)


def _ssd_group_layout(a, width):
    lead = a.shape[:-2]
    rows = a.shape[-2]
    rb = width // LANE // SSD_G
    xs = a[..., :width].reshape(*lead, rows, SSD_G, rb, LANE)
    bs = a[..., width:width + SSD_G * SSD_N].reshape(*lead, rows, SSD_G, 1, LANE)
    cs = a[..., width + SSD_G * SSD_N:].reshape(*lead, rows, SSD_G, 1, LANE)
    cat = jnp.concatenate([xs, bs, cs], axis=-2)
    n = cat.ndim
    return jnp.moveaxis(cat, n - 4, n - 2)


def _gdn_block_layout(a, heads, hb):
    lead = a.shape[:-2]
    rows = a.shape[-2]
    r = a.reshape(*lead, rows, 3, heads // hb, hb, LANE)
    n = r.ndim
    r = jnp.moveaxis(r, n - 5, n - 2)
    r = jnp.moveaxis(r, n - 5, n - 4)
    return r.reshape(*lead, heads // hb, 3 * hb, rows, LANE)


def _pick_tn_cb(ncb):
    return INPROJ_TN_CB, (-ncb) % INPROJ_TN_CB


def kernel(x_prompt, x_sample, state_ssd_conv, state_ssd, state_gdn_conv, state_gdn, norm_mix, w_in, ssd_conv_w,
           ssd_conv_b, ssd_dt_bias, ssd_A_log, ssd_D, ssd_norm, gdn_conv_w, gdn_dt_bias, gdn_A_log, gdn_norm, w_out,
           norm_ffn, w_router, b_router, w_gate, b_gate, w_up, b_up, w_down, b_down, norm_final):
    assert w_in.shape[0] == 1, "single layer"
    nb_p, seq_p, d = x_prompt.shape
    nb_s, seq_s, _ = x_sample.shape
    assert seq_p % CHUNK == 0 and seq_s % CHUNK == 0
    ssd_heads = d // SSD_P
    ssd_w = ssd_heads * SSD_P
    ssd_cs = ssd_w + 2 * SSD_G * SSD_N
    gdn_heads = d // GDN_D
    gdn_w = gdn_heads * GDN_D
    t_p = nb_p * seq_p
    t = t_p + nb_s * seq_s
    nseq = nb_p + nb_s

    x_p = x_prompt.reshape(t_p, d)
    x_s = x_sample.reshape(nb_s * seq_s, d)
    seq_len = [seq_p] * nb_p + [seq_s] * nb_s
    seq_id, start, end = [], [], []
    for s, n in enumerate(seq_len):
        for cidx in range(n // CHUNK):
            seq_id.append(s)
            start.append(int(cidx == 0))
            end.append(int(cidx == n // CHUNK - 1))
    seq_id = jnp.asarray(np.array(seq_id, np.int32))
    start = jnp.asarray(np.array(start, np.int32))
    end = jnp.asarray(np.array(end, np.int32))

    o_z, o_xbc, o_dt = 0, ssd_w, ssd_w + ssd_cs
    o_qkv = o_dt + ssd_heads
    o_zg = o_qkv + 3 * gdn_w
    o_b = o_zg + gdn_w
    o_a = o_b + gdn_heads
    n_small = ssd_heads + 2 * gdn_heads
    assert n_small <= LANE
    cb_q = 0
    cb_zg = cb_q + 3 * gdn_w // LANE
    cb_z = cb_zg + gdn_w // LANE
    cb_x = cb_z + ssd_w // LANE
    cb_sm = cb_x + ssd_cs // LANE
    ncb = cb_sm + 1
    tn_cb, pad_cb = _pick_tn_cb(ncb)
    assert o_zg == o_qkv + 3 * gdn_w and o_xbc == o_z + ssd_w and o_a == o_b + gdn_heads
    w_perm = _permute_cast_weight(w_in[0], o_qkv, 4 * gdn_w, ssd_w + ssd_cs, o_dt, o_b, ssd_heads, n_small,
                                  (ncb + pad_cb) * LANE)
    lane_beta = ssd_heads
    lane_a = ssd_heads + gdn_heads

    tm_big = _row_tile(t, 1056)
    tm_mid = _row_tile(t, 528)
    tm_src = _row_tile(int(np.gcd(t_p, t - t_p)), 512)
    h = _rmsnorm_cast(x_p, x_s, norm_mix[0], tm_src)
    proj = _inproj(h, w_perm, tm_big, tn_cb)

    def with_zero_prompt(a):
        return jnp.concatenate([jnp.zeros((nb_p,) + a.shape[1:], a.dtype), a], axis=0)

    ssd_conv0 = _ssd_group_layout(with_zero_prompt(state_ssd_conv[0]), ssd_w)
    gdn_hb = min(GDN_HB, gdn_heads)
    gdn_conv0 = _gdn_block_layout(with_zero_prompt(state_gdn_conv[0]), gdn_heads, gdn_hb)
    s0 = with_zero_prompt(state_ssd[0])
    ssd_s0 = s0.reshape(nseq, ssd_heads // 2, 2, SSD_P, SSD_N).transpose(0, 1, 4, 2, 3).reshape(
        nseq, ssd_heads // 2, SSD_N, LANE)
    gdn_s0 = with_zero_prompt(state_gdn[0])

    ssd_cw = _ssd_group_layout(ssd_conv_w[0][None], ssd_w)[0]
    ssd_cw = jnp.swapaxes(ssd_cw, 1, 2)[:, :, :, None, :]
    ssd_cb = _ssd_group_layout(ssd_conv_b[0][None, None], ssd_w)[0]
    d_exp = jnp.repeat(ssd_D[0], SSD_P).reshape(ssd_w // LANE, 1, LANE)
    ssd_nw = ssd_norm[0].reshape(ssd_w // LANE, 1, LANE)
    y_ssd, ssd_s = _ssd_mixer(
        proj, seq_id, start, end, ssd_conv0, ssd_s0, ssd_cw, ssd_cb,
        _pad_lanes(ssd_dt_bias[0], 0), _pad_lanes(ssd_A_log[0], 0), d_exp, ssd_nw,
        t, ssd_w, cb_z, cb_x, cb_x + ssd_w // LANE, cb_x + ssd_w // LANE + SSD_G, cb_sm)

    gdn_cw = _gdn_block_layout(gdn_conv_w[0][None], gdn_heads, gdn_hb)[0]
    gdn_cw = jnp.swapaxes(gdn_cw, 1, 2)[:, :, :, None, :]
    y_gdn, gdn_s = _gdn_mixer(
        proj, seq_id, start, end, gdn_conv0, gdn_s0, gdn_cw,
        _pad_lanes(gdn_dt_bias[0], lane_a), _pad_lanes(gdn_A_log[0], lane_a), gdn_norm[0].reshape(1, LANE),
        t, gdn_heads, cb_q, cb_zg, cb_sm, lane_beta, lane_a)

    w_o = w_out[0].astype(BF16)
    x1 = _outproj(y_ssd, y_gdn, w_o[:ssd_w], w_o[ssd_w:], x_p, x_s, tm_src, min(512, d))

    wr = jnp.concatenate([w_router[0], jnp.zeros((d, LANE - N_EXPERTS), F32)], axis=1)
    br = jnp.concatenate([b_router[0], jnp.full((LANE - N_EXPERTS,), NEG_BIG, F32)]).reshape(1, LANE)
    nffn = norm_ffn[0].reshape(1, d)
    idx_pad, gate_pad, rank_pad = _router(x1, nffn, wr, br, tm_mid)
    top_idx = idx_pad[:, :TOP_K]
    tk = t * TOP_K
    flat_e = top_idx.reshape(tk)
    order = jnp.argsort(flat_e).astype(I32)
    counts = jnp.sum((flat_e[:, None] == jnp.arange(N_EXPERTS, dtype=I32)[None, :]).astype(I32), axis=0)
    starts = jnp.cumsum(counts) - counts
    pcounts = (counts + MOE_BM - 1) // MOE_BM * MOE_BM
    pends = jnp.cumsum(pcounts)
    pstarts = pends - pcounts
    nblk = -(-tk // MOE_BM) + N_EXPERTS
    pos = (pstarts[flat_e] + rank_pad[:, :TOP_K].reshape(tk)).astype(I32)

    def count_le(ends, v):
        return jnp.sum((ends[None, :] <= v[:, None]).astype(I32), axis=1)

    blk_e = jnp.minimum(count_le(pends, jnp.arange(nblk, dtype=I32) * MOE_BM), N_EXPERTS - 1).astype(I32)
    n_used = (pends[-1] // MOE_BM).astype(I32).reshape(1)
    blk0 = (pstarts // MOE_BM).astype(I32)
    nblk_e = (pcounts // MOE_BM).astype(I32)
    slot = jnp.arange(nblk * MOE_BM, dtype=I32)
    slot_e = jnp.repeat(blk_e, MOE_BM)
    slot_off = slot - pstarts[slot_e].astype(I32)
    slot_src = jnp.clip(starts[slot_e].astype(I32) + slot_off, 0, tk - 1)
    slot_tok = jnp.where(slot_off < counts[slot_e], (order // TOP_K)[slot_src], 0).astype(I32)
    nsb_e = (nblk_e + MOE_SB_BLOCKS - 1) // MOE_SB_BLOCKS
    sb_ends = jnp.cumsum(nsb_e)
    sb_starts = sb_ends - nsb_e
    n_sb = sb_ends[-1]
    sb_i = jnp.arange(N_EXPERTS + nblk // MOE_SB_BLOCKS, dtype=I32)
    sb_c = jnp.minimum(sb_i, n_sb - 1)
    sb_e = jnp.minimum(count_le(sb_ends, sb_c), N_EXPERTS - 1).astype(I32)
    sb_j = sb_c - sb_starts[sb_e]
    sb_blk0 = (blk0[sb_e] + sb_j * MOE_SB_BLOCKS).astype(I32)
    sb_nblk = jnp.where(sb_i < n_sb, jnp.clip(nblk_e[sb_e] - sb_j * MOE_SB_BLOCKS, 0, MOE_SB_BLOCKS), 0).astype(I32)

    xs = _dispatch(slot_tok, n_used, x1, nffn, nblk)
    act = _resident_call(_gateup_kernel, sb_e, sb_blk0, sb_nblk, n_used, xs, (w_gate[0], w_up[0]),
                         (b_gate[0][:, None, :], b_up[0][:, None, :]), w_gate.shape[3], min(MOE_TN, d), BF16,
                         "moe_gateup")
    ys = _resident_call(_down_kernel, sb_e, sb_blk0, sb_nblk, n_used, act, (w_down[0],),
                        (b_down[0][:, None, :],), d, min(MOE_TN_DOWN, d), F32, "moe_down")

    tt = CHUNK
    pos_tiles = pos.reshape(t // tt, tt, TOP_K).transpose(0, 2, 1).reshape(tk)
    nfin = norm_final.reshape(1, d)
    y_prompt = _combine(pos_tiles, ys, x1, gate_pad, nfin, tt, 0, t_p // tt).reshape(nb_p, seq_p, d)
    y_sample = _combine(pos_tiles, ys, x1, gate_pad, nfin, tt, t_p // tt, (t - t_p) // tt).reshape(nb_s, seq_s, d)

    def last_rows(cb0, ncols):
        nblk_c = ncols // LANE
        ends = np.cumsum(seq_len)
        rows = jnp.concatenate(
            [lax.slice(proj, (cb0, int(e) - (CONV_K - 1), 0), (cb0 + nblk_c, int(e), LANE)) for e in ends], axis=1)
        a = rows.reshape(nblk_c, nseq, CONV_K - 1, LANE).transpose(1, 2, 0, 3).reshape(nseq, CONV_K - 1, ncols)
        return a[:nb_p][None], a[nb_p:][None]

    ssd_conv_p, ssd_conv_s = last_rows(cb_x, ssd_cs)
    gdn_conv_p, gdn_conv_s = last_rows(cb_q, 3 * gdn_w)
    ssd_state = ssd_s.reshape(nseq, ssd_heads // 2, SSD_N, 2, SSD_P).transpose(0, 1, 3, 4, 2).reshape(
        nseq, ssd_heads, SSD_P, SSD_N)
    return (y_prompt, y_sample,
            ssd_conv_p, ssd_state[:nb_p][None], gdn_conv_p, gdn_s[:nb_p][None],
            ssd_conv_s, ssd_state[nb_p:][None], gdn_conv_s, gdn_s[nb_p:][None])
```

```python
import functools

import numpy as np
import jax
import jax.numpy as jnp
from jax import lax
from jax.experimental import pallas as pl
from jax.experimental.pallas import tpu as pltpu

F32 = jnp.float32
BF16 = jnp.bfloat16
I32 = jnp.int32

LANE = 128
SUBLANE = 8
VMEM_LIMIT = 60 * 1024 * 1024

CHUNK = 64
CONV_K = 4
SSD_P = 64
SSD_N = 128
SSD_G = 8
GDN_D = 128
N_EXPERTS = 32
TOP_K = 4
SWIGLU_LIMIT = 7.0
SWIGLU_ALPHA = 1.702
EPS = 1e-6
NEG_BIG = -1e30

MOE_BM = 256
MOE_SB_BLOCKS = 9
MOE_PARTIAL_HEIGHTS = (4, 2, 1)
MOE_TN = 256
MOE_TN_DOWN = 512
GATHER_UNROLL = 8
INPROJ_TN_CB = 10
GDN_HB = 32
SSD_GS = 8
HIGHEST = lax.Precision.HIGHEST


def _row_tile(n, target, mult=16):
    best = None
    for t in range(mult, min(n, target) + 1, mult):
        if n % t == 0:
            best = t
    assert best is not None, (n, target)
    return best


def _params(sem):
    return pltpu.CompilerParams(dimension_semantics=sem, vmem_limit_bytes=VMEM_LIMIT)


def _silu(x):
    return x * jax.nn.sigmoid(x)


def _softplus(x):
    return jnp.maximum(x, 0.0) + jnp.log1p(jnp.exp(-jnp.abs(x)))


def _dot(a, b, **kw):
    return jnp.dot(a, b, preferred_element_type=F32, **kw)


def _split3(a):
    hi = a.astype(BF16)
    r = a - hi.astype(F32)
    mid = r.astype(BF16)
    lo = (r - mid.astype(F32)).astype(BF16)
    return hi, mid, lo


def _two_source_specs(block, na, col=None):
    if col is None:
        ia = lambda i, *_: (jnp.minimum(i, na - 1), 0)
        ib = lambda i, *_: (jnp.maximum(i - na, 0), 0)
    else:
        ia = lambda i, j, *_: (jnp.minimum(i, na - 1), jnp.where(i < na, j, col))
        ib = lambda i, j, *_: (jnp.maximum(i - na, 0), jnp.where(i >= na, j, 0))
    return pl.BlockSpec(block, ia), pl.BlockSpec(block, ib)


def _rmsnorm_cast_kernel(xa_ref, xb_ref, w_ref, o_ref, *, na):
    def body(x_ref):
        x = x_ref[...]
        xn = x * lax.rsqrt(jnp.mean(x * x, axis=-1, keepdims=True) + EPS)
        o_ref[...] = (xn * w_ref[...]).astype(o_ref.dtype)

    pl.when(pl.program_id(0) < na)(lambda: body(xa_ref))
    pl.when(pl.program_id(0) >= na)(lambda: body(xb_ref))


def _rmsnorm_cast(xa, xb, w, tm):
    ta, d = xa.shape
    tb = xb.shape[0]
    na = ta // tm
    spec_a, spec_b = _two_source_specs((tm, d), na)
    return pl.pallas_call(
        functools.partial(_rmsnorm_cast_kernel, na=na),
        grid=(na + tb // tm,),
        in_specs=[spec_a, spec_b, pl.BlockSpec((1, d), lambda i: (0, 0))],
        out_specs=pl.BlockSpec((tm, d), lambda i: (i, 0)),
        out_shape=jax.ShapeDtypeStruct((ta + tb, d), BF16),
        compiler_params=_params(("arbitrary",)),
        name="rmsnorm_cast",
    )(xa, xb, w.reshape(1, d))


def _wperm_kernel(main_ref, right_ref, dt_ref, ba_ref, o_ref, *, t_b, t_a, shift, n_dt, n_small):
    j = pl.program_id(0)
    d, ct = o_ref.shape
    rc = 256

    def by_rows(fn):
        def body(i, carry):
            rows = pl.ds(pl.multiple_of(i * rc, rc), rc)
            o_ref[rows, :] = fn(rows).astype(o_ref.dtype)
            return carry
        lax.fori_loop(0, d // rc, body, 0)

    @pl.when(j < t_b)
    def _():
        by_rows(lambda rows: jnp.concatenate([main_ref[rows, :], right_ref[rows, :]], axis=1)[:, shift:shift + ct])

    @pl.when(jnp.logical_and(j >= t_b, j < t_b + t_a))
    def _():
        by_rows(lambda rows: main_ref[rows, :])

    @pl.when(j == t_b + t_a)
    def _():
        def small(rows):
            lane = lax.broadcasted_iota(I32, (rc, LANE), 1)
            blk = jnp.where(lane < n_dt, dt_ref[rows, :], jnp.where(lane < n_small, ba_ref[rows, :], 0.0))
            return jnp.concatenate([blk, jnp.zeros((rc, ct - LANE), F32)], axis=1)
        by_rows(small)

    @pl.when(j > t_b + t_a)
    def _():
        o_ref[...] = jnp.zeros(o_ref.shape, o_ref.dtype)


def _permute_cast_weight(w, o_b_part, w_b_part, w_a_part, o_dt, o_ba, n_dt, n_small, n_out):
    d = w.shape[0]
    ct = 2 * LANE
    shift = o_b_part % LANE
    base = o_b_part - shift
    assert base % ct == 0 and w_b_part % ct == 0 and w_a_part % ct == 0 and n_out % ct == 0
    assert o_dt % LANE == 0 and o_ba % LANE == n_dt and n_out >= w_b_part + w_a_part + ct
    t_b, t_a = w_b_part // ct, w_a_part // ct

    def main_map(j):
        return (0, jnp.where(j < t_b, base // ct + j, jnp.where(j < t_b + t_a, j - t_b, 0)))

    return pl.pallas_call(
        functools.partial(_wperm_kernel, t_b=t_b, t_a=t_a, shift=shift, n_dt=n_dt, n_small=n_small),
        grid=(n_out // ct,),
        in_specs=[pl.BlockSpec((d, ct), main_map),
                  pl.BlockSpec((d, LANE), lambda j: (0, base // LANE + 2 * (jnp.minimum(j, t_b - 1) + 1))),
                  pl.BlockSpec((d, LANE), lambda j: (0, o_dt // LANE)),
                  pl.BlockSpec((d, LANE), lambda j: (0, o_ba // LANE))],
        out_specs=pl.BlockSpec((d, ct), lambda j: (0, j)),
        out_shape=jax.ShapeDtypeStruct((d, n_out), BF16),
        compiler_params=_params(("arbitrary",)),
        name="inproj_weight_layout",
    )(w, w, w, w)


def _inproj_kernel(a_ref, w_ref, o_ref):
    ncb = o_ref.shape[0]
    step = 2 if ncb % 2 == 0 else 1
    for j in range(0, ncb, step):
        acc = _dot(a_ref[...], w_ref[:, j * LANE:(j + step) * LANE])
        for s in range(step):
            o_ref[j + s] = acc[:, s * LANE:(s + 1) * LANE]


def _inproj(a, w, tm, tn_cb):
    t, d = a.shape
    n = w.shape[1]
    ncb = n // LANE
    return pl.pallas_call(
        _inproj_kernel,
        grid=(t // tm, ncb // tn_cb),
        in_specs=[pl.BlockSpec((tm, d), lambda i, j: (i, 0)),
                  pl.BlockSpec((d, tn_cb * LANE), lambda i, j: (0, j))],
        out_specs=pl.BlockSpec((tn_cb, tm, LANE), lambda i, j: (j, i, 0)),
        out_shape=jax.ShapeDtypeStruct((ncb, t, LANE), F32),
        compiler_params=_params(("arbitrary", "arbitrary")),
        name="inproj",
    )(a, w)


def _causal_conv(u, ext_scr, conv_in, taps, is_start):
    L = u.shape[1]
    base = SUBLANE - (CONV_K - 1)

    @pl.when(is_start)
    def _():
        ext_scr[:, base:SUBLANE, :] = conv_in()

    ext_scr[:, SUBLANE:SUBLANE + L, :] = u
    acc = ext_scr[:, base:base + L, :] * taps(0)
    for j in range(1, CONV_K):
        acc = acc + ext_scr[:, base + j:base + j + L, :] * taps(j)
    ext_scr[:, base:SUBLANE, :] = ext_scr[:, base + L:SUBLANE + L, :]
    return acc


def _cumsum_rows(a, incl):
    m = incl.astype(BF16)
    hi, mid, lo = _split3(a)
    return _dot(m, hi) + (_dot(m, mid) + _dot(m, lo))


def _select_dot(a, onehot):
    oh = onehot.astype(BF16)
    hi, mid, lo = _split3(a)
    return _dot(hi, oh) + (_dot(mid, oh) + _dot(lo, oh))


def _ssd_kernel(seq_ref, start_ref, end_ref,
                z_ref, x_ref, b_ref, c_ref, sm_ref, cin_ref, sin_ref, cw_ref, cb_ref,
                dtb_ref, alog_ref, d_ref, nw_ref,
                y_ref, sout_ref,
                ext_scr, s_scr):
    gi0 = pl.program_id(0)
    c = pl.program_id(1)
    gs = b_ref.shape[0]
    rb = x_ref.shape[0] // gs
    nb = rb + 2
    L = x_ref.shape[1]
    is_start = start_ref[c] == 1

    @pl.when(is_start)
    def _():
        s_scr[...] = sin_ref[0]

    u = jnp.concatenate(
        [p for gi in range(gs) for p in (x_ref[gi * rb:(gi + 1) * rb], b_ref[gi:gi + 1], c_ref[gi:gi + 1])], axis=0)
    conv = _causal_conv(u, ext_scr, lambda: cin_ref[0].reshape(gs * nb, CONV_K - 1, LANE),
                        lambda j: cw_ref[:, j].reshape(gs * nb, 1, LANE), is_start)
    uc = _silu(conv + cb_ref[...].reshape(gs * nb, 1, LANE))

    sm = sm_ref[0]
    dt_all = _softplus(sm + dtb_ref[...])
    a_all = dt_all * (-jnp.exp(alog_ref[...]))
    ti = lax.broadcasted_iota(I32, (L, L), 0)
    si = lax.broadcasted_iota(I32, (L, L), 1)
    cum_all = _cumsum_rows(a_all, ti >= si)
    ej = lax.broadcasted_iota(I32, (LANE, gs * rb * LANE), 0)
    ec = lax.broadcasted_iota(I32, (LANE, gs * rb * LANE), 1)
    expand = ej == gi0 * (gs * 2 * rb) + jnp.right_shift(ec, 6)
    dtx = _select_dot(dt_all, expand)
    cumx = _select_dot(cum_all, expand)

    t2 = lax.broadcasted_iota(I32, (L, LANE), 0)
    l2 = lax.broadcasted_iota(I32, (L, LANE), 1)
    s2 = jnp.bitwise_and(l2, SSD_P - 1)
    diag2 = (t2 == s2).astype(F32)
    causal2 = t2 >= s2
    left = l2 < SSD_P

    for gi in range(gs):
        bm = uc[gi * nb + rb]
        cm = uc[gi * nb + rb + 1]
        b2 = jnp.concatenate([bm, bm], axis=0).astype(BF16)
        cm_b = cm.astype(BF16)
        bm_b = bm.astype(BF16)
        cb2 = lax.dot_general(cm_b, b2, (((1,), (1,)), ((), ())), preferred_element_type=F32)
        ygs = []
        ms = jnp.zeros((L, 1), F32)
        for j in range(rb):
            jj = gi * rb + j
            ccol = cumx[:, jj * LANE:(jj + 1) * LANE]
            dtc = dtx[:, jj * LANE:(jj + 1) * LANE]
            crow = jnp.sum(ccol * diag2, axis=0, keepdims=True)
            dec = jnp.exp(jnp.where(causal2, ccol - crow, NEG_BIG))
            ww = (cb2 * dec).astype(BF16)
            xb = uc[gi * nb + j]
            xdt = xb * dtc
            xbd = jnp.concatenate([jnp.where(left, xdt, 0.0), jnp.where(left, 0.0, xdt)], axis=0).astype(BF16)
            y = _dot(ww, xbd)
            sj = s_scr[jj]
            y = y + _dot(cm_b, sj.astype(BF16)) * jnp.exp(ccol)
            y = y + d_ref[jj] * xb
            cl = ccol[L - 1:L, :]
            xw = (xdt * jnp.exp(cl - ccol)).astype(BF16)
            s_scr[jj] = sj * jnp.exp(cl) + lax.dot_general(
                bm_b, xw, (((0,), (0,)), ((), ())), preferred_element_type=F32)
            yg = y * _silu(z_ref[jj])
            ms = ms + jnp.sum(yg * yg, axis=-1, keepdims=True)
            ygs.append(yg)
        inv = lax.rsqrt(ms / (rb * LANE) + EPS)
        for j in range(rb):
            jj = gi * rb + j
            y_ref[:, jj * LANE:(jj + 1) * LANE] = ((ygs[j] * inv) * nw_ref[jj]).astype(y_ref.dtype)

    @pl.when(end_ref[c] == 1)
    def _():
        sout_ref[0] = s_scr[...]


def _ssd_mixer(proj, seq_id, start, end, conv_in, state_in, conv_w, conv_b, dtb, alog, d_exp, nw,
               t, width, cb_z, cb_x, cb_b, cb_c, cb_sm):
    rb = width // LANE // SSD_G
    nb = rb + 2
    nc = t // CHUNK
    nseq = state_in.shape[0]
    L = CHUNK

    def im(f):
        return lambda g, c, s, st, en: f(g, c, s)

    gs = SSD_GS
    grb = gs * rb
    grid_spec = pltpu.PrefetchScalarGridSpec(
        num_scalar_prefetch=3,
        grid=(SSD_G // gs, nc),
        in_specs=[
            pl.BlockSpec((grb, L, LANE), im(lambda g, c, s: (cb_z // grb + g, c, 0))),
            pl.BlockSpec((grb, L, LANE), im(lambda g, c, s: (cb_x // grb + g, c, 0))),
            pl.BlockSpec((gs, L, LANE), im(lambda g, c, s: (cb_b // gs + g, c, 0))),
            pl.BlockSpec((gs, L, LANE), im(lambda g, c, s: (cb_c // gs + g, c, 0))),
            pl.BlockSpec((1, L, LANE), im(lambda g, c, s: (cb_sm, c, 0))),
            pl.BlockSpec((1, gs, nb, CONV_K - 1, LANE), im(lambda g, c, s: (s[c], g, 0, 0, 0))),
            pl.BlockSpec((1, grb, SSD_N, LANE), im(lambda g, c, s: (s[c], g, 0, 0))),
            pl.BlockSpec((gs, CONV_K, nb, 1, LANE), im(lambda g, c, s: (g, 0, 0, 0, 0))),
            pl.BlockSpec((gs, nb, 1, LANE), im(lambda g, c, s: (g, 0, 0, 0))),
            pl.BlockSpec((1, LANE), im(lambda g, c, s: (0, 0))),
            pl.BlockSpec((1, LANE), im(lambda g, c, s: (0, 0))),
            pl.BlockSpec((grb, 1, LANE), im(lambda g, c, s: (g, 0, 0))),
            pl.BlockSpec((grb, 1, LANE), im(lambda g, c, s: (g, 0, 0))),
        ],
        out_specs=[
            pl.BlockSpec((L, grb * LANE), im(lambda g, c, s: (c, g))),
            pl.BlockSpec((1, grb, SSD_N, LANE), im(lambda g, c, s: (s[c], g, 0, 0))),
        ],
        scratch_shapes=[pltpu.VMEM((gs * nb, L + SUBLANE, LANE), F32), pltpu.VMEM((grb, SSD_N, LANE), F32)],
    )
    assert cb_z % grb == 0 and cb_x % grb == 0 and cb_b % gs == 0 and cb_c % gs == 0 and SSD_G % gs == 0
    return pl.pallas_call(
        _ssd_kernel,
        grid_spec=grid_spec,
        out_shape=[jax.ShapeDtypeStruct((t, width), BF16),
                   jax.ShapeDtypeStruct((nseq, width // LANE, SSD_N, LANE), F32)],
        compiler_params=_params(("arbitrary", "arbitrary")),
        name="ssd_mixer",
    )(seq_id, start, end, proj, proj, proj, proj, proj, conv_in, state_in, conv_w, conv_b, dtb, alog, d_exp, nw)


def _bdot(a, b, ca, cb):
    return lax.dot_general(a, b, (((ca,), (cb,)), ((0,), (0,))), preferred_element_type=F32)


def _hi_lo(a):
    hi = a.astype(BF16)
    return hi, (a - hi.astype(F32)).astype(BF16)


def _bdot3_shared_rhs(lhs_list, b):
    n = len(lhs_list)
    rows = lhs_list[0].shape[1]
    pieces = [_hi_lo(a) for a in lhs_list]
    bh, bl = _hi_lo(b)
    his = [p[0] for p in pieces]
    t_hi = _bdot(jnp.concatenate(his + [p[1] for p in pieces], axis=1), bh, 2, 1)
    t_lo = _bdot(jnp.concatenate(his, axis=1), bl, 2, 1) if n > 1 else _bdot(his[0], bl, 2, 1)
    out = []
    for i in range(n):
        sl = lambda t, j: t[:, j * rows:(j + 1) * rows]
        out.append(sl(t_hi, i) + (sl(t_hi, n + i) + sl(t_lo, i)))
    return out


def _gdn_kernel(seq_ref, start_ref, end_ref,
                q_ref, k_ref, v_ref, z_ref, sm_ref, cin_ref, sin_ref, cw_ref,
                dtb_ref, alog_ref, nw_ref,
                y_ref, sout_ref,
                ext_scr, s_scr, *, lane_beta, lane_a):
    hb_i = pl.program_id(0)
    c = pl.program_id(1)
    hb = q_ref.shape[0]
    L = q_ref.shape[1]
    is_start = start_ref[c] == 1

    @pl.when(is_start)
    def _():
        s_scr[...] = sin_ref[0]

    u = jnp.concatenate([q_ref[...], k_ref[...], v_ref[...]], axis=0)
    uc = _silu(_causal_conv(u, ext_scr, lambda: cin_ref[0, 0], lambda j: cw_ref[0, j], is_start))
    q = uc[:hb]
    k = uc[hb:2 * hb]
    v = uc[2 * hb:]
    q = q * (lax.rsqrt(jnp.sum(q * q, axis=-1, keepdims=True) + EPS) * (GDN_D ** -0.5))
    k = k * lax.rsqrt(jnp.sum(k * k, axis=-1, keepdims=True) + EPS)

    sm = sm_ref[0]
    beta_all = jax.nn.sigmoid(sm)
    g_all = -jnp.exp(alog_ref[...]) * _softplus(sm + dtb_ref[...])
    ti = lax.broadcasted_iota(I32, (L, L), 0)
    si = lax.broadcasted_iota(I32, (L, L), 1)
    incl = ti >= si
    strict = ti > si
    gam_all = _cumsum_rows(g_all, incl)
    ej = lax.broadcasted_iota(I32, (LANE, hb * LANE), 0)
    ec = jnp.right_shift(lax.broadcasted_iota(I32, (LANE, hb * LANE), 1), 7) + hb_i * hb
    betax = _select_dot(beta_all, ej == ec + lane_beta)
    gamx = _select_dot(gam_all, ej == ec + lane_a)
    beta_c = jnp.stack([betax[:, h * LANE:(h + 1) * LANE] for h in range(hb)])
    gam_c = jnp.stack([gamx[:, h * LANE:(h + 1) * LANE] for h in range(hb)])

    t2 = lax.broadcasted_iota(I32, (L, LANE), 0)
    l2 = lax.broadcasted_iota(I32, (L, LANE), 1)
    diag2 = (t2 == l2).astype(F32)
    gam_r = jnp.sum(gam_c * diag2, axis=1, keepdims=True)[:, :, :L]
    gam_t = gam_c[:, :, :L]
    gam_m = jnp.exp(jnp.where(incl, gam_t - gam_r, NEG_BIG))

    kb = k.astype(BF16)
    kk = _bdot(kb, kb, 2, 2)
    a_mat = jnp.where(strict, beta_c[:, :, :L] * kk * gam_m, 0.0)
    n_pow = -a_mat
    x_inv = jnp.where(ti == si, 1.0, 0.0) + n_pow
    (n_pow,) = _bdot3_shared_rhs([n_pow], n_pow)
    span = 4
    while span < L:
        xp, n_next = _bdot3_shared_rhs([x_inv, n_pow], n_pow)
        x_inv = x_inv + xp
        n_pow = n_next
        span *= 2
    x_inv = x_inv + _bdot3_shared_rhs([x_inv], n_pow)[0]

    eg = jnp.exp(gam_c)
    rhs = jnp.concatenate([v * beta_c, k * (beta_c * eg)], axis=-1)
    (sol,) = _bdot3_shared_rhs([x_inv], rhs)
    u_ = sol[:, :, :GDN_D]
    w_ = sol[:, :, GDN_D:]
    s_prev = s_scr[...]
    s_b = s_prev.astype(BF16)
    v_new = u_ - _bdot(w_.astype(BF16), s_b, 2, 1)
    vn_b = v_new.astype(BF16)
    qk = _bdot(q.astype(BF16), kb, 2, 2) * gam_m
    o = _bdot((q * eg).astype(BF16), s_b, 2, 1) + _bdot(qk.astype(BF16), vn_b, 2, 1)
    gl = gam_c[:, L - 1:L, :]
    kt = (k * jnp.exp(gl - gam_c)).astype(BF16)
    for h in range(hb):
        upd = lax.dot_general(kt[h], vn_b[h], (((0,), (0,)), ((), ())), preferred_element_type=F32)
        s_scr[h] = s_prev[h] * jnp.exp(gl[h]) + upd

    o = o * lax.rsqrt(jnp.mean(o * o, axis=-1, keepdims=True) + EPS)
    o = (o * nw_ref[...]) * _silu(z_ref[...])
    for h in range(hb):
        y_ref[:, h * LANE:(h + 1) * LANE] = o[h].astype(y_ref.dtype)

    @pl.when(end_ref[c] == 1)
    def _():
        sout_ref[0] = s_scr[...]


def _gdn_mixer(proj, seq_id, start, end, conv_in, state_in, conv_w, dtb, alog, nw,
               t, heads, cb_q, cb_z, cb_sm, lane_beta, lane_a):
    hb = min(GDN_HB, heads)
    assert heads % hb == 0 and cb_q % hb == 0 and cb_z % hb == 0
    nhb = heads // hb
    nc = t // CHUNK
    nseq = state_in.shape[0]
    L = CHUNK

    def im(f):
        return lambda h, c, s, st, en: f(h, c, s)

    grid_spec = pltpu.PrefetchScalarGridSpec(
        num_scalar_prefetch=3,
        grid=(nhb, nc),
        in_specs=[
            pl.BlockSpec((hb, L, LANE), im(lambda h, c, s: (cb_q // hb + h, c, 0))),
            pl.BlockSpec((hb, L, LANE), im(lambda h, c, s: ((cb_q + heads) // hb + h, c, 0))),
            pl.BlockSpec((hb, L, LANE), im(lambda h, c, s: ((cb_q + 2 * heads) // hb + h, c, 0))),
            pl.BlockSpec((hb, L, LANE), im(lambda h, c, s: (cb_z // hb + h, c, 0))),
            pl.BlockSpec((1, L, LANE), im(lambda h, c, s: (cb_sm, c, 0))),
            pl.BlockSpec((1, 1, 3 * hb, CONV_K - 1, LANE), im(lambda h, c, s: (s[c], h, 0, 0, 0))),
            pl.BlockSpec((1, hb, GDN_D, GDN_D), im(lambda h, c, s: (s[c], h, 0, 0))),
            pl.BlockSpec((1, CONV_K, 3 * hb, 1, LANE), im(lambda h, c, s: (h, 0, 0, 0, 0))),
            pl.BlockSpec((1, LANE), im(lambda h, c, s: (0, 0))),
            pl.BlockSpec((1, LANE), im(lambda h, c, s: (0, 0))),
            pl.BlockSpec((1, LANE), im(lambda h, c, s: (0, 0))),
        ],
        out_specs=[
            pl.BlockSpec((L, hb * LANE), im(lambda h, c, s: (c, h))),
            pl.BlockSpec((1, hb, GDN_D, GDN_D), im(lambda h, c, s: (s[c], h, 0, 0))),
        ],
        scratch_shapes=[pltpu.VMEM((3 * hb, L + SUBLANE, LANE), F32), pltpu.VMEM((hb, GDN_D, GDN_D), F32)],
    )
    return pl.pallas_call(
        functools.partial(_gdn_kernel, lane_beta=lane_beta, lane_a=lane_a),
        grid_spec=grid_spec,
        out_shape=[jax.ShapeDtypeStruct((t, heads * GDN_D), BF16),
                   jax.ShapeDtypeStruct((nseq, heads, GDN_D, GDN_D), F32)],
        compiler_params=_params(("arbitrary", "arbitrary")),
        name="gdn_mixer",
    )(seq_id, start, end, proj, proj, proj, proj, proj, conv_in, state_in, conv_w, dtb, alog, nw)


def _outproj_kernel(a1_ref, a2_ref, w1_ref, w2_ref, xa_ref, xb_ref, o_ref, *, na):
    acc = _dot(a1_ref[...], w1_ref[...]) + _dot(a2_ref[...], w2_ref[...])

    @pl.when(pl.program_id(0) < na)
    def _():
        o_ref[...] = xa_ref[...] + acc

    @pl.when(pl.program_id(0) >= na)
    def _():
        o_ref[...] = xb_ref[...] + acc


def _outproj(a1, a2, w1, w2, xa, xb, tm, tn):
    t, k1 = a1.shape
    k2 = a2.shape[1]
    d = w1.shape[1]
    na = xa.shape[0] // tm
    spec_a, spec_b = _two_source_specs((tm, tn), na, col=d // tn - 1)
    return pl.pallas_call(
        functools.partial(_outproj_kernel, na=na),
        grid=(t // tm, d // tn),
        in_specs=[pl.BlockSpec((tm, k1), lambda i, j: (i, 0)),
                  pl.BlockSpec((tm, k2), lambda i, j: (i, 0)),
                  pl.BlockSpec((k1, tn), lambda i, j: (0, j)),
                  pl.BlockSpec((k2, tn), lambda i, j: (0, j)),
                  spec_a, spec_b],
        out_specs=pl.BlockSpec((tm, tn), lambda i, j: (i, j)),
        out_shape=jax.ShapeDtypeStruct((t, d), F32),
        compiler_params=_params(("arbitrary", "arbitrary")),
        name="outproj",
    )(a1, a2, w1, w2, xa, xb)


def _router_kernel(x_ref, nw_ref, wr_ref, br_ref, idx_ref, gate_ref, rank_ref, run_scr):
    @pl.when(pl.program_id(0) == 0)
    def _():
        run_scr[...] = jnp.zeros(run_scr.shape, run_scr.dtype)

    x = x_ref[...]
    h = (x * lax.rsqrt(jnp.mean(x * x, axis=-1, keepdims=True) + EPS)) * nw_ref[...]
    logits = _dot(h, wr_ref[...], precision=HIGHEST) + br_ref[...]
    lane = lax.broadcasted_iota(I32, logits.shape, 1)
    vals = logits
    idx_out = jnp.zeros(logits.shape, I32)
    top = []
    sels = []
    for kk in range(TOP_K):
        m = jnp.max(vals, axis=-1, keepdims=True)
        sel = jnp.min(jnp.where(vals == m, lane, LANE), axis=-1, keepdims=True)
        idx_out = jnp.where(lane == kk, sel, idx_out)
        top.append(m)
        sels.append(sel)
        vals = jnp.where(lane == sel, -jnp.inf, vals)

    tm = logits.shape[0]
    picked = jnp.zeros(logits.shape, F32)
    for sel in sels:
        picked = picked + (lane == sel).astype(F32)
    ti = lax.broadcasted_iota(I32, (tm, tm), 0)
    si = lax.broadcasted_iota(I32, (tm, tm), 1)
    before = _dot((ti > si).astype(BF16), picked.astype(BF16)) + run_scr[...]
    rank_out = jnp.zeros(logits.shape, I32)
    for kk in range(TOP_K):
        r = jnp.sum(jnp.where(lane == sels[kk], before, 0.0), axis=-1, keepdims=True)
        rank_out = jnp.where(lane == kk, r.astype(I32), rank_out)
    rank_ref[...] = rank_out
    run_scr[...] = run_scr[...] + jnp.sum(picked, axis=0, keepdims=True)
    es = [jnp.exp(m - top[0]) for m in top]
    den = es[0]
    for e in es[1:]:
        den = den + e
    gate_out = jnp.zeros(logits.shape, F32)
    for kk in range(TOP_K):
        gate_out = jnp.where(lane == kk, es[kk] / den, gate_out)
    idx_ref[...] = idx_out
    gate_ref[...] = gate_out


def _router(x1, nw, wr, br, tm):
    t, d = x1.shape
    return pl.pallas_call(
        _router_kernel,
        grid=(t // tm,),
        in_specs=[pl.BlockSpec((tm, d), lambda i: (i, 0)),
                  pl.BlockSpec((1, d), lambda i: (0, 0)),
                  pl.BlockSpec((d, LANE), lambda i: (0, 0)),
                  pl.BlockSpec((1, LANE), lambda i: (0, 0))],
        out_specs=[pl.BlockSpec((tm, LANE), lambda i: (i, 0)) for _ in range(3)],
        out_shape=[jax.ShapeDtypeStruct((t, LANE), I32), jax.ShapeDtypeStruct((t, LANE), F32),
                   jax.ShapeDtypeStruct((t, LANE), I32)],
        scratch_shapes=[pltpu.VMEM((1, LANE), F32)],
        compiler_params=_params(("arbitrary",)),
        name="router",
    )(x1, nw, wr, br)


def _row_copy(src_hbm, dst_vmem, sem, src_row, dst_row):
    return pltpu.make_async_copy(src_hbm.at[pl.ds(src_row, 1), :], dst_vmem.at[pl.ds(dst_row, 1), :], sem)


def _gather_rows(idx_ref, idx0, src_hbm, dst_vmem, sem):
    def body(r, carry):
        _row_copy(src_hbm, dst_vmem, sem, idx_ref[idx0 + r], r).start()
        return carry
    lax.fori_loop(0, dst_vmem.shape[0], body, 0, unroll=GATHER_UNROLL)


def _drain_rows(src_hbm, dst_vmem, sem):
    pltpu.make_async_copy(src_hbm.at[pl.ds(0, dst_vmem.shape[0]), :], dst_vmem, sem).wait()


def _dispatch_kernel(tok_ref, nused_ref, x_hbm, nw_ref, o_ref, buf, sem):
    b = pl.program_id(0)
    bm = o_ref.shape[0]
    n_used = nused_ref[0]
    slot = lax.rem(b, 2)

    @pl.when(jnp.logical_and(b == 0, n_used > 0))
    def _():
        _gather_rows(tok_ref, 0, x_hbm, buf.at[0], sem.at[0])

    @pl.when(b + 1 < n_used)
    def _():
        _gather_rows(tok_ref, (b + 1) * bm, x_hbm, buf.at[1 - slot], sem.at[1 - slot])

    @pl.when(b < n_used)
    def _():
        _drain_rows(x_hbm, buf.at[slot], sem.at[slot])

        x = buf[slot]
        h = (x * lax.rsqrt(jnp.mean(x * x, axis=-1, keepdims=True) + EPS)) * nw_ref[...]
        o_ref[...] = h.astype(o_ref.dtype)

    @pl.when(b >= n_used)
    def _():
        o_ref[...] = jnp.zeros(o_ref.shape, o_ref.dtype)


def _dispatch(slot_tok, n_used, x1, nw, nblk):
    t, d = x1.shape
    grid_spec = pltpu.PrefetchScalarGridSpec(
        num_scalar_prefetch=2,
        grid=(nblk,),
        in_specs=[pl.BlockSpec(memory_space=pl.ANY),
                  pl.BlockSpec((1, d), lambda b, *_: (0, 0))],
        out_specs=pl.BlockSpec((MOE_BM, d), lambda b, *_: (b, 0)),
        scratch_shapes=[pltpu.VMEM((2, MOE_BM, d), F32), pltpu.SemaphoreType.DMA((2,))],
    )
    return pl.pallas_call(
        _dispatch_kernel,
        grid_spec=grid_spec,
        out_shape=jax.ShapeDtypeStruct((nblk * MOE_BM, d), BF16),
        compiler_params=_params(("arbitrary",)),
        name="moe_dispatch",
    )(slot_tok, n_used, x1, nw)


def _resident_rows_matmul(blk0_ref, nblk_ref, nused_ref, src_hbm, dst_hbm,
                          xbuf, obuf, zbuf, sem_in, sem_out, sem_z, compute):
    s = pl.program_id(0)
    n = pl.program_id(1)
    nt = pl.num_programs(1)
    bm = MOE_BM
    tn = obuf.shape[2]
    n_blocks = nblk_ref[s]
    first = blk0_ref[s]
    lin = s * nt + n
    slot = lax.rem(lin, 2)
    col = pl.multiple_of(n * tn, tn)

    def hbm_rows(blk):
        return pl.ds(pl.multiple_of(blk * bm, bm), bm)

    def x_copy(i):
        return pltpu.make_async_copy(src_hbm.at[hbm_rows(first + i), :], xbuf.at[pl.ds(i * bm, bm), :], sem_in)

    def o_copy(slot_, blk, i):
        return pltpu.make_async_copy(obuf.at[slot_, pl.ds(i * bm, bm), :],
                                     dst_hbm.at[hbm_rows(blk), pl.ds(col, tn)], sem_out.at[slot_])

    def for_blocks(count, fn):
        for i in range(MOE_SB_BLOCKS):
            pl.when(i < count)(functools.partial(fn, i))

    def wait_out(step, slot_):
        for_blocks(nblk_ref[lax.div(step, nt)], lambda i: o_copy(slot_, 0, i).wait())

    @pl.when(lin == 0)
    def _():
        zbuf[...] = jnp.zeros(zbuf.shape, zbuf.dtype)

    @pl.when(n == 0)
    def _():
        for_blocks(n_blocks, lambda i: x_copy(i).start())
        for_blocks(n_blocks, lambda i: x_copy(i).wait())

    @pl.when(lin >= 2)
    def _():
        wait_out(lin - 2, slot)

    def matmul(blk_off, m):
        rows = pl.ds(pl.multiple_of(blk_off * bm, bm), m * bm)
        obuf[slot, rows, :] = compute(lambda: xbuf[rows, :])

    @pl.when(n_blocks == MOE_SB_BLOCKS)
    def _():
        matmul(0, MOE_SB_BLOCKS)

    @pl.when(jnp.logical_and(n_blocks > 0, n_blocks < MOE_SB_BLOCKS))
    def _():
        top = MOE_PARTIAL_HEIGHTS[0]
        n_top = lax.div(n_blocks, top)
        lax.fori_loop(0, n_top, lambda i, c: (matmul(i * top, top), c)[1], 0)
        done = n_top * top
        for m in MOE_PARTIAL_HEIGHTS[1:]:
            has = lax.rem(lax.div(n_blocks, m), 2) == 1
            pl.when(has)(functools.partial(matmul, done, m))
            done = done + jnp.where(has, m, 0)
    for_blocks(n_blocks, lambda i: o_copy(slot, first + i, i).start())

    @pl.when(lin == pl.num_programs(0) * nt - 1)
    def _():
        for_blocks(n_blocks, lambda i: o_copy(slot, 0, i).wait())

        @pl.when(lin >= 1)
        def _():
            wait_out(lin - 1, 1 - slot)

    @pl.when(s == pl.num_programs(0) - 1)
    def _():
        n_used = nused_ref[0]
        n_spare = dst_hbm.shape[0] // bm - n_used

        def z_copy(i):
            return pltpu.make_async_copy(zbuf, dst_hbm.at[hbm_rows(n_used + i), pl.ds(col, tn)], sem_z)
        lax.fori_loop(0, n_spare, lambda i, c: (z_copy(i).start(), c)[1], 0)
        lax.fori_loop(0, n_spare, lambda i, c: (z_copy(i).wait(), c)[1], 0)


def _resident_call(body, sb_e, sb_blk0, sb_nblk, n_used, src, weights, biases, dff_out, tn, out_dtype, name):
    p, k = src.shape
    nsb = sb_e.shape[0]
    nt = dff_out // tn

    def w_map(s, n, e_ref, b0_ref, nb_ref, nu_ref):
        return (e_ref[s], 0, jnp.where(nb_ref[s] > 0, n, nt - 1))

    grid_spec = pltpu.PrefetchScalarGridSpec(
        num_scalar_prefetch=4,
        grid=(nsb, nt),
        in_specs=([pl.BlockSpec(memory_space=pl.ANY)]
                  + [pl.BlockSpec((1, k, tn), w_map) for _ in weights]
                  + [pl.BlockSpec((1, 1, tn), w_map) for _ in biases]),
        out_specs=pl.BlockSpec(memory_space=pl.ANY),
        scratch_shapes=[pltpu.VMEM((MOE_SB_BLOCKS * MOE_BM, k), BF16),
                        pltpu.VMEM((2, MOE_SB_BLOCKS * MOE_BM, tn), out_dtype),
                        pltpu.VMEM((MOE_BM, tn), out_dtype),
                        pltpu.SemaphoreType.DMA(()), pltpu.SemaphoreType.DMA((2,)), pltpu.SemaphoreType.DMA(())],
    )
    return pl.pallas_call(
        body,
        grid_spec=grid_spec,
        out_shape=jax.ShapeDtypeStruct((p, dff_out), out_dtype),
        compiler_params=_params(("arbitrary", "arbitrary")),
        name=name,
    )(sb_e, sb_blk0, sb_nblk, n_used, src, *weights, *biases)


def _gateup_kernel(e_ref, blk0_ref, nblk_ref, nused_ref, x_hbm, wg_ref, wu_ref, bg_ref, bu_ref, act_hbm,
                   xbuf, obuf, zbuf, sem_in, sem_out, sem_z):
    def compute(x):
        gate = jnp.minimum(_dot(x(), wg_ref[0].astype(BF16)) + bg_ref[0], SWIGLU_LIMIT)
        up = jnp.clip(_dot(x(), wu_ref[0].astype(BF16)) + bu_ref[0], -SWIGLU_LIMIT, SWIGLU_LIMIT)
        act = gate * jax.nn.sigmoid(SWIGLU_ALPHA * gate) * (up + 1.0)
        return act.astype(obuf.dtype)

    _resident_rows_matmul(blk0_ref, nblk_ref, nused_ref, x_hbm, act_hbm,
                          xbuf, obuf, zbuf, sem_in, sem_out, sem_z, compute)


def _down_kernel(e_ref, blk0_ref, nblk_ref, nused_ref, a_hbm, wd_ref, bd_ref, y_hbm,
                 xbuf, obuf, zbuf, sem_in, sem_out, sem_z):
    def compute(a):
        return _dot(a(), wd_ref[0].astype(BF16)) + bd_ref[0]

    _resident_rows_matmul(blk0_ref, nblk_ref, nused_ref, a_hbm, y_hbm,
                          xbuf, obuf, zbuf, sem_in, sem_out, sem_z, compute)


def _combine_kernel(pos_ref, ys_hbm, x_ref, g_ref, nw_ref, o_ref, buf, sem, *, tile0):
    i = pl.program_id(0)
    tt = o_ref.shape[0]
    n = TOP_K * tt
    slot = lax.rem(i, 2)

    def issue(tile, slot_):
        _gather_rows(pos_ref, (tile0 + tile) * n, ys_hbm, buf.at[slot_], sem.at[slot_])

    @pl.when(i == 0)
    def _():
        issue(0, 0)

    @pl.when(i + 1 < pl.num_programs(0))
    def _():
        issue(i + 1, 1 - slot)

    _drain_rows(ys_hbm, buf.at[slot], sem.at[slot])

    g = g_ref[...]
    acc = x_ref[...]
    for kk in range(TOP_K):
        acc = acc + buf[slot, kk * tt:(kk + 1) * tt, :] * g[:, kk:kk + 1]
    y = acc * lax.rsqrt(jnp.mean(acc * acc, axis=-1, keepdims=True) + EPS)
    o_ref[...] = y * nw_ref[...]


def _combine(pos_tiles, ys, x1, gates, nw, tt, tile0, n_tiles):
    d = x1.shape[1]
    grid_spec = pltpu.PrefetchScalarGridSpec(
        num_scalar_prefetch=1,
        grid=(n_tiles,),
        in_specs=[pl.BlockSpec(memory_space=pl.ANY),
                  pl.BlockSpec((tt, d), lambda i, pos: (tile0 + i, 0)),
                  pl.BlockSpec((tt, LANE), lambda i, pos: (tile0 + i, 0)),
                  pl.BlockSpec((1, d), lambda i, pos: (0, 0))],
        out_specs=pl.BlockSpec((tt, d), lambda i, pos: (i, 0)),
        scratch_shapes=[pltpu.VMEM((2, TOP_K * tt, d), F32), pltpu.SemaphoreType.DMA((2,))],
    )
    return pl.pallas_call(
        functools.partial(_combine_kernel, tile0=tile0),
        grid_spec=grid_spec,
        out_shape=jax.ShapeDtypeStruct((n_tiles * tt, d), F32),
        compiler_params=_params(("arbitrary",)),
        name="moe_combine",
    )(pos_tiles, ys, x1, gates, nw)


def _pad_lanes(v, offset):
    out = jnp.zeros((LANE,), F32)
    return lax.dynamic_update_slice(out, v.astype(F32), (offset,)).reshape(1, LANE)


def _ssd_group_layout(a, width):
    lead = a.shape[:-2]
    rows = a.shape[-2]
    rb = width // LANE // SSD_G
    xs = a[..., :width].reshape(*lead, rows, SSD_G, rb, LANE)
    bs = a[..., width:width + SSD_G * SSD_N].reshape(*lead, rows, SSD_G, 1, LANE)
    cs = a[..., width + SSD_G * SSD_N:].reshape(*lead, rows, SSD_G, 1, LANE)
    cat = jnp.concatenate([xs, bs, cs], axis=-2)
    n = cat.ndim
    return jnp.moveaxis(cat, n - 4, n - 2)


def _gdn_block_layout(a, heads, hb):
    lead = a.shape[:-2]
    rows = a.shape[-2]
    r = a.reshape(*lead, rows, 3, heads // hb, hb, LANE)
    n = r.ndim
    r = jnp.moveaxis(r, n - 5, n - 2)
    r = jnp.moveaxis(r, n - 5, n - 4)
    return r.reshape(*lead, heads // hb, 3 * hb, rows, LANE)


def _pick_tn_cb(ncb):
    return INPROJ_TN_CB, (-ncb) % INPROJ_TN_CB


def kernel(x_prompt, x_sample, state_ssd_conv, state_ssd, state_gdn_conv, state_gdn, norm_mix, w_in, ssd_conv_w,
           ssd_conv_b, ssd_dt_bias, ssd_A_log, ssd_D, ssd_norm, gdn_conv_w, gdn_dt_bias, gdn_A_log, gdn_norm, w_out,
           norm_ffn, w_router, b_router, w_gate, b_gate, w_up, b_up, w_down, b_down, norm_final):
    assert w_in.shape[0] == 1, "single layer"
    nb_p, seq_p, d = x_prompt.shape
    nb_s, seq_s, _ = x_sample.shape
    assert seq_p % CHUNK == 0 and seq_s % CHUNK == 0
    ssd_heads = d // SSD_P
    ssd_w = ssd_heads * SSD_P
    ssd_cs = ssd_w + 2 * SSD_G * SSD_N
    gdn_heads = d // GDN_D
    gdn_w = gdn_heads * GDN_D
    t_p = nb_p * seq_p
    t = t_p + nb_s * seq_s
    nseq = nb_p + nb_s

    x_p = x_prompt.reshape(t_p, d)
    x_s = x_sample.reshape(nb_s * seq_s, d)
    seq_len = [seq_p] * nb_p + [seq_s] * nb_s
    seq_id, start, end = [], [], []
    for s, n in enumerate(seq_len):
        for cidx in range(n // CHUNK):
            seq_id.append(s)
            start.append(int(cidx == 0))
            end.append(int(cidx == n // CHUNK - 1))
    seq_id = jnp.asarray(np.array(seq_id, np.int32))
    start = jnp.asarray(np.array(start, np.int32))
    end = jnp.asarray(np.array(end, np.int32))

    o_z, o_xbc, o_dt = 0, ssd_w, ssd_w + ssd_cs
    o_qkv = o_dt + ssd_heads
    o_zg = o_qkv + 3 * gdn_w
    o_b = o_zg + gdn_w
    o_a = o_b + gdn_heads
    n_small = ssd_heads + 2 * gdn_heads
    assert n_small <= LANE
    cb_q = 0
    cb_zg = cb_q + 3 * gdn_w // LANE
    cb_z = cb_zg + gdn_w // LANE
    cb_x = cb_z + ssd_w // LANE
    cb_sm = cb_x + ssd_cs // LANE
    ncb = cb_sm + 1
    tn_cb, pad_cb = _pick_tn_cb(ncb)
    assert o_zg == o_qkv + 3 * gdn_w and o_xbc == o_z + ssd_w and o_a == o_b + gdn_heads
    w_perm = _permute_cast_weight(w_in[0], o_qkv, 4 * gdn_w, ssd_w + ssd_cs, o_dt, o_b, ssd_heads, n_small,
                                  (ncb + pad_cb) * LANE)
    lane_beta = ssd_heads
    lane_a = ssd_heads + gdn_heads

    tm_big = _row_tile(t, 1056)
    tm_mid = _row_tile(t, 528)
    tm_src = _row_tile(int(np.gcd(t_p, t - t_p)), 512)
    h = _rmsnorm_cast(x_p, x_s, norm_mix[0], tm_src)
    proj = _inproj(h, w_perm, tm_big, tn_cb)

    def with_zero_prompt(a):
        return jnp.concatenate([jnp.zeros((nb_p,) + a.shape[1:], a.dtype), a], axis=0)

    ssd_conv0 = _ssd_group_layout(with_zero_prompt(state_ssd_conv[0]), ssd_w)
    gdn_hb = min(GDN_HB, gdn_heads)
    gdn_conv0 = _gdn_block_layout(with_zero_prompt(state_gdn_conv[0]), gdn_heads, gdn_hb)
    s0 = with_zero_prompt(state_ssd[0])
    ssd_s0 = s0.reshape(nseq, ssd_heads // 2, 2, SSD_P, SSD_N).transpose(0, 1, 4, 2, 3).reshape(
        nseq, ssd_heads // 2, SSD_N, LANE)
    gdn_s0 = with_zero_prompt(state_gdn[0])

    ssd_cw = _ssd_group_layout(ssd_conv_w[0][None], ssd_w)[0]
    ssd_cw = jnp.swapaxes(ssd_cw, 1, 2)[:, :, :, None, :]
    ssd_cb = _ssd_group_layout(ssd_conv_b[0][None, None], ssd_w)[0]
    d_exp = jnp.repeat(ssd_D[0], SSD_P).reshape(ssd_w // LANE, 1, LANE)
    ssd_nw = ssd_norm[0].reshape(ssd_w // LANE, 1, LANE)
    y_ssd, ssd_s = _ssd_mixer(
        proj, seq_id, start, end, ssd_conv0, ssd_s0, ssd_cw, ssd_cb,
        _pad_lanes(ssd_dt_bias[0], 0), _pad_lanes(ssd_A_log[0], 0), d_exp, ssd_nw,
        t, ssd_w, cb_z, cb_x, cb_x + ssd_w // LANE, cb_x + ssd_w // LANE + SSD_G, cb_sm)

    gdn_cw = _gdn_block_layout(gdn_conv_w[0][None], gdn_heads, gdn_hb)[0]
    gdn_cw = jnp.swapaxes(gdn_cw, 1, 2)[:, :, :, None, :]
    y_gdn, gdn_s = _gdn_mixer(
        proj, seq_id, start, end, gdn_conv0, gdn_s0, gdn_cw,
        _pad_lanes(gdn_dt_bias[0], lane_a), _pad_lanes(gdn_A_log[0], lane_a), gdn_norm[0].reshape(1, LANE),
        t, gdn_heads, cb_q, cb_zg, cb_sm, lane_beta, lane_a)

    w_o = w_out[0].astype(BF16)
    x1 = _outproj(y_ssd, y_gdn, w_o[:ssd_w], w_o[ssd_w:], x_p, x_s, tm_src, min(512, d))

    wr = jnp.concatenate([w_router[0], jnp.zeros((d, LANE - N_EXPERTS), F32)], axis=1)
    br = jnp.concatenate([b_router[0], jnp.full((LANE - N_EXPERTS,), NEG_BIG, F32)]).reshape(1, LANE)
    nffn = norm_ffn[0].reshape(1, d)
    idx_pad, gate_pad, rank_pad = _router(x1, nffn, wr, br, tm_mid)
    top_idx = idx_pad[:, :TOP_K]
    tk = t * TOP_K
    flat_e = top_idx.reshape(tk)
    order = jnp.argsort(flat_e).astype(I32)
    counts = jnp.sum((flat_e[:, None] == jnp.arange(N_EXPERTS, dtype=I32)[None, :]).astype(I32), axis=0)
    starts = jnp.cumsum(counts) - counts
    pcounts = (counts + MOE_BM - 1) // MOE_BM * MOE_BM
    pends = jnp.cumsum(pcounts)
    pstarts = pends - pcounts
    nblk = -(-tk // MOE_BM) + N_EXPERTS
    pos = (pstarts[flat_e] + rank_pad[:, :TOP_K].reshape(tk)).astype(I32)

    def count_le(ends, v):
        return jnp.sum((ends[None, :] <= v[:, None]).astype(I32), axis=1)

    blk_e = jnp.minimum(count_le(pends, jnp.arange(nblk, dtype=I32) * MOE_BM), N_EXPERTS - 1).astype(I32)
    n_used = (pends[-1] // MOE_BM).astype(I32).reshape(1)
    blk0 = (pstarts // MOE_BM).astype(I32)
    nblk_e = (pcounts // MOE_BM).astype(I32)
    slot = jnp.arange(nblk * MOE_BM, dtype=I32)
    slot_e = jnp.repeat(blk_e, MOE_BM)
    slot_off = slot - pstarts[slot_e].astype(I32)
    slot_src = jnp.clip(starts[slot_e].astype(I32) + slot_off, 0, tk - 1)
    slot_tok = jnp.where(slot_off < counts[slot_e], (order // TOP_K)[slot_src], 0).astype(I32)
    nsb_e = (nblk_e + MOE_SB_BLOCKS - 1) // MOE_SB_BLOCKS
    sb_ends = jnp.cumsum(nsb_e)
    sb_starts = sb_ends - nsb_e
    n_sb = sb_ends[-1]
    sb_i = jnp.arange(N_EXPERTS + nblk // MOE_SB_BLOCKS, dtype=I32)
    sb_c = jnp.minimum(sb_i, n_sb - 1)
    sb_e = jnp.minimum(count_le(sb_ends, sb_c), N_EXPERTS - 1).astype(I32)
    sb_j = sb_c - sb_starts[sb_e]
    sb_blk0 = (blk0[sb_e] + sb_j * MOE_SB_BLOCKS).astype(I32)
    sb_nblk = jnp.where(sb_i < n_sb, jnp.clip(nblk_e[sb_e] - sb_j * MOE_SB_BLOCKS, 0, MOE_SB_BLOCKS), 0).astype(I32)

    xs = _dispatch(slot_tok, n_used, x1, nffn, nblk)
    act = _resident_call(_gateup_kernel, sb_e, sb_blk0, sb_nblk, n_used, xs, (w_gate[0], w_up[0]),
                         (b_gate[0][:, None, :], b_up[0][:, None, :]), w_gate.shape[3], min(MOE_TN, d), BF16,
                         "moe_gateup")
    ys = _resident_call(_down_kernel, sb_e, sb_blk0, sb_nblk, n_used, act, (w_down[0],),
                        (b_down[0][:, None, :],), d, min(MOE_TN_DOWN, d), F32, "moe_down")

    tt = _row_tile(int(np.gcd(t_p, t - t_p)), 2 * CHUNK, mult=SUBLANE)
    pos_tiles = pos.reshape(t // tt, tt, TOP_K).transpose(0, 2, 1).reshape(tk)
    nfin = norm_final.reshape(1, d)
    y_prompt = _combine(pos_tiles, ys, x1, gate_pad, nfin, tt, 0, t_p // tt).reshape(nb_p, seq_p, d)
    y_sample = _combine(pos_tiles, ys, x1, gate_pad, nfin, tt, t_p // tt, (t - t_p) // tt).reshape(nb_s, seq_s, d)

    def last_rows(cb0, ncols):
        nblk_c = ncols // LANE
        ends = np.cumsum(seq_len)
        rows = jnp.concatenate(
            [lax.slice(proj, (cb0, int(e) - (CONV_K - 1), 0), (cb0 + nblk_c, int(e), LANE)) for e in ends], axis=1)
        a = rows.reshape(nblk_c, nseq, CONV_K - 1, LANE).transpose(1, 2, 0, 3).reshape(nseq, CONV_K - 1, ncols)
        return a[:nb_p][None], a[nb_p:][None]

    ssd_conv_p, ssd_conv_s = last_rows(cb_x, ssd_cs)
    gdn_conv_p, gdn_conv_s = last_rows(cb_q, 3 * gdn_w)
    ssd_state = ssd_s.reshape(nseq, ssd_heads // 2, SSD_N, 2, SSD_P).transpose(0, 1, 3, 4, 2).reshape(
        nseq, ssd_heads, SSD_P, SSD_N)
    return (y_prompt, y_sample,
            ssd_conv_p, ssd_state[:nb_p][None], gdn_conv_p, gdn_s[:nb_p][None],
            ssd_conv_s, ssd_state[nb_p:][None], gdn_conv_s, gdn_s[nb_p:][None])
```
